```python
import math
import jax, jax.numpy as jnp
from jax import lax
import numpy as np

D_MODEL = 1024
BATCH = 16
SEQ = 256
DEPTH = 4
DEC_BATCH = 2
DEC_SEQ = 2048
PAST_LEN = 256

GRID_W = 64
NORM_EPS = 1e-6
POOL_WIDTH = 512
POOL_WINDOWS = (2, 4, 8, 16)
POOL_GROUPS = len(POOL_WINDOWS)
POOL_GC = POOL_WIDTH // POOL_GROUPS
DA_HEADS = 4
DA_HEAD_DIM = 64
DA_VDIM = 2 * DA_HEAD_DIM
DA_WIDTH = DA_HEADS * DA_VDIM
ROPE_BASE = 10000.0
Q_BLOCK = 128
HY_WIDTH = 512
HY_EMB = 33
HY_BANDS = (HY_EMB - 1) // 2
HY_HIDDEN = 64
HY_FAST = 0.3
HY_SLOW = 1.5
HY_TARGET = 1e-2
N_BRANCH = 3
D_IN = POOL_WIDTH + 3 * DA_WIDTH + 3 * HY_WIDTH
N_EXPERTS = 32
TOP_K = 4
D_FF = 1024
SWIGLU_ALPHA = 1.702
SWIGLU_LIMIT = 7.0
N_MOD = 6

kernel_name = "hybrid_diffusion_prefix_trunk_step"

F32 = jnp.float32


def rmsnorm(x, g):
    x32 = x.astype(F32)
    y = x32 * lax.rsqrt(jnp.mean(x32 * x32, axis=-1, keepdims=True) + NORM_EPS)
    return (y * g.astype(F32)).astype(x.dtype)


def axial_rope_tables(L):
    rows = L // GRID_W
    row = jnp.repeat(jnp.arange(rows), GRID_W).astype(F32)
    col = jnp.tile(jnp.arange(GRID_W), rows).astype(F32)
    ax = DA_HEAD_DIM // 2
    inv = ROPE_BASE ** (-(jnp.arange(ax // 2, dtype=F32) * 2.0 / ax))
    ang_r = row[:, None] * inv
    ang_c = col[:, None] * inv
    ang = jnp.concatenate([ang_r, ang_r, ang_c, ang_c], axis=-1)
    return jnp.cos(ang), jnp.sin(ang)


def apply_rope(x, cos, sin):
    x32 = x.astype(F32)
    r1, r2, c1, c2 = jnp.split(x32, 4, axis=-1)
    rot = jnp.concatenate([-r2, r1, -c2, c1], axis=-1)
    out = x32 * cos[None, :, None, None, :] + rot * sin[None, :, None, None, :]
    return out.astype(x.dtype)


def diff_attend(q, k, v, lam):
    B, Lq = q.shape[0], q.shape[1]
    nb = Lq // Q_BLOCK
    qb = q.reshape(B, nb, Q_BLOCK, DA_HEADS, 2, DA_HEAD_DIM).swapaxes(0, 1)
    k32 = k.astype(F32)
    v32 = v.astype(F32)
    scale = DA_HEAD_DIM ** -0.5

    def one_block(qblk):
        s = jnp.einsum('bqhid,bkhid->ibhqk', qblk.astype(F32), k32) * scale
        p = jax.nn.softmax(s, axis=-1)
        a = p[0] - lam * p[1]
        return jnp.einsum('bhqk,bkhe->bqhe', a, v32)

    o = lax.map(one_block, qb)
    return o.swapaxes(0, 1).reshape(B, Lq, DA_HEADS, DA_VDIM)


def diff_attn_branch(u_q, u_k, u_v, lam_p, subln_g, layer_idx, rope, ctx_k, ctx_v):
    B, L, _ = u_q.shape
    q = u_q.reshape(B, L, DA_HEADS, 2, DA_HEAD_DIM)
    k = u_k.reshape(B, L, DA_HEADS, 2, DA_HEAD_DIM)
    v = u_v.reshape(B, L, DA_HEADS, DA_VDIM)
    if rope is None:
        keys, vals = k, v
    else:
        cos, sin = rope
        q = apply_rope(q, cos, sin)
        Lc = ctx_k.shape[1]
        keys = jnp.concatenate(
            [apply_rope(k, cos, sin), ctx_k.reshape(B, Lc, DA_HEADS, 2, DA_HEAD_DIM).astype(k.dtype)], axis=1)
        vals = jnp.concatenate([v, ctx_v.astype(v.dtype)], axis=1)
    lam_init = 0.8 - 0.6 * math.exp(-0.3 * layer_idx)
    lp = lam_p.astype(F32)
    lam = jnp.exp(jnp.sum(lp[0] * lp[1])) - jnp.exp(jnp.sum(lp[2] * lp[3])) + lam_init
    o = diff_attend(q, keys, vals, lam)
    o = rmsnorm(o, subln_g) * (1.0 - lam_init)
    return (o.reshape(B, L, DA_WIDTH).astype(u_q.dtype),
            k.reshape(B, L, DA_HEADS, 2 * DA_HEAD_DIM), v)


def pool_branch(u, w_pool, pool_scale):
    B, L, _ = u.shape
    u32 = u.astype(F32)
    cs = jnp.concatenate([jnp.zeros((B, 1, POOL_WIDTH), F32), lax.cumsum(u32, axis=1)], axis=1)
    t = jnp.arange(L)
    outs = []
    for g, w in enumerate(POOL_WINDOWS):
        lo = jnp.clip(t - w // 2, 0, L)
        hi = jnp.clip(t + w // 2, 0, L)
        sl = slice(g * POOL_GC, (g + 1) * POOL_GC)
        cs_g = cs[..., sl]
        s = jnp.take(cs_g, hi, axis=1) - jnp.take(cs_g, lo, axis=1)
        mean = s / (hi - lo).astype(F32)[None, :, None]
        outs.append(mean - u32[..., sl])
    d = jnp.stack(outs, axis=2)
    y = jnp.einsum('blgc,gcd->blgd', d, w_pool.astype(F32)).reshape(B, L, POOL_WIDTH)
    return (y * pool_scale.astype(F32)).astype(u.dtype)


def hyena_filter(L, w1, b1, w2, b2, w3, sin_freq):
    t = jnp.linspace(0.0, 1.0, L, dtype=F32)
    w_ang = 2.0 * math.pi * jnp.arange(L, dtype=F32) / L
    f = jnp.linspace(1e-4, HY_BANDS - 1, HY_BANDS, dtype=F32)
    z = jnp.concatenate([t[:, None], jnp.cos(w_ang[:, None] * f), -jnp.sin(w_ang[:, None] * f)], axis=-1)
    fr = sin_freq.astype(F32)
    hid = jnp.sin(fr[0] * (z @ w1.astype(F32) + b1.astype(F32)))
    hid = jnp.sin(fr[1] * (hid @ w2.astype(F32) + b2.astype(F32)))
    h = hid @ w3.astype(F32)
    deltas = jnp.linspace(math.log(HY_TARGET) / HY_FAST, math.log(HY_TARGET) / HY_SLOW, HY_WIDTH, dtype=F32)
    decay = jnp.exp(-t[:, None] * jnp.abs(deltas))
    h_fwd = h[:, :HY_WIDTH] * decay
    h_bwd = h[:, HY_WIDTH:] * decay
    return jnp.concatenate([h_fwd, jnp.zeros((1, HY_WIDTH), F32), h_bwd[1:][::-1]], axis=0)


def hyena_branch(u, conv_w, conv_b, w1, b1, w2, b2, w3, sin_freq, skip):
    B, L, _ = u.shape
    up = jnp.pad(u, ((0, 0), (1, 1), (0, 0)))
    uc = up[:, :-2] * conv_w[0] + up[:, 1:-1] * conv_w[1] + up[:, 2:] * conv_w[2] + conv_b
    x0, x1, v = jnp.split(uc.astype(F32), 3, axis=-1)
    z = v * x1
    h_full = hyena_filter(L, w1, b1, w2, b2, w3, sin_freq)
    n = 2 * L
    y = jnp.fft.irfft(jnp.fft.rfft(z, n=n, axis=1) * jnp.fft.rfft(h_full, n=n, axis=0)[None], n=n, axis=1)[:, :L]
    y = y + z * skip.astype(F32)
    return (y * x0).astype(u.dtype)


def moe(h, w_router, b_router, w_gu, b_gu, w_down, b_down):
    B, L, D = h.shape
    x = h.reshape(B * L, D)
    logits = (x @ w_router + b_router).astype(F32)
    top_v, top_i = lax.top_k(logits, TOP_K)
    probs = jax.nn.softmax(top_v, axis=-1)
    combine = jnp.sum(jax.nn.one_hot(top_i, N_EXPERTS, dtype=F32) * probs[..., None], axis=1)
    out = jnp.zeros((B * L, D), F32)
    for e in range(N_EXPERTS):
        gu = (x @ w_gu[e] + b_gu[e]).astype(F32)
        gate, up = gu[:, :D_FF], gu[:, D_FF:]
        gate = jnp.minimum(gate, SWIGLU_LIMIT)
        up = jnp.clip(up, -SWIGLU_LIMIT, SWIGLU_LIMIT)
        act = (up + 1.0) * gate * jax.nn.sigmoid(SWIGLU_ALPHA * gate)
        y = act.astype(x.dtype) @ w_down[e] + b_down[e]
        out = out + combine[:, e:e + 1] * y.astype(F32)
    return out.astype(h.dtype).reshape(B, L, D)


def trunk_layer(x, mod, p, layer_idx, rope, ctx_k, ctx_v):
    B, L, D = x.shape
    shift1, scale1, gate1, shift2, scale2, gate2 = jnp.split(mod[:, None, :], N_MOD, axis=-1)
    h = rmsnorm(x, p['norm1']) * (1.0 + scale1) + shift1
    u = h @ p['w_in']
    i1 = POOL_WIDTH
    i2 = i1 + DA_WIDTH
    i3 = i2 + DA_WIDTH
    i4 = i3 + DA_WIDTH
    a_out = pool_branch(u[..., :i1], p['w_pool'], p['pool_scale'])
    b_out, k_own, v_own = diff_attn_branch(u[..., i1:i2], u[..., i2:i3], u[..., i3:i4],
                                           p['da_lambda'], p['da_subln'], layer_idx, rope, ctx_k, ctx_v)
    c_out = hyena_branch(u[..., i4:], p['hy_conv_w'], p['hy_conv_b'], p['hy_f_w1'], p['hy_f_b1'],
                         p['hy_f_w2'], p['hy_f_b2'], p['hy_f_w3'], p['hy_sin_freq'], p['hy_skip'])
    br = jnp.stack([a_out.astype(x.dtype), b_out.astype(x.dtype), c_out.astype(x.dtype)], axis=2)
    proj = jnp.einsum('blnc,ncd->blnd', br, p['w_branch'])
    gates = jax.nn.sigmoid((h @ p['w_gate'] + p['b_gate']).astype(F32)).reshape(B, L, N_BRANCH, D)
    merged = jnp.sum(gates * proj.astype(F32), axis=2).astype(x.dtype)
    x = x + gate1 * (merged @ p['w_o'])
    h2 = rmsnorm(x, p['norm2']) * (1.0 + scale2) + shift2
    x = x + gate2 * moe(h2, p['w_router'], p['b_router'], p['w_gu'], p['b_gu'], p['w_down'], p['b_down'])
    return x, k_own, v_own


def setup_inputs(seed: int = 0) -> dict:
    key = jax.random.key(seed)
    ks = iter(jax.random.split(key, 40))

    def nrm(shape, s):
        return jax.random.normal(next(ks), shape, F32) * s

    D = D_MODEL
    return {
        "x_prompt": nrm((BATCH, SEQ, D), 1.0),
        "x_sample": nrm((DEC_BATCH, DEC_SEQ, D), 1.0),
        "cache_k": nrm((DEC_BATCH, DEPTH, PAST_LEN, DA_HEADS, 2 * DA_HEAD_DIM), 1.0),
        "cache_v": nrm((DEC_BATCH, DEPTH, PAST_LEN, DA_HEADS, DA_VDIM), 1.0),
        "c": nrm((DEC_BATCH, D), 1.0),
        "c_ctx": nrm((D,), 1.0),
        "w_ada": nrm((DEPTH, D, N_MOD * D), 0.2 * D ** -0.5),
        "b_ada": nrm((DEPTH, N_MOD * D), 0.1),
        "norm1": 1.0 + nrm((DEPTH, D), 0.05),
        "norm2": 1.0 + nrm((DEPTH, D), 0.05),
        "w_in": nrm((DEPTH, D, D_IN), D ** -0.5),
        "w_pool": nrm((DEPTH, POOL_GROUPS, POOL_GC, POOL_GC), POOL_GC ** -0.5),
        "pool_scale": 1.0 + nrm((DEPTH, POOL_WIDTH), 0.1),
        "da_lambda": nrm((DEPTH, 4, DA_HEAD_DIM), 0.1),
        "da_subln": 1.0 + nrm((DEPTH, DA_VDIM), 0.05),
        "hy_conv_w": nrm((DEPTH, 3, 3 * HY_WIDTH), 0.5),
        "hy_conv_b": nrm((DEPTH, 3 * HY_WIDTH), 0.02),
        "hy_f_w1": nrm((DEPTH, HY_EMB, HY_HIDDEN), HY_EMB ** -0.5),
        "hy_f_b1": nrm((DEPTH, HY_HIDDEN), 0.1),
        "hy_f_w2": nrm((DEPTH, HY_HIDDEN, HY_HIDDEN), HY_HIDDEN ** -0.5),
        "hy_f_b2": nrm((DEPTH, HY_HIDDEN), 0.1),
        "hy_f_w3": nrm((DEPTH, HY_HIDDEN, 2 * HY_WIDTH), 0.04 * HY_HIDDEN ** -0.5),
        "hy_sin_freq": 1.0 + nrm((DEPTH, 2, HY_HIDDEN), 0.1),
        "hy_skip": nrm((DEPTH, HY_WIDTH), 0.1),
        "w_branch": nrm((DEPTH, N_BRANCH, POOL_WIDTH, D), POOL_WIDTH ** -0.5),
        "w_gate": nrm((DEPTH, D, N_BRANCH * D), D ** -0.5),
        "b_gate": nrm((DEPTH, N_BRANCH * D), 0.1),
        "w_o": nrm((DEPTH, D, D), D ** -0.5),
        "w_router": nrm((DEPTH, D, N_EXPERTS), D ** -0.5),
        "b_router": nrm((DEPTH, N_EXPERTS), 0.01),
        "w_gu": nrm((DEPTH, N_EXPERTS, D, 2 * D_FF), D ** -0.5),
        "b_gu": nrm((DEPTH, N_EXPERTS, 2 * D_FF), 0.02),
        "w_down": nrm((DEPTH, N_EXPERTS, D_FF, D), D_FF ** -0.5),
        "b_down": nrm((DEPTH, N_EXPERTS, D), 0.02),
        "final_norm": 1.0 + nrm((D,), 0.05),
    }


def reference(x_prompt, x_sample, cache_k, cache_v, c, c_ctx, w_ada, b_ada, norm1, norm2, w_in,
              w_pool, pool_scale, da_lambda, da_subln, hy_conv_w, hy_conv_b, hy_f_w1, hy_f_b1,
              hy_f_w2, hy_f_b2, hy_f_w3, hy_sin_freq, hy_skip, w_branch, w_gate, b_gate, w_o,
              w_router, b_router, w_gu, b_gu, w_down, b_down, final_norm):
    rope = axial_rope_tables(x_sample.shape[1])
    xp = x_prompt
    xs = x_sample
    new_k = []
    new_v = []
    for l in range(DEPTH):
        p = {
            'norm1': norm1[l], 'norm2': norm2[l], 'w_in': w_in[l],
            'w_pool': w_pool[l], 'pool_scale': pool_scale[l],
            'da_lambda': da_lambda[l], 'da_subln': da_subln[l],
            'hy_conv_w': hy_conv_w[l], 'hy_conv_b': hy_conv_b[l],
            'hy_f_w1': hy_f_w1[l], 'hy_f_b1': hy_f_b1[l], 'hy_f_w2': hy_f_w2[l], 'hy_f_b2': hy_f_b2[l],
            'hy_f_w3': hy_f_w3[l], 'hy_sin_freq': hy_sin_freq[l], 'hy_skip': hy_skip[l],
            'w_branch': w_branch[l], 'w_gate': w_gate[l], 'b_gate': b_gate[l], 'w_o': w_o[l],
            'w_router': w_router[l], 'b_router': b_router[l],
            'w_gu': w_gu[l], 'b_gu': b_gu[l], 'w_down': w_down[l], 'b_down': b_down[l],
        }
        mod_ctx = (jax.nn.silu(c_ctx) @ w_ada[l] + b_ada[l])[None]
        mod_lat = jax.nn.silu(c) @ w_ada[l] + b_ada[l]
        xp, k_ctx, v_ctx = trunk_layer(xp, mod_ctx, p, l, None, None, None)
        new_k.append(k_ctx)
        new_v.append(v_ctx)
        xs, _, _ = trunk_layer(xs, mod_lat, p, l, rope, cache_k[:, l], cache_v[:, l])
    y_prompt = rmsnorm(xp, final_norm)
    y_sample = rmsnorm(xs, final_norm)
    new_cache_k = jnp.stack(new_k, axis=1)
    new_cache_v = jnp.stack(new_v, axis=1)
    return (y_prompt, y_sample, new_cache_k, new_cache_v)
```

```python
import functools
import math

import numpy as np
import jax
import jax.numpy as jnp
from jax import lax
from jax.experimental import pallas as pl
from jax.experimental.pallas import tpu as pltpu

F32 = jnp.float32
BF16 = jnp.bfloat16

D_MODEL = 1024
GRID_W = 64
NORM_EPS = 1e-6
POOL_WIDTH = 512
POOL_WINDOWS = (2, 4, 8, 16)
POOL_GC = POOL_WIDTH // len(POOL_WINDOWS)
DA_HEADS = 4
DA_HEAD_DIM = 64
DA_VDIM = 2 * DA_HEAD_DIM
DA_WIDTH = DA_HEADS * DA_VDIM
ROPE_BASE = 10000.0
HY_WIDTH = 512
HY_EMB = 33
HY_BANDS = (HY_EMB - 1) // 2
HY_HIDDEN = 64
HY_FAST = 0.3
HY_SLOW = 1.5
HY_TARGET = 1e-2
N_BRANCH = 3
D_IN = POOL_WIDTH + 3 * DA_WIDTH + 3 * HY_WIDTH
N_EXPERTS = 32
TOP_K = 4
D_FF = 1024
SWIGLU_ALPHA = 1.702
SWIGLU_LIMIT = 7.0
N_MOD = 6

TOKEN_BLOCK = 256
HALO = 8
MOE_TILE = 256
V7X_VMEM_LIMIT = 56 * 1024 * 1024


def _cparams(n_axes):
    return pltpu.CompilerParams(
        dimension_semantics=("arbitrary",) * n_axes,
        vmem_limit_bytes=V7X_VMEM_LIMIT)


def _dot(a, b):
    return jnp.dot(a, b, preferred_element_type=F32)


def _dot_nt(a, b):
    return lax.dot_general(a, b, (((1,), (1,)), ((), ())), preferred_element_type=F32)


def _split_bf16(a):
    hi = a.astype(BF16)
    lo = (a - hi.astype(F32)).astype(BF16)
    return hi, lo


def _dot3(a, b):
    ah, al = _split_bf16(a)
    bh, bl = _split_bf16(b)
    return _dot(ah, bh) + _dot(al, bh) + _dot(ah, bl)


def _sigmoid(x):
    return 1.0 / (1.0 + jnp.exp(-x))


def _rms(x):
    return x * lax.rsqrt(jnp.mean(x * x, axis=-1, keepdims=True) + NORM_EPS)


class _Geom:
    def __init__(self, n_ctx_seq, ctx_len, n_lat_seq, lat_len):
        assert ctx_len == TOKEN_BLOCK, "one context sequence per token block"
        assert lat_len % TOKEN_BLOCK == 0
        self.bc, self.lc, self.bl, self.ll = n_ctx_seq, ctx_len, n_lat_seq, lat_len
        self.nbc = n_ctx_seq
        self.bpl = lat_len // TOKEN_BLOCK
        self.nbl = n_lat_seq * self.bpl
        self.nb = self.nbc + self.nbl
        self.nc = n_ctx_seq * ctx_len
        self.nl = n_lat_seq * lat_len
        self.n = self.nc + self.nl

    def group(self, i):
        return jnp.where(i < self.nbc, 0, 1 + (i - self.nbc) // self.bpl)

    def pos_block(self, i):
        return jnp.where(i < self.nbc, 0, (i - self.nbc) % self.bpl)

    def is_start(self, i):
        return jnp.logical_or(i < self.nbc, (i - self.nbc) % self.bpl == 0)

    def is_end(self, i):
        return jnp.logical_or(i < self.nbc, (i - self.nbc) % self.bpl == self.bpl - 1)

    def halo_specs(self, width):
        per = TOKEN_BLOCK // HALO
        last = self.n // HALO - 1
        before = pl.BlockSpec((HALO, width), lambda i: (jnp.maximum(i * per - 1, 0), 0))
        after = pl.BlockSpec((HALO, width), lambda i: (jnp.minimum((i + 1) * per, last), 0))
        return before, after


def _ada_kernel(c_ref, w_ref, b_ref, o_ref):
    c = c_ref[...]
    s = c * _sigmoid(c)
    o_ref[0] = _dot(s.astype(BF16), w_ref[0].astype(BF16)) + b_ref[0]


def _ada(cond, w_ada, b_ada):
    depth, d, n6 = w_ada.shape
    rows = cond.shape[0]
    tn = 1024
    return pl.pallas_call(
        _ada_kernel,
        grid=(depth, n6 // tn),
        in_specs=[pl.BlockSpec((rows, d), lambda l, j: (0, 0)),
                  pl.BlockSpec((1, d, tn), lambda l, j: (l, 0, j)),
                  pl.BlockSpec((1, 1, tn), lambda l, j: (l, 0, j))],
        out_specs=pl.BlockSpec((1, rows, tn), lambda l, j: (l, 0, j)),
        out_shape=jax.ShapeDtypeStruct((depth, rows, n6), F32),
        compiler_params=_cparams(2),
        name="ada_mod",
    )(cond, w_ada, b_ada.reshape(depth, 1, n6))


def _rope(x, cos, sin_signed, first_half):
    d = x.shape[-1]
    partner = jnp.where(first_half, pltpu.roll(x, d - 16, 1), pltpu.roll(x, 16, 1))
    return x * cos + partner * sin_signed


def _inproj_kernel(x_ref, mod_ref, g_ref, w_ref, cos_ref, sin_ref,
                   up_ref, q_ref, kb_ref, vb_ref, kf_ref, vf_ref, uh_ref, *, nbc):
    i = pl.program_id(0)
    d = D_MODEL
    mod = mod_ref[0]
    shift, scale = mod[:, 0:d], mod[:, d:2 * d]
    h = (_rms(x_ref[...]) * g_ref[...] * (1.0 + scale) + shift).astype(BF16)

    c1 = POOL_WIDTH
    c2 = c1 + DA_WIDTH
    c3 = c2 + DA_WIDTH
    c4 = c3 + DA_WIDTH
    up_ref[...] = _dot(h, w_ref[:, 0:c1])
    uh_ref[...] = _dot(h, w_ref[:, c4:D_IN])
    q = _dot(h, w_ref[:, c1:c2]) * (DA_HEAD_DIM ** -0.5)
    k = _dot(h, w_ref[:, c2:c3])
    v = _dot(h, w_ref[:, c3:c4])
    vb_ref[...] = v.astype(BF16)

    @pl.when(i < nbc)
    def _():
        q_ref[...] = q.astype(BF16)
        kb_ref[...] = k.astype(BF16)
        kf_ref[...] = k
        vf_ref[...] = v

    @pl.when(i >= nbc)
    def _():
        cos, sin_signed = cos_ref[...], sin_ref[...]
        lane = lax.broadcasted_iota(jnp.int32, q.shape, 1)
        first_half = (lane % 32) < 16
        q_ref[...] = _rope(q, cos, sin_signed, first_half).astype(BF16)
        kb_ref[...] = _rope(k, cos, sin_signed, first_half).astype(BF16)


def _inproj(geo, x, mod_l, g1, w_in_b, cos_t, sin_t):
    d = D_MODEL
    tb = TOKEN_BLOCK
    row = lambda w: pl.BlockSpec((tb, w), lambda i: (i, 0))
    ctx_row = lambda w: pl.BlockSpec((tb, w), lambda i: (jnp.minimum(i, geo.nbc - 1), 0))
    full = lambda a: pl.BlockSpec(a.shape, lambda i: (0,) * a.ndim, pipeline_mode=pl.Buffered(1))
    tab = pl.BlockSpec((tb, DA_WIDTH), lambda i: (geo.pos_block(i), 0))
    return pl.pallas_call(
        functools.partial(_inproj_kernel, nbc=geo.nbc),
        grid=(geo.nb,),
        in_specs=[row(d),
                  pl.BlockSpec((1, 1, N_MOD * d), lambda i: (geo.group(i), 0, 0)),
                  full(g1), full(w_in_b), tab, tab],
        out_specs=[row(POOL_WIDTH), row(DA_WIDTH), row(DA_WIDTH), row(DA_WIDTH),
                   ctx_row(DA_WIDTH), ctx_row(DA_WIDTH), row(3 * HY_WIDTH)],
        out_shape=[jax.ShapeDtypeStruct((geo.n, POOL_WIDTH), F32),
                   jax.ShapeDtypeStruct((geo.n, DA_WIDTH), BF16),
                   jax.ShapeDtypeStruct((geo.n, DA_WIDTH), BF16),
                   jax.ShapeDtypeStruct((geo.n, DA_WIDTH), BF16),
                   jax.ShapeDtypeStruct((geo.nc, DA_WIDTH), F32),
                   jax.ShapeDtypeStruct((geo.nc, DA_WIDTH), F32),
                   jax.ShapeDtypeStruct((geo.n, 3 * HY_WIDTH), F32)],
        compiler_params=_cparams(1),
        name="in_proj",
    )(x, mod_l, g1, w_in_b, cos_t, sin_t)


def _fill_padded(pad_ref, before_ref, main_ref, after_ref, start, end):
    tb = TOKEN_BLOCK
    zero = jnp.zeros(before_ref.shape, F32)
    pad_ref[0:HALO, :] = jnp.where(start, zero, before_ref[...])
    pad_ref[HALO:HALO + tb, :] = main_ref[...]
    pad_ref[HALO + tb:2 * HALO + tb, :] = jnp.where(end, zero, after_ref[...])


def _pool_kernel(main_ref, before_ref, after_ref, w_ref, s_ref, o_ref, pad_ref, *, geo):
    i = pl.program_id(0)
    tb = TOKEN_BLOCK
    start, end = geo.is_start(i), geo.is_end(i)
    _fill_padded(pad_ref, before_ref, main_ref, after_ref, start, end)
    r = lax.broadcasted_iota(jnp.int32, (tb, 1), 0)
    for g, w in enumerate(POOL_WINDOWS):
        cols = slice(g * POOL_GC, (g + 1) * POOL_GC)
        acc = pad_ref[HALO - w // 2:HALO - w // 2 + tb, cols]
        for j in range(-w // 2 + 1, w // 2):
            acc = acc + pad_ref[HALO + j:HALO + j + tb, cols]
        lo = jnp.where(start, jnp.maximum(r - w // 2, 0), r - w // 2)
        hi = jnp.where(end, jnp.minimum(r + w // 2, tb), r + w // 2)
        mean = acc / (hi - lo).astype(F32)
        dlt = mean - main_ref[:, cols]
        y = _dot(dlt.astype(BF16), w_ref[g])
        o_ref[:, cols] = (y * s_ref[:, cols]).astype(BF16)


def _pool(geo, u_pool, w_pool_b, pool_scale):
    tb = TOKEN_BLOCK
    before, after = geo.halo_specs(POOL_WIDTH)
    return pl.pallas_call(
        functools.partial(_pool_kernel, geo=geo),
        grid=(geo.nb,),
        in_specs=[pl.BlockSpec((tb, POOL_WIDTH), lambda i: (i, 0)), before, after,
                  pl.BlockSpec(w_pool_b.shape, lambda i: (0, 0, 0)),
                  pl.BlockSpec(pool_scale.shape, lambda i: (0, 0))],
        out_specs=pl.BlockSpec((tb, POOL_WIDTH), lambda i: (i, 0)),
        out_shape=jax.ShapeDtypeStruct((geo.n, POOL_WIDTH), BF16),
        scratch_shapes=[pltpu.VMEM((tb + 2 * HALO, POOL_WIDTH), F32)],
        compiler_params=_cparams(1),
        name="pool_branch",
    )(u_pool, u_pool, u_pool, w_pool_b, pool_scale)


def _attn_kernel(*refs, lam_init, has_ctx):
    if has_ctx:
        q_ref, k_ref, v_ref, kc_ref, vc_ref, lam_ref, g_ref, o_ref = refs
    else:
        q_ref, k_ref, v_ref, lam_ref, g_ref, o_ref = refs
    lp = lam_ref[...]
    lam = (jnp.exp(jnp.sum(lp[0:1] * lp[1:2], axis=-1, keepdims=True))
           - jnp.exp(jnp.sum(lp[2:3] * lp[3:4], axis=-1, keepdims=True)) + lam_init)
    for h in range(DA_HEADS):
        vcols = slice(h * DA_VDIM, (h + 1) * DA_VDIM)
        v = v_ref[0, :, vcols]
        if has_ctx:
            vc = vc_ref[0, 0, :, vcols].astype(BF16)
        outs = []
        for sub in range(2):
            c0 = h * DA_VDIM + sub * DA_HEAD_DIM
            cols = slice(c0, c0 + DA_HEAD_DIM)
            qs = q_ref[0, :, cols]
            s1 = _dot_nt(qs, k_ref[0, :, cols])
            m = jnp.max(s1, axis=-1, keepdims=True)
            if has_ctx:
                s2 = _dot_nt(qs, kc_ref[0, 0, :, cols].astype(BF16))
                m = jnp.maximum(m, jnp.max(s2, axis=-1, keepdims=True))
            e1 = jnp.exp(s1 - m)
            den = jnp.sum(e1, axis=-1, keepdims=True)
            pv = _dot(e1.astype(BF16), v)
            if has_ctx:
                e2 = jnp.exp(s2 - m)
                den = den + jnp.sum(e2, axis=-1, keepdims=True)
                pv = pv + _dot(e2.astype(BF16), vc)
            outs.append(pv / den)
        o = outs[0] - lam * outs[1]
        o = _rms(o) * g_ref[...] * (1.0 - lam_init)
        o_ref[0, :, vcols] = o.astype(BF16)


def _attn_ctx(geo, q, kb, vb, lam_p, subln, lam_init):
    bc, lc = geo.bc, geo.lc
    seq = pl.BlockSpec((1, lc, DA_WIDTH), lambda b: (b, 0, 0))
    out = pl.pallas_call(
        functools.partial(_attn_kernel, lam_init=lam_init, has_ctx=False),
        grid=(bc,),
        in_specs=[seq, seq, seq,
                  pl.BlockSpec(lam_p.shape, lambda b: (0, 0)),
                  pl.BlockSpec(subln.shape, lambda b: (0, 0))],
        out_specs=seq,
        out_shape=jax.ShapeDtypeStruct((bc, lc, DA_WIDTH), BF16),
        compiler_params=_cparams(1),
        name="attn_ctx",
    )(q, kb, vb, lam_p, subln)
    return out.reshape(bc * lc, DA_WIDTH)


def _attn_lat(geo, layer, q, kb, vb, cache_k, cache_v, lam_p, subln, lam_init):
    bl, ll = geo.bl, geo.ll
    tq = TOKEN_BLOCK
    past = cache_k.shape[2]
    qblk = pl.BlockSpec((1, tq, DA_WIDTH), lambda b, j: (b, j, 0))
    seq = pl.BlockSpec((1, ll, DA_WIDTH), lambda b, j: (b, 0, 0))
    cache = pl.BlockSpec((1, 1, past, DA_WIDTH), lambda b, j: (b, layer, 0, 0))
    out = pl.pallas_call(
        functools.partial(_attn_kernel, lam_init=lam_init, has_ctx=True),
        grid=(bl, ll // tq),
        in_specs=[qblk, seq, seq, cache, cache,
                  pl.BlockSpec(lam_p.shape, lambda b, j: (0, 0)),
                  pl.BlockSpec(subln.shape, lambda b, j: (0, 0))],
        out_specs=qblk,
        out_shape=jax.ShapeDtypeStruct((bl, ll, DA_WIDTH), BF16),
        compiler_params=_cparams(2),
        name="attn_lat",
    )(q, kb, vb, cache_k, cache_v, lam_p, subln)
    return out.reshape(bl * ll, DA_WIDTH)


@functools.lru_cache(maxsize=None)
def _dft_tables(length):
    k = np.arange(length, dtype=np.int64)
    ks = (k[:, None] * k[None, :]) % (2 * length)
    ang = ks.astype(np.float64) * (np.pi / length)
    cmat = np.cos(ang)
    smat = -np.sin(ang)
    smat[0, :] = 1.0 - 2.0 * (k % 2)
    to_bf16 = lambda a: jnp.asarray(a.astype(np.float32)).astype(BF16)
    return to_bf16(cmat), to_bf16(smat), to_bf16(smat.T)


@functools.lru_cache(maxsize=None)
def _filter_features(length):
    t = np.linspace(0.0, 1.0, length, dtype=np.float32)
    w_ang = (2.0 * math.pi * np.arange(length, dtype=np.float32) / length).astype(np.float32)
    f = np.linspace(1e-4, HY_BANDS - 1, HY_BANDS, dtype=np.float32)
    arg = (w_ang[:, None] * f[None, :]).astype(np.float32).astype(np.float64)
    z = np.concatenate([t[:, None].astype(np.float64), np.cos(arg), -np.sin(arg)], axis=-1)
    z = np.pad(z, ((0, 0), (0, 64 - HY_EMB))).astype(np.float32)
    rev = np.concatenate([z[:1], z[:0:-1]], axis=0)
    deltas = np.linspace(math.log(HY_TARGET) / HY_FAST, math.log(HY_TARGET) / HY_SLOW,
                         HY_WIDTH, dtype=np.float32)
    return jnp.asarray(z), jnp.asarray(rev), jnp.asarray(np.abs(deltas)[None, :])


def _hyfilter_kernel(z_ref, zr_ref, dl_ref, w1_ref, b1_ref, w2_ref, b2_ref, w3_ref, fr_ref, o_ref):
    c = pl.program_id(0)
    fr = fr_ref[...]

    def mlp(z, w3):
        hid = jnp.sin(fr[0:1] * (_dot3(z, w1_ref[...]) + b1_ref[...]))
        hid = jnp.sin(fr[1:2] * (_dot3(hid, w2_ref[...]) + b2_ref[...]))
        return _dot3(hid, w3) * jnp.exp(-z[:, 0:1] * dl_ref[...])

    z, zr = z_ref[...], zr_ref[...]
    o_ref[:, 0:HY_WIDTH] = mlp(z, w3_ref[:, 0:HY_WIDTH]).astype(BF16)
    bwd = mlp(zr, w3_ref[:, HY_WIDTH:2 * HY_WIDTH])
    row = lax.broadcasted_iota(jnp.int32, bwd.shape, 0) + c * z.shape[0]
    o_ref[:, HY_WIDTH:2 * HY_WIDTH] = jnp.where(row == 0, 0.0, bwd).astype(BF16)


def _hyfilter(length, w1p, b1, w2, b2, w3, fr):
    z, zr, dl = _filter_features(length)
    tb = TOKEN_BLOCK
    blk = pl.BlockSpec((tb, 64), lambda c: (c, 0))
    full = lambda a: pl.BlockSpec(a.shape, lambda c: (0,) * a.ndim)
    return pl.pallas_call(
        _hyfilter_kernel,
        grid=(length // tb,),
        in_specs=[blk, blk, full(dl), full(w1p), full(b1), full(w2), full(b2), full(w3), full(fr)],
        out_specs=pl.BlockSpec((tb, 2 * HY_WIDTH), lambda c: (c, 0)),
        out_shape=jax.ShapeDtypeStruct((length, 2 * HY_WIDTH), BF16),
        compiler_params=_cparams(1),
        name="hyena_filter",
    )(z, zr, dl, w1p, b1, w2, b2, w3, fr)


def _hyconv_kernel(main_ref, before_ref, after_ref, cw_ref, cb_ref, z_ref, zb_ref, x0_ref, pad_ref, *, geo):
    i = pl.program_id(0)
    tb = TOKEN_BLOCK
    _fill_padded(pad_ref, before_ref, main_ref, after_ref, geo.is_start(i), geo.is_end(i))
    w = HY_WIDTH
    parts = []
    for p in range(3):
        cols = slice(p * w, (p + 1) * w)
        uc = (pad_ref[HALO - 1:HALO - 1 + tb, cols] * cw_ref[0:1, cols]
              + main_ref[:, cols] * cw_ref[1:2, cols]
              + pad_ref[HALO + 1:HALO + 1 + tb, cols] * cw_ref[2:3, cols]
              + cb_ref[:, cols])
        parts.append(uc)
    x0, x1, v = parts
    z = v * x1
    z_ref[...] = z
    zb_ref[...] = z.astype(BF16)
    x0_ref[...] = x0


def _hyconv(geo, u_hy, conv_w, conv_b):
    tb = TOKEN_BLOCK
    w3 = 3 * HY_WIDTH
    before, after = geo.halo_specs(w3)
    row = pl.BlockSpec((tb, HY_WIDTH), lambda i: (i, 0))
    return pl.pallas_call(
        functools.partial(_hyconv_kernel, geo=geo),
        grid=(geo.nb,),
        in_specs=[pl.BlockSpec((tb, w3), lambda i: (i, 0)), before, after,
                  pl.BlockSpec(conv_w.shape, lambda i: (0, 0)),
                  pl.BlockSpec(conv_b.shape, lambda i: (0, 0))],
        out_specs=[row, row, row],
        out_shape=[jax.ShapeDtypeStruct((geo.n, HY_WIDTH), F32),
                   jax.ShapeDtypeStruct((geo.n, HY_WIDTH), BF16),
                   jax.ShapeDtypeStruct((geo.n, HY_WIDTH), F32)],
        scratch_shapes=[pltpu.VMEM((tb + 2 * HALO, w3), F32)],
        compiler_params=_cparams(1),
        name="hyena_conv_gate",
    )(u_hy, u_hy, u_hy, conv_w, conv_b)


def _dft_fwd_kernel(c_ref, s_ref, x_ref, re_ref, im_ref):
    x = x_ref[0]
    re_ref[0] = _dot(c_ref[...], x)
    im_ref[0] = _dot(s_ref[...], x)


def _dft_fwd(length, x):
    cmat, smat, _ = _dft_tables(length)
    nb, _, n = x.shape
    tm = min(length, 512)
    tn = 512
    a_spec = pl.BlockSpec((tm, length), lambda b, j, m: (m, 0))
    o_spec = pl.BlockSpec((1, tm, tn), lambda b, j, m: (b, m, j))
    return pl.pallas_call(
        _dft_fwd_kernel,
        grid=(nb, n // tn, length // tm),
        in_specs=[a_spec, a_spec, pl.BlockSpec((1, length, tn), lambda b, j, m: (b, 0, j))],
        out_specs=[o_spec, o_spec],
        out_shape=[jax.ShapeDtypeStruct((nb, length, n), F32)] * 2,
        compiler_params=_cparams(3),
        name="hyena_dft",
    )(cmat, smat, x)


def _hyprod_kernel(zr_ref, zi_ref, hfr_ref, hgr_ref, hfi_ref, hgi_ref, yr_ref, yi_ref, *, length):
    c = pl.program_id(1)
    shape = zr_ref.shape[1:]
    k = lax.broadcasted_iota(jnp.int32, shape, 0) + c * shape[0]
    sgn = (1 - 2 * (k % 2)).astype(F32)
    hr = hfr_ref[0] + sgn * hgr_ref[0]
    hi = hfi_ref[0] + sgn * hgi_ref[0]
    zr, zi = zr_ref[0], zi_ref[0]
    inv = 1.0 / length
    yr = (zr * hr - zi * hi) * inv
    yi = (zr * hi + zi * hr) * inv
    first = k == 0
    yr_ref[0] = jnp.where(first, zr * hr * (0.5 * inv), yr).astype(BF16)
    yi_ref[0] = jnp.where(first, zi * hi * (0.5 * inv), yi).astype(BF16)


def _hyprod(length, zre, zim, hre, him):
    nb = zre.shape[0]
    tb = TOKEN_BLOCK
    zs = pl.BlockSpec((1, tb, HY_WIDTH), lambda b, c: (b, c, 0))
    hf = pl.BlockSpec((1, tb, HY_WIDTH), lambda b, c: (0, c, 0))
    hg = pl.BlockSpec((1, tb, HY_WIDTH), lambda b, c: (0, c, 1))
    return pl.pallas_call(
        functools.partial(_hyprod_kernel, length=length),
        grid=(nb, length // tb),
        in_specs=[zs, zs, hf, hg, hf, hg],
        out_specs=[zs, zs],
        out_shape=[jax.ShapeDtypeStruct((nb, length, HY_WIDTH), BF16)] * 2,
        compiler_params=_cparams(2),
        name="hyena_spectral_product",
    )(zre, zim, hre, hre, him, him)


def _hyinv_kernel(c_ref, st_ref, yr_ref, yi_ref, z_ref, x0_ref, skip_ref, o_ref):
    y = _dot(c_ref[...], yr_ref[0]) + _dot(st_ref[...], yi_ref[0])
    o_ref[0] = ((y + z_ref[0] * skip_ref[...]) * x0_ref[0]).astype(BF16)


def _hyinv(length, yr, yi, z, x0, skip):
    cmat, _, smat_t = _dft_tables(length)
    nb = yr.shape[0]
    tm = min(length, 512)
    a_spec = pl.BlockSpec((tm, length), lambda b, m: (m, 0))
    y_spec = pl.BlockSpec((1, length, HY_WIDTH), lambda b, m: (b, 0, 0))
    t_spec = pl.BlockSpec((1, tm, HY_WIDTH), lambda b, m: (b, m, 0))
    return pl.pallas_call(
        _hyinv_kernel,
        grid=(nb, length // tm),
        in_specs=[a_spec, a_spec, y_spec, y_spec, t_spec, t_spec,
                  pl.BlockSpec(skip.shape, lambda b, m: (0, 0))],
        out_specs=t_spec,
        out_shape=jax.ShapeDtypeStruct((nb, length, HY_WIDTH), BF16),
        compiler_params=_cparams(2),
        name="hyena_idft",
    )(cmat, smat_t, yr, yi, z, x0, skip)


def _hyena_long_conv(length, nseq, zb, z, x0, filt, skip):
    zre, zim = _dft_fwd(length, zb.reshape(nseq, length, HY_WIDTH))
    hre, him = _dft_fwd(length, filt.reshape(1, length, 2 * HY_WIDTH))
    yr, yi = _hyprod(length, zre, zim, hre, him)
    out = _hyinv(length, yr, yi, z.reshape(nseq, length, HY_WIDTH),
                 x0.reshape(nseq, length, HY_WIDTH), skip)
    return out.reshape(nseq * length, HY_WIDTH)


def _merge_kernel(x_ref, mod_ref, g1_ref, a_ref, b_ref, c_ref, wg_ref, bg_ref, wbr_ref, wo_ref,
                  g2_ref, wr_ref, br_ref, xo_ref, h2_ref, ti_ref, tp_ref):
    d = D_MODEL
    mod = mod_ref[0]
    shift1, scale1, gate1 = mod[:, 0:d], mod[:, d:2 * d], mod[:, 2 * d:3 * d]
    shift2, scale2 = mod[:, 3 * d:4 * d], mod[:, 4 * d:5 * d]
    x = x_ref[...]
    h = (_rms(x) * g1_ref[...] * (1.0 + scale1) + shift1).astype(BF16)
    merged = None
    for n, br_ref_n in enumerate((a_ref, b_ref, c_ref)):
        cols = slice(n * d, (n + 1) * d)
        gate = _sigmoid(_dot(h, wg_ref[:, cols]) + bg_ref[:, cols])
        term = gate * _dot(br_ref_n[...], wbr_ref[n])
        merged = term if merged is None else merged + term
    x = x + gate1 * _dot(merged.astype(BF16), wo_ref[...])
    xo_ref[...] = x
    h2 = _rms(x) * g2_ref[...] * (1.0 + scale2) + shift2
    h2_ref[...] = h2.astype(BF16)

    logits = _dot3(h2, wr_ref[...]) + br_ref[...]
    ne = logits.shape[-1]
    lane = lax.broadcasted_iota(jnp.int32, logits.shape, 1).astype(F32)
    vals = logits
    top_v, top_i = [], []
    for _ in range(TOP_K):
        m = jnp.max(vals, axis=-1, keepdims=True)
        idx = jnp.min(jnp.where(vals == m, lane, float(ne)), axis=-1, keepdims=True)
        top_v.append(m)
        top_i.append(idx)
        vals = jnp.where(lane == idx, -jnp.inf, vals)
    es = [jnp.exp(v - top_v[0]) for v in top_v]
    den = es[0] + es[1] + es[2] + es[3]
    for kk in range(TOP_K):
        ti_ref[:, kk:kk + 1] = top_i[kk].astype(jnp.int32)
        tp_ref[:, kk:kk + 1] = es[kk] / den


def _merge(geo, x, mod_l, g1, a_out, b_out, c_out, wg_b, bg, wbr_b, wo_b, g2, wr, br):
    d = D_MODEL
    tb = TOKEN_BLOCK
    row = lambda w: pl.BlockSpec((tb, w), lambda i: (i, 0))
    full = lambda a: pl.BlockSpec(a.shape, lambda i: (0,) * a.ndim, pipeline_mode=pl.Buffered(1))
    return pl.pallas_call(
        _merge_kernel,
        grid=(geo.nb,),
        in_specs=[row(d), pl.BlockSpec((1, 1, N_MOD * d), lambda i: (geo.group(i), 0, 0)),
                  full(g1), row(POOL_WIDTH), row(DA_WIDTH), row(HY_WIDTH),
                  full(wg_b), full(bg), full(wbr_b), full(wo_b), full(g2), full(wr), full(br)],
        out_specs=[row(d), row(d), row(TOP_K), row(TOP_K)],
        out_shape=[jax.ShapeDtypeStruct((geo.n, d), F32),
                   jax.ShapeDtypeStruct((geo.n, d), BF16),
                   jax.ShapeDtypeStruct((geo.n, TOP_K), jnp.int32),
                   jax.ShapeDtypeStruct((geo.n, TOP_K), F32)],
        compiler_params=_cparams(1),
        name="merge_route",
    )(x, mod_l, g1, a_out, b_out, c_out, wg_b, bg, wbr_b, wo_b, g2, wr, br)


def _ffn_kernel(te_ref, nv_ref, x_ref, wgu_ref, bgu_ref, wd_ref, bd_ref, y_ref, wgub_ref, wdb_ref):
    t = pl.program_id(0)
    e = te_ref[t]
    prev = te_ref[jnp.maximum(t - 1, 0)]
    valid = t < nv_ref[0]
    first = jnp.logical_or(t == 0, e != prev)
    chunk = 128

    @pl.when(jnp.logical_and(valid, first))
    def _():
        def cast_gu(c, carry):
            rows = pl.ds(pl.multiple_of(c * chunk, chunk), chunk)
            wgub_ref[rows, :] = wgu_ref[0, rows, :].astype(BF16)
            return carry
        lax.fori_loop(0, wgu_ref.shape[1] // chunk, cast_gu, 0)

        def cast_d(c, carry):
            rows = pl.ds(pl.multiple_of(c * chunk, chunk), chunk)
            wdb_ref[rows, :] = wd_ref[0, rows, :].astype(BF16)
            return carry
        lax.fori_loop(0, wd_ref.shape[1] // chunk, cast_d, 0)

    @pl.when(valid)
    def _():
        gu = _dot(x_ref[...], wgub_ref[...]) + bgu_ref[0]
        gate = jnp.minimum(gu[:, 0:D_FF], SWIGLU_LIMIT)
        up = jnp.clip(gu[:, D_FF:2 * D_FF], -SWIGLU_LIMIT, SWIGLU_LIMIT)
        act = (up + 1.0) * gate * _sigmoid(SWIGLU_ALPHA * gate)
        y = _dot(act.astype(BF16), wdb_ref[...]) + bd_ref[0]
        y_ref[...] = y.astype(BF16)


def _ffn(xs, tile_expert, n_valid, w_gu, b_gu, w_down, b_down):
    p, d = xs.shape
    ne = w_gu.shape[0]
    t = MOE_TILE
    n_tiles = p // t
    tile = pl.BlockSpec((t, d), lambda i, te, nv: (jnp.minimum(i, nv[0] - 1), 0))
    grid_spec = pltpu.PrefetchScalarGridSpec(
        num_scalar_prefetch=2,
        grid=(n_tiles,),
        in_specs=[tile,
                  pl.BlockSpec((1, d, 2 * D_FF), lambda i, te, nv: (te[i], 0, 0)),
                  pl.BlockSpec((1, 1, 2 * D_FF), lambda i, te, nv: (te[i], 0, 0)),
                  pl.BlockSpec((1, D_FF, d), lambda i, te, nv: (te[i], 0, 0)),
                  pl.BlockSpec((1, 1, d), lambda i, te, nv: (te[i], 0, 0))],
        out_specs=tile,
        scratch_shapes=[pltpu.VMEM((d, 2 * D_FF), BF16), pltpu.VMEM((D_FF, d), BF16)])
    return pl.pallas_call(
        _ffn_kernel,
        grid_spec=grid_spec,
        out_shape=jax.ShapeDtypeStruct((p, d), BF16),
        compiler_params=_cparams(1),
        name="moe_experts",
    )(tile_expert, n_valid, xs, w_gu, b_gu.reshape(ne, 1, 2 * D_FF), w_down, b_down.reshape(ne, 1, d))


def _route(top_i, n_experts):
    n = top_i.shape[0]
    t = MOE_TILE
    flat_e = top_i.reshape(-1)
    order = jnp.argsort(flat_e, stable=True).astype(jnp.int32)
    sorted_e = flat_e[order]
    counts = jnp.zeros((n_experts,), jnp.int32).at[flat_e].add(1)
    padded = ((counts + t - 1) // t) * t
    pad_start = jnp.cumsum(padded) - padded
    raw_start = jnp.cumsum(counts) - counts
    rank = jnp.arange(n * TOP_K, dtype=jnp.int32) - raw_start[sorted_e]
    slot_sorted = pad_start[sorted_e] + rank
    n_rows = n * TOP_K + n_experts * t
    slot_token = jnp.zeros((n_rows,), jnp.int32).at[slot_sorted].set(order // TOP_K)
    pair_slot = jnp.zeros((n * TOP_K,), jnp.int32).at[order].set(slot_sorted)
    n_tiles = n_rows // t
    tile_end = jnp.cumsum(padded) // t
    n_valid = tile_end[-1]
    tile_ids = jnp.arange(n_tiles, dtype=jnp.int32)
    tile_expert = jnp.searchsorted(tile_end, jnp.minimum(tile_ids, n_valid - 1), side="right")
    tile_expert = jnp.minimum(tile_expert, n_experts - 1).astype(jnp.int32)
    return slot_token, pair_slot.reshape(n, TOP_K), tile_expert, n_valid.reshape(1).astype(jnp.int32)


def _combine_kernel(x_ref, mod_ref, yg_ref, p_ref, gf_ref, xo_ref, yo_ref):
    d = D_MODEL
    gate2 = mod_ref[0][:, 5 * d:6 * d]
    p = p_ref[...]
    moe = p[:, 0:1] * yg_ref[0].astype(F32)
    for kk in range(1, TOP_K):
        moe = moe + p[:, kk:kk + 1] * yg_ref[kk].astype(F32)
    x = x_ref[...] + gate2 * moe
    xo_ref[...] = x
    yo_ref[...] = _rms(x) * gf_ref[...]


def _combine(geo, x, mod_l, yg, top_p, final_g):
    d = D_MODEL
    tb = TOKEN_BLOCK
    row = pl.BlockSpec((tb, d), lambda i: (i, 0))
    return pl.pallas_call(
        _combine_kernel,
        grid=(geo.nb,),
        in_specs=[row, pl.BlockSpec((1, 1, N_MOD * d), lambda i: (geo.group(i), 0, 0)),
                  pl.BlockSpec((TOP_K, tb, d), lambda i: (0, i, 0)),
                  pl.BlockSpec((tb, TOP_K), lambda i: (i, 0)),
                  pl.BlockSpec(final_g.shape, lambda i: (0, 0))],
        out_specs=[row, row],
        out_shape=[jax.ShapeDtypeStruct((geo.n, d), F32)] * 2,
        compiler_params=_cparams(1),
        name="moe_combine",
    )(x, mod_l, yg, top_p, final_g)


@functools.lru_cache(maxsize=None)
def _rope_tables(length):
    rows = length // GRID_W
    row = np.repeat(np.arange(rows), GRID_W).astype(np.float32)
    col = np.tile(np.arange(GRID_W), rows).astype(np.float32)
    ax = DA_HEAD_DIM // 2
    inv = (ROPE_BASE ** (-(np.arange(ax // 2, dtype=np.float32) * 2.0 / ax))).astype(np.float32)
    ang_r = (row[:, None] * inv).astype(np.float32)
    ang_c = (col[:, None] * inv).astype(np.float32)
    ang = np.concatenate([ang_r, ang_r, ang_c, ang_c], axis=-1).astype(np.float64)
    sign = np.where((np.arange(DA_HEAD_DIM) % 32) < 16, -1.0, 1.0)
    reps = DA_WIDTH // DA_HEAD_DIM
    cos = np.tile(np.cos(ang), (1, reps)).astype(np.float32)
    sin_signed = np.tile(np.sin(ang) * sign[None, :], (1, reps)).astype(np.float32)
    return jnp.asarray(cos), jnp.asarray(sin_signed)


def kernel(x_prompt, x_sample, cache_k, cache_v, c, c_ctx, w_ada, b_ada, norm1, norm2, w_in, w_pool, pool_scale, da_lambda, da_subln, hy_conv_w, hy_conv_b, hy_f_w1, hy_f_b1, hy_f_w2, hy_f_b2, hy_f_w3, hy_sin_freq, hy_skip, w_branch, w_gate, b_gate, w_o, w_router, b_router, w_gu, b_gu, w_down, b_down, final_norm):
    bc, lc, d = x_prompt.shape
    bl, ll, _ = x_sample.shape
    depth = w_in.shape[0]
    n_experts = w_router.shape[-1]
    past = cache_k.shape[2]
    geo = _Geom(bc, lc, bl, ll)

    x = jnp.concatenate([x_prompt.reshape(bc * lc, d), x_sample.reshape(bl * ll, d)], axis=0)
    cond = jnp.concatenate([c_ctx[None], c, jnp.zeros((8 - 1 - bl, d), F32)], axis=0)
    mods = _ada(cond, w_ada, b_ada)
    cos_t, sin_t = _rope_tables(ll)
    ck = cache_k.reshape(bl, depth, past, DA_WIDTH)
    cv = cache_v.reshape(bl, depth, past, DA_WIDTH)

    w_in_b = w_in.astype(BF16)
    w_pool_b = w_pool.astype(BF16)
    w_gate_b = w_gate.astype(BF16)
    w_branch_b = w_branch.astype(BF16)
    w_o_b = w_o.astype(BF16)
    w1p = jnp.pad(hy_f_w1, ((0, 0), (0, 64 - HY_EMB), (0, 0)))

    new_k, new_v = [], []
    y = None
    for l in range(depth):
        mod_l = mods[l].reshape(8, 1, N_MOD * d)
        g1 = norm1[l][None]
        u_pool, q, kb, vb, kf, vf, u_hy = _inproj(geo, x, mod_l, g1, w_in_b[l], cos_t, sin_t)
        new_k.append(kf.reshape(bc, lc, DA_HEADS, 2 * DA_HEAD_DIM))
        new_v.append(vf.reshape(bc, lc, DA_HEADS, DA_VDIM))

        a_out = _pool(geo, u_pool, w_pool_b[l], pool_scale[l][None])

        lam_init = 0.8 - 0.6 * math.exp(-0.3 * l)
        subln = da_subln[l][None]
        b_ctx = _attn_ctx(geo, q[:geo.nc].reshape(bc, lc, DA_WIDTH), kb[:geo.nc].reshape(bc, lc, DA_WIDTH),
                          vb[:geo.nc].reshape(bc, lc, DA_WIDTH), da_lambda[l], subln, lam_init)
        b_lat = _attn_lat(geo, l, q[geo.nc:].reshape(bl, ll, DA_WIDTH), kb[geo.nc:].reshape(bl, ll, DA_WIDTH),
                          vb[geo.nc:].reshape(bl, ll, DA_WIDTH), ck, cv, da_lambda[l], subln, lam_init)
        b_out = jnp.concatenate([b_ctx, b_lat], axis=0)

        z, zb, x0 = _hyconv(geo, u_hy, hy_conv_w[l], hy_conv_b[l][None])
        fargs = (w1p[l], hy_f_b1[l][None], hy_f_w2[l], hy_f_b2[l][None], hy_f_w3[l], hy_sin_freq[l])
        skip = hy_skip[l][None]
        c_ctx_out = _hyena_long_conv(lc, bc, zb[:geo.nc], z[:geo.nc], x0[:geo.nc],
                                     _hyfilter(lc, *fargs), skip)
        c_lat_out = _hyena_long_conv(ll, bl, zb[geo.nc:], z[geo.nc:], x0[geo.nc:],
                                     _hyfilter(ll, *fargs), skip)
        c_out = jnp.concatenate([c_ctx_out, c_lat_out], axis=0)

        x, h2, top_i, top_p = _merge(geo, x, mod_l, g1, a_out, b_out, c_out, w_gate_b[l], b_gate[l][None],
                                     w_branch_b[l], w_o_b[l], norm2[l][None], w_router[l], b_router[l][None])

        slot_token, pair_slot, tile_expert, n_valid = _route(top_i, n_experts)
        xs = jnp.take(h2, slot_token, axis=0)
        ys = _ffn(xs, tile_expert, n_valid, w_gu[l], b_gu[l], w_down[l], b_down[l])
        yg = jnp.take(ys, pair_slot.T, axis=0)
        x, y = _combine(geo, x, mod_l, yg, top_p, final_norm[None])

    y_prompt = y[:geo.nc].reshape(bc, lc, d)
    y_sample = y[geo.nc:].reshape(bl, ll, d)
    return (y_prompt, y_sample, jnp.stack(new_k, axis=1), jnp.stack(new_v, axis=1))
```

```python
import functools
import math

import numpy as np
import jax
import jax.numpy as jnp
from jax import lax
from jax.experimental import pallas as pl
from jax.experimental.pallas import tpu as pltpu

F32 = jnp.float32
BF16 = jnp.bfloat16

D_MODEL = 1024
GRID_W = 64
NORM_EPS = 1e-6
POOL_WIDTH = 512
POOL_WINDOWS = (2, 4, 8, 16)
POOL_GC = POOL_WIDTH // len(POOL_WINDOWS)
DA_HEADS = 4
DA_HEAD_DIM = 64
DA_VDIM = 2 * DA_HEAD_DIM
DA_WIDTH = DA_HEADS * DA_VDIM
ROPE_BASE = 10000.0
HY_WIDTH = 512
HY_EMB = 33
HY_BANDS = (HY_EMB - 1) // 2
HY_HIDDEN = 64
HY_FAST = 0.3
HY_SLOW = 1.5
HY_TARGET = 1e-2
N_BRANCH = 3
D_IN = POOL_WIDTH + 3 * DA_WIDTH + 3 * HY_WIDTH
TOP_K = 4
D_FF = 1024
SWIGLU_ALPHA = 1.702
SWIGLU_LIMIT = 7.0
N_MOD = 6
MOD_ROWS = 8

TOKEN_BLOCK = 256
WIDE_BLOCK = 512
HALO = 8
MOE_TILE = 256
V7X_VMEM_LIMIT = 56 * 1024 * 1024


def _cparams(n_axes):
    return pltpu.CompilerParams(
        dimension_semantics=("arbitrary",) * n_axes,
        vmem_limit_bytes=V7X_VMEM_LIMIT)


def _dot(a, b):
    return jnp.dot(a, b, preferred_element_type=F32)


def _dot_nt(a, b):
    return lax.dot_general(a, b, (((1,), (1,)), ((), ())), preferred_element_type=F32)


def _split_bf16(a):
    hi = a.astype(BF16)
    lo = (a - hi.astype(F32)).astype(BF16)
    return hi, lo


def _dot3(a, b):
    ah, al = _split_bf16(a)
    bh, bl = _split_bf16(b)
    return _dot(ah, bh) + _dot(al, bh) + _dot(ah, bl)


def _sigmoid(x):
    return 1.0 / (1.0 + jnp.exp(-x))


def _rms(x):
    return x * lax.rsqrt(jnp.mean(x * x, axis=-1, keepdims=True) + NORM_EPS)


class _Geom:
    def __init__(self, n_ctx_seq, ctx_len, n_lat_seq, lat_len):
        assert ctx_len == TOKEN_BLOCK, "one context sequence per token block"
        assert lat_len % WIDE_BLOCK == 0 and (n_ctx_seq * ctx_len) % lat_len == 0
        self.bc, self.lc, self.bl, self.ll = n_ctx_seq, ctx_len, n_lat_seq, lat_len
        self.nc = n_ctx_seq * ctx_len
        self.nl = n_lat_seq * lat_len
        self.n = self.nc + self.nl

    def group(self, i, tb):
        nbc = self.nc // tb
        return jnp.where(i < nbc, 0, 1 + (i - nbc) // (self.ll // tb))

    def pos_block(self, i, tb):
        nbc = self.nc // tb
        return jnp.where(i < nbc, 0, (i - nbc) % (self.ll // tb))

    def is_start(self, i):
        nbc, bpl = self.nc // TOKEN_BLOCK, self.ll // TOKEN_BLOCK
        return jnp.logical_or(i < nbc, (i - nbc) % bpl == 0)

    def is_end(self, i):
        nbc, bpl = self.nc // TOKEN_BLOCK, self.ll // TOKEN_BLOCK
        return jnp.logical_or(i < nbc, (i - nbc) % bpl == bpl - 1)

    def mod_spec(self, layer, tb):
        return pl.BlockSpec((1, 1, N_MOD * D_MODEL),
                            lambda i: (layer * MOD_ROWS + self.group(i, tb), 0, 0))

    def halo_specs(self, width):
        per = TOKEN_BLOCK // HALO
        last = self.n // HALO - 1
        before = pl.BlockSpec((HALO, width), lambda i: (jnp.maximum(i * per - 1, 0), 0))
        after = pl.BlockSpec((HALO, width), lambda i: (jnp.minimum((i + 1) * per, last), 0))
        return before, after

    def split_specs(self, tb, width):
        nbc = self.nc // tb
        last_lat = self.nl // tb - 1
        ctx = pl.BlockSpec((tb, width), lambda i: (jnp.minimum(i, nbc - 1), 0))
        lat = pl.BlockSpec((tb, width), lambda i: (jnp.clip(i - nbc, 0, last_lat), 0))
        return ctx, lat


def _layer_spec(a, layer, single_buffer=False):
    kw = dict(pipeline_mode=pl.Buffered(1)) if single_buffer else {}
    return pl.BlockSpec((None,) + a.shape[1:], lambda *_: (layer,) + (0,) * (a.ndim - 1), **kw)


def _ada_kernel(c_ref, w_ref, b_ref, o_ref):
    c = c_ref[...]
    s = c * _sigmoid(c)
    o_ref[0] = _dot(s.astype(BF16), w_ref[0].astype(BF16)) + b_ref[0]


def _ada(cond, w_ada, b_ada):
    depth, d, n6 = w_ada.shape
    rows = cond.shape[0]
    tn = 1024
    return pl.pallas_call(
        _ada_kernel,
        grid=(depth, n6 // tn),
        in_specs=[pl.BlockSpec((rows, d), lambda l, j: (0, 0)),
                  pl.BlockSpec((1, d, tn), lambda l, j: (l, 0, j)),
                  pl.BlockSpec((1, 1, tn), lambda l, j: (l, 0, j))],
        out_specs=pl.BlockSpec((1, rows, tn), lambda l, j: (l, 0, j)),
        out_shape=jax.ShapeDtypeStruct((depth, rows, n6), F32),
        compiler_params=_cparams(2),
        name="ada_mod",
    )(cond, w_ada, b_ada.reshape(depth, 1, n6))


def _rope(x, cos, sin_signed, first_half):
    d = x.shape[-1]
    partner = jnp.where(first_half, pltpu.roll(x, d - 16, 1), pltpu.roll(x, 16, 1))
    return x * cos + partner * sin_signed


def _inproj_kernel(x_ref, mod_ref, g_ref, w_ref, cos_ref, sin_ref, kacc_ref, vacc_ref,
                   up_ref, q_ref, kb_ref, vb_ref, kf_ref, vf_ref, uh_ref, *, nbc, lc):
    del kacc_ref, vacc_ref
    i = pl.program_id(0)
    d = D_MODEL
    mod = mod_ref[0]
    shift, scale = mod[:, 0:d], mod[:, d:2 * d]
    h = (_rms(x_ref[...]) * g_ref[...] * (1.0 + scale) + shift).astype(BF16)

    c1 = POOL_WIDTH
    c2 = c1 + DA_WIDTH
    c3 = c2 + DA_WIDTH
    c4 = c3 + DA_WIDTH
    up_ref[...] = _dot(h, w_ref[:, 0:c1])
    uh_ref[...] = _dot(h, w_ref[:, c4:D_IN])
    q = _dot(h, w_ref[:, c1:c2]) * (DA_HEAD_DIM ** -0.5)
    k = _dot(h, w_ref[:, c2:c3])
    v = _dot(h, w_ref[:, c3:c4])
    vb_ref[...] = v.astype(BF16)

    @pl.when(i < nbc)
    def _():
        q_ref[...] = q.astype(BF16)
        kb_ref[...] = k.astype(BF16)
        for s in range(k.shape[0] // lc):
            kf_ref[s, 0] = k[s * lc:(s + 1) * lc]
            vf_ref[s, 0] = v[s * lc:(s + 1) * lc]

    @pl.when(i >= nbc)
    def _():
        cos, sin_signed = cos_ref[...], sin_ref[...]
        lane = lax.broadcasted_iota(jnp.int32, q.shape, 1)
        first_half = (lane % 32) < 16
        q_ref[...] = _rope(q, cos, sin_signed, first_half).astype(BF16)
        kb_ref[...] = _rope(k, cos, sin_signed, first_half).astype(BF16)


def _inproj(geo, layer, x, mods, g1, w_in_b, cos_t, sin_t, kacc, vacc):
    d = D_MODEL
    tb = WIDE_BLOCK
    nbc = geo.nc // tb
    spb = tb // geo.lc
    row = lambda w: pl.BlockSpec((tb, w), lambda i: (i, 0))
    cache = pl.BlockSpec((spb, 1, geo.lc, DA_WIDTH), lambda i: (jnp.minimum(i, nbc - 1), layer, 0, 0))
    tab = pl.BlockSpec((tb, DA_WIDTH), lambda i: (geo.pos_block(i, tb), 0))
    hbm = pl.BlockSpec(memory_space=pl.ANY)
    return pl.pallas_call(
        functools.partial(_inproj_kernel, nbc=nbc, lc=geo.lc),
        grid=(geo.n // tb,),
        in_specs=[row(d), geo.mod_spec(layer, tb), _layer_spec(g1, layer),
                  _layer_spec(w_in_b, layer, single_buffer=True), tab, tab, hbm, hbm],
        out_specs=[row(POOL_WIDTH), row(DA_WIDTH), row(DA_WIDTH), row(DA_WIDTH),
                   cache, cache, row(3 * HY_WIDTH)],
        out_shape=[jax.ShapeDtypeStruct((geo.n, POOL_WIDTH), F32),
                   jax.ShapeDtypeStruct((geo.n, DA_WIDTH), BF16),
                   jax.ShapeDtypeStruct((geo.n, DA_WIDTH), BF16),
                   jax.ShapeDtypeStruct((geo.n, DA_WIDTH), BF16),
                   jax.ShapeDtypeStruct(kacc.shape, F32),
                   jax.ShapeDtypeStruct(vacc.shape, F32),
                   jax.ShapeDtypeStruct((geo.n, 3 * HY_WIDTH), F32)],
        input_output_aliases={6: 4, 7: 5},
        compiler_params=_cparams(1),
        name="in_proj",
    )(x, mods, g1, w_in_b, cos_t, sin_t, kacc, vacc)


def _fill_padded(pad_ref, before_ref, main_ref, after_ref, start, end):
    tb = TOKEN_BLOCK
    zero = jnp.zeros(before_ref.shape, F32)
    pad_ref[0:HALO, :] = jnp.where(start, zero, before_ref[...])
    pad_ref[HALO:HALO + tb, :] = main_ref[...]
    pad_ref[HALO + tb:2 * HALO + tb, :] = jnp.where(end, zero, after_ref[...])


def _pool_kernel(main_ref, before_ref, after_ref, w_ref, s_ref, o_ref, pad_ref, *, geo):
    i = pl.program_id(0)
    tb = TOKEN_BLOCK
    start, end = geo.is_start(i), geo.is_end(i)
    _fill_padded(pad_ref, before_ref, main_ref, after_ref, start, end)
    r = lax.broadcasted_iota(jnp.int32, (tb, 1), 0)
    for g, w in enumerate(POOL_WINDOWS):
        cols = slice(g * POOL_GC, (g + 1) * POOL_GC)
        acc = pad_ref[HALO - w // 2:HALO - w // 2 + tb, cols]
        for j in range(-w // 2 + 1, w // 2):
            acc = acc + pad_ref[HALO + j:HALO + j + tb, cols]
        lo = jnp.where(start, jnp.maximum(r - w // 2, 0), r - w // 2)
        hi = jnp.where(end, jnp.minimum(r + w // 2, tb), r + w // 2)
        mean = acc / (hi - lo).astype(F32)
        dlt = mean - main_ref[:, cols]
        y = _dot(dlt.astype(BF16), w_ref[g])
        o_ref[:, cols] = (y * s_ref[:, cols]).astype(BF16)


def _pool(geo, layer, u_pool, w_pool_b, pool_scale):
    tb = TOKEN_BLOCK
    before, after = geo.halo_specs(POOL_WIDTH)
    return pl.pallas_call(
        functools.partial(_pool_kernel, geo=geo),
        grid=(geo.n // tb,),
        in_specs=[pl.BlockSpec((tb, POOL_WIDTH), lambda i: (i, 0)), before, after,
                  _layer_spec(w_pool_b, layer), _layer_spec(pool_scale, layer)],
        out_specs=pl.BlockSpec((tb, POOL_WIDTH), lambda i: (i, 0)),
        out_shape=jax.ShapeDtypeStruct((geo.n, POOL_WIDTH), BF16),
        scratch_shapes=[pltpu.VMEM((tb + 2 * HALO, POOL_WIDTH), F32)],
        compiler_params=_cparams(1),
        name="pool_branch",
    )(u_pool, u_pool, u_pool, w_pool_b, pool_scale)


def _attn_kernel(*refs, lam_init, has_ctx):
    if has_ctx:
        q_ref, k_ref, v_ref, kc_ref, vc_ref, lam_ref, g_ref, o_ref = refs
    else:
        q_ref, k_ref, v_ref, lam_ref, g_ref, o_ref = refs
    lp = lam_ref[...]
    lam = (jnp.exp(jnp.sum(lp[0:1] * lp[1:2], axis=-1, keepdims=True))
           - jnp.exp(jnp.sum(lp[2:3] * lp[3:4], axis=-1, keepdims=True)) + lam_init)
    for h in range(DA_HEADS):
        vcols = slice(h * DA_VDIM, (h + 1) * DA_VDIM)
        v = v_ref[0, :, vcols]
        if has_ctx:
            vc = vc_ref[0, 0, :, vcols].astype(BF16)
        outs = []
        for sub in range(2):
            c0 = h * DA_VDIM + sub * DA_HEAD_DIM
            cols = slice(c0, c0 + DA_HEAD_DIM)
            qs = q_ref[0, :, cols]
            s1 = _dot_nt(qs, k_ref[0, :, cols])
            m = jnp.max(s1, axis=-1, keepdims=True)
            if has_ctx:
                s2 = _dot_nt(qs, kc_ref[0, 0, :, cols].astype(BF16))
                m = jnp.maximum(m, jnp.max(s2, axis=-1, keepdims=True))
            e1 = jnp.exp(s1 - m)
            den = jnp.sum(e1, axis=-1, keepdims=True)
            pv = _dot(e1.astype(BF16), v)
            if has_ctx:
                e2 = jnp.exp(s2 - m)
                den = den + jnp.sum(e2, axis=-1, keepdims=True)
                pv = pv + _dot(e2.astype(BF16), vc)
            outs.append(pv / den)
        o = outs[0] - lam * outs[1]
        o = _rms(o) * g_ref[...] * (1.0 - lam_init)
        o_ref[0, :, vcols] = o.astype(BF16)


def _attn_ctx(geo, layer, q, kb, vb, lam_p, subln, lam_init):
    bc, lc = geo.bc, geo.lc
    seq = pl.BlockSpec((1, lc, DA_WIDTH), lambda b: (b, 0, 0))
    view = lambda a: a.reshape(geo.n // lc, lc, DA_WIDTH)
    out = pl.pallas_call(
        functools.partial(_attn_kernel, lam_init=lam_init, has_ctx=False),
        grid=(bc,),
        in_specs=[seq, seq, seq, _layer_spec(lam_p, layer), _layer_spec(subln, layer)],
        out_specs=seq,
        out_shape=jax.ShapeDtypeStruct((bc, lc, DA_WIDTH), BF16),
        compiler_params=_cparams(1),
        name="attn_ctx",
    )(view(q), view(kb), view(vb), lam_p, subln)
    return out.reshape(bc * lc, DA_WIDTH)


def _attn_lat(geo, layer, q, kb, vb, cache_k, cache_v, lam_p, subln, lam_init):
    bl, ll = geo.bl, geo.ll
    tq = TOKEN_BLOCK
    past = cache_k.shape[2]
    off = geo.nc // ll
    qblk = pl.BlockSpec((1, tq, DA_WIDTH), lambda b, j: (b + off, j, 0))
    seq = pl.BlockSpec((1, ll, DA_WIDTH), lambda b, j: (b + off, 0, 0))
    cache = pl.BlockSpec((1, 1, past, DA_WIDTH), lambda b, j: (b, layer, 0, 0))
    view = lambda a: a.reshape(geo.n // ll, ll, DA_WIDTH)
    out = pl.pallas_call(
        functools.partial(_attn_kernel, lam_init=lam_init, has_ctx=True),
        grid=(bl, ll // tq),
        in_specs=[qblk, seq, seq, cache, cache, _layer_spec(lam_p, layer), _layer_spec(subln, layer)],
        out_specs=pl.BlockSpec((1, tq, DA_WIDTH), lambda b, j: (b, j, 0)),
        out_shape=jax.ShapeDtypeStruct((bl, ll, DA_WIDTH), BF16),
        compiler_params=_cparams(2),
        name="attn_lat",
    )(view(q), view(kb), view(vb), cache_k, cache_v, lam_p, subln)
    return out.reshape(bl * ll, DA_WIDTH)


@functools.lru_cache(maxsize=None)
def _dft_tables(length):
    k = np.arange(length, dtype=np.int64)
    ks = (k[:, None] * k[None, :]) % (2 * length)
    ang = ks.astype(np.float64) * (np.pi / length)
    cmat = np.cos(ang)
    smat = -np.sin(ang)
    smat[0, :] = 1.0 - 2.0 * (k % 2)
    to_bf16 = lambda a: jnp.asarray(a.astype(np.float32)).astype(BF16)
    return to_bf16(cmat), to_bf16(smat), to_bf16(smat.T)


@functools.lru_cache(maxsize=None)
def _filter_features(length):
    t = np.linspace(0.0, 1.0, length, dtype=np.float32)
    w_ang = (2.0 * math.pi * np.arange(length, dtype=np.float32) / length).astype(np.float32)
    f = np.linspace(1e-4, HY_BANDS - 1, HY_BANDS, dtype=np.float32)
    arg = (w_ang[:, None] * f[None, :]).astype(np.float32).astype(np.float64)
    z = np.concatenate([t[:, None].astype(np.float64), np.cos(arg), -np.sin(arg)], axis=-1)
    z = np.pad(z, ((0, 0), (0, 64 - HY_EMB))).astype(np.float32)
    rev = np.concatenate([z[:1], z[:0:-1]], axis=0)
    deltas = np.linspace(math.log(HY_TARGET) / HY_FAST, math.log(HY_TARGET) / HY_SLOW,
                         HY_WIDTH, dtype=np.float32)
    return jnp.asarray(z), jnp.asarray(rev), jnp.asarray(np.abs(deltas)[None, :])


def _hyfilter_kernel(z_ref, zr_ref, dl_ref, w1_ref, b1_ref, w2_ref, b2_ref, w3_ref, fr_ref, o_ref):
    c = pl.program_id(0)
    fr = fr_ref[...]

    def mlp(z, w3):
        hid = jnp.sin(fr[0:1] * (_dot3(z, w1_ref[...]) + b1_ref[...]))
        hid = jnp.sin(fr[1:2] * (_dot3(hid, w2_ref[...]) + b2_ref[...]))
        return _dot3(hid, w3) * jnp.exp(-z[:, 0:1] * dl_ref[...])

    z, zr = z_ref[...], zr_ref[...]
    o_ref[0, :, 0:HY_WIDTH] = mlp(z, w3_ref[:, 0:HY_WIDTH]).astype(BF16)
    bwd = mlp(zr, w3_ref[:, HY_WIDTH:2 * HY_WIDTH])
    row = lax.broadcasted_iota(jnp.int32, bwd.shape, 0) + c * z.shape[0]
    o_ref[0, :, HY_WIDTH:2 * HY_WIDTH] = jnp.where(row == 0, 0.0, bwd).astype(BF16)


def _hyfilter(length, layer, w1p, b1, w2, b2, w3, fr):
    z, zr, dl = _filter_features(length)
    tb = TOKEN_BLOCK
    blk = pl.BlockSpec((tb, 64), lambda c: (c, 0))
    lay = lambda a: _layer_spec(a, layer)
    return pl.pallas_call(
        _hyfilter_kernel,
        grid=(length // tb,),
        in_specs=[blk, blk, pl.BlockSpec(dl.shape, lambda c: (0, 0)),
                  lay(w1p), lay(b1), lay(w2), lay(b2), lay(w3), lay(fr)],
        out_specs=pl.BlockSpec((1, tb, 2 * HY_WIDTH), lambda c: (0, c, 0)),
        out_shape=jax.ShapeDtypeStruct((1, length, 2 * HY_WIDTH), BF16),
        compiler_params=_cparams(1),
        name="hyena_filter",
    )(z, zr, dl, w1p, b1, w2, b2, w3, fr)


def _hyconv_kernel(main_ref, before_ref, after_ref, cw_ref, cb_ref, z_ref, zb_ref, x0_ref, pad_ref, *, geo):
    i = pl.program_id(0)
    tb = TOKEN_BLOCK
    _fill_padded(pad_ref, before_ref, main_ref, after_ref, geo.is_start(i), geo.is_end(i))
    w = HY_WIDTH
    parts = []
    for p in range(3):
        cols = slice(p * w, (p + 1) * w)
        uc = (pad_ref[HALO - 1:HALO - 1 + tb, cols] * cw_ref[0:1, cols]
              + main_ref[:, cols] * cw_ref[1:2, cols]
              + pad_ref[HALO + 1:HALO + 1 + tb, cols] * cw_ref[2:3, cols]
              + cb_ref[:, cols])
        parts.append(uc)
    x0, x1, v = parts
    z = v * x1
    z_ref[...] = z
    zb_ref[...] = z.astype(BF16)
    x0_ref[...] = x0


def _hyconv(geo, layer, u_hy, conv_w, conv_b):
    tb = TOKEN_BLOCK
    w3 = 3 * HY_WIDTH
    before, after = geo.halo_specs(w3)
    row = pl.BlockSpec((tb, HY_WIDTH), lambda i: (i, 0))
    return pl.pallas_call(
        functools.partial(_hyconv_kernel, geo=geo),
        grid=(geo.n // tb,),
        in_specs=[pl.BlockSpec((tb, w3), lambda i: (i, 0)), before, after,
                  _layer_spec(conv_w, layer), _layer_spec(conv_b, layer)],
        out_specs=[row, row, row],
        out_shape=[jax.ShapeDtypeStruct((geo.n, HY_WIDTH), F32),
                   jax.ShapeDtypeStruct((geo.n, HY_WIDTH), BF16),
                   jax.ShapeDtypeStruct((geo.n, HY_WIDTH), F32)],
        scratch_shapes=[pltpu.VMEM((tb + 2 * HALO, w3), F32)],
        compiler_params=_cparams(1),
        name="hyena_conv_gate",
    )(u_hy, u_hy, u_hy, conv_w, conv_b)


def _dft_fwd_kernel(c_ref, s_ref, x_ref, re_ref, im_ref):
    x = x_ref[0]
    re_ref[0] = _dot(c_ref[...], x)
    im_ref[0] = _dot(s_ref[...], x)


def _dft_fwd(length, x, nseq, seq_off):
    cmat, smat, _ = _dft_tables(length)
    n = x.shape[-1]
    tm = min(length, 512)
    tn = 512
    a_spec = pl.BlockSpec((tm, length), lambda b, j, m: (m, 0))
    o_spec = pl.BlockSpec((1, tm, tn), lambda b, j, m: (b, m, j))
    return pl.pallas_call(
        _dft_fwd_kernel,
        grid=(nseq, n // tn, length // tm),
        in_specs=[a_spec, a_spec, pl.BlockSpec((1, length, tn), lambda b, j, m: (b + seq_off, 0, j))],
        out_specs=[o_spec, o_spec],
        out_shape=[jax.ShapeDtypeStruct((nseq, length, n), F32)] * 2,
        compiler_params=_cparams(3),
        name="hyena_dft",
    )(cmat, smat, x)


def _hyprod_kernel(zr_ref, zi_ref, hfr_ref, hgr_ref, hfi_ref, hgi_ref, yr_ref, yi_ref, *, length):
    c = pl.program_id(1)
    shape = zr_ref.shape[1:]
    k = lax.broadcasted_iota(jnp.int32, shape, 0) + c * shape[0]
    sgn = (1 - 2 * (k % 2)).astype(F32)
    hr = hfr_ref[0] + sgn * hgr_ref[0]
    hi = hfi_ref[0] + sgn * hgi_ref[0]
    zr, zi = zr_ref[0], zi_ref[0]
    inv = 1.0 / length
    yr = (zr * hr - zi * hi) * inv
    yi = (zr * hi + zi * hr) * inv
    first = k == 0
    yr_ref[0] = jnp.where(first, zr * hr * (0.5 * inv), yr).astype(BF16)
    yi_ref[0] = jnp.where(first, zi * hi * (0.5 * inv), yi).astype(BF16)


def _hyprod(length, zre, zim, hre, him):
    nb = zre.shape[0]
    tb = TOKEN_BLOCK
    zs = pl.BlockSpec((1, tb, HY_WIDTH), lambda b, c: (b, c, 0))
    hf = pl.BlockSpec((1, tb, HY_WIDTH), lambda b, c: (0, c, 0))
    hg = pl.BlockSpec((1, tb, HY_WIDTH), lambda b, c: (0, c, 1))
    return pl.pallas_call(
        functools.partial(_hyprod_kernel, length=length),
        grid=(nb, length // tb),
        in_specs=[zs, zs, hf, hg, hf, hg],
        out_specs=[zs, zs],
        out_shape=[jax.ShapeDtypeStruct((nb, length, HY_WIDTH), BF16)] * 2,
        compiler_params=_cparams(2),
        name="hyena_spectral_product",
    )(zre, zim, hre, hre, him, him)


def _hyinv_kernel(c_ref, st_ref, yr_ref, yi_ref, z_ref, x0_ref, skip_ref, o_ref):
    y = _dot(c_ref[...], yr_ref[0]) + _dot(st_ref[...], yi_ref[0])
    o_ref[0] = ((y + z_ref[0] * skip_ref[...]) * x0_ref[0]).astype(BF16)


def _hyinv(length, layer, yr, yi, z, x0, skip, seq_off):
    cmat, _, smat_t = _dft_tables(length)
    nseq = yr.shape[0]
    tm = min(length, 512)
    a_spec = pl.BlockSpec((tm, length), lambda b, m: (m, 0))
    y_spec = pl.BlockSpec((1, length, HY_WIDTH), lambda b, m: (b, 0, 0))
    t_spec = pl.BlockSpec((1, tm, HY_WIDTH), lambda b, m: (b + seq_off, m, 0))
    return pl.pallas_call(
        _hyinv_kernel,
        grid=(nseq, length // tm),
        in_specs=[a_spec, a_spec, y_spec, y_spec, t_spec, t_spec, _layer_spec(skip, layer)],
        out_specs=pl.BlockSpec((1, tm, HY_WIDTH), lambda b, m: (b, m, 0)),
        out_shape=jax.ShapeDtypeStruct((nseq, length, HY_WIDTH), BF16),
        compiler_params=_cparams(2),
        name="hyena_idft",
    )(cmat, smat_t, yr, yi, z, x0, skip)


def _hyena_long_conv(length, layer, nseq, seq_off, zb, z, x0, filt, skip):
    view = lambda a: a.reshape(a.shape[0] // length, length, HY_WIDTH)
    zre, zim = _dft_fwd(length, view(zb), nseq, seq_off)
    hre, him = _dft_fwd(length, filt, 1, 0)
    yr, yi = _hyprod(length, zre, zim, hre, him)
    out = _hyinv(length, layer, yr, yi, view(z), view(x0), skip, seq_off)
    return out.reshape(nseq * length, HY_WIDTH)


def _merge_kernel(x_ref, mod_ref, g1_ref, a_ref, bc_ref, bl_ref, cc_ref, cl_ref, wg_ref, bg_ref, wbr_ref,
                  wo_ref, g2_ref, wr_ref, br_ref, xo_ref, h2_ref, ti_ref, tp_ref, *, nbc):
    d = D_MODEL
    is_ctx = pl.program_id(0) < nbc
    mod = mod_ref[0]
    shift1, scale1, gate1 = mod[:, 0:d], mod[:, d:2 * d], mod[:, 2 * d:3 * d]
    shift2, scale2 = mod[:, 3 * d:4 * d], mod[:, 4 * d:5 * d]
    x = x_ref[...]
    h = (_rms(x) * g1_ref[...] * (1.0 + scale1) + shift1).astype(BF16)
    branches = (a_ref[...],
                jnp.where(is_ctx, bc_ref[...], bl_ref[...]),
                jnp.where(is_ctx, cc_ref[...], cl_ref[...]))
    merged = None
    for n, br_n in enumerate(branches):
        cols = slice(n * d, (n + 1) * d)
        gate = _sigmoid(_dot(h, wg_ref[:, cols]) + bg_ref[:, cols])
        term = gate * _dot(br_n, wbr_ref[n])
        merged = term if merged is None else merged + term
    x = x + gate1 * _dot(merged.astype(BF16), wo_ref[...])
    xo_ref[...] = x
    h2 = _rms(x) * g2_ref[...] * (1.0 + scale2) + shift2
    h2_ref[...] = h2.astype(BF16)

    logits = _dot3(h2, wr_ref[...]) + br_ref[...]
    ne = logits.shape[-1]
    lane = lax.broadcasted_iota(jnp.int32, logits.shape, 1).astype(F32)
    vals = logits
    top_v, top_i = [], []
    for _ in range(TOP_K):
        m = jnp.max(vals, axis=-1, keepdims=True)
        idx = jnp.min(jnp.where(vals == m, lane, float(ne)), axis=-1, keepdims=True)
        top_v.append(m)
        top_i.append(idx)
        vals = jnp.where(lane == idx, -jnp.inf, vals)
    es = [jnp.exp(v - top_v[0]) for v in top_v]
    den = es[0] + es[1] + es[2] + es[3]
    for kk in range(TOP_K):
        ti_ref[:, kk:kk + 1] = top_i[kk].astype(jnp.int32)
        tp_ref[:, kk:kk + 1] = es[kk] / den


def _merge(geo, layer, x, mods, g1, a_out, b_ctx, b_lat, c_ctx, c_lat, wg_b, bg, wbr_b, wo_b, g2, wr, br):
    d = D_MODEL
    tb = WIDE_BLOCK
    row = lambda w: pl.BlockSpec((tb, w), lambda i: (i, 0))
    ctx, lat = geo.split_specs(tb, DA_WIDTH)
    lay = lambda a: _layer_spec(a, layer, single_buffer=True)
    return pl.pallas_call(
        functools.partial(_merge_kernel, nbc=geo.nc // tb),
        grid=(geo.n // tb,),
        in_specs=[row(d), geo.mod_spec(layer, tb), lay(g1), row(POOL_WIDTH), ctx, lat, ctx, lat,
                  lay(wg_b), lay(bg), lay(wbr_b), lay(wo_b), lay(g2), lay(wr), lay(br)],
        out_specs=[row(d), row(d), row(TOP_K), row(TOP_K)],
        out_shape=[jax.ShapeDtypeStruct((geo.n, d), F32),
                   jax.ShapeDtypeStruct((geo.n, d), BF16),
                   jax.ShapeDtypeStruct((geo.n, TOP_K), jnp.int32),
                   jax.ShapeDtypeStruct((geo.n, TOP_K), F32)],
        compiler_params=_cparams(1),
        name="merge_route",
    )(x, mods, g1, a_out, b_ctx, b_lat, c_ctx, c_lat, wg_b, bg, wbr_b, wo_b, g2, wr, br)


def _ffn_kernel(te_ref, nv_ref, x_ref, wgu_ref, bgu_ref, wd_ref, bd_ref, y_ref, wgub_ref, wdb_ref):
    t = pl.program_id(0)
    e = te_ref[t]
    prev = te_ref[jnp.maximum(t - 1, 0)]
    valid = t < nv_ref[0]
    first = jnp.logical_or(t == 0, e != prev)
    chunk = 128

    @pl.when(jnp.logical_and(valid, first))
    def _():
        def cast_gu(c, carry):
            rows = pl.ds(pl.multiple_of(c * chunk, chunk), chunk)
            wgub_ref[rows, :] = wgu_ref[0, rows, :].astype(BF16)
            return carry
        lax.fori_loop(0, wgu_ref.shape[1] // chunk, cast_gu, 0)

        def cast_d(c, carry):
            rows = pl.ds(pl.multiple_of(c * chunk, chunk), chunk)
            wdb_ref[rows, :] = wd_ref[0, rows, :].astype(BF16)
            return carry
        lax.fori_loop(0, wd_ref.shape[1] // chunk, cast_d, 0)

    @pl.when(valid)
    def _():
        gu = _dot(x_ref[...], wgub_ref[...]) + bgu_ref[0]
        gate = jnp.minimum(gu[:, 0:D_FF], SWIGLU_LIMIT)
        up = jnp.clip(gu[:, D_FF:2 * D_FF], -SWIGLU_LIMIT, SWIGLU_LIMIT)
        act = (up + 1.0) * gate * _sigmoid(SWIGLU_ALPHA * gate)
        y = _dot(act.astype(BF16), wdb_ref[...]) + bd_ref[0]
        y_ref[...] = y.astype(BF16)


def _ffn(xs, tile_weight, n_valid, w_gu, b_gu, w_down, b_down):
    p, d = xs.shape
    t = MOE_TILE
    tile = pl.BlockSpec((t, d), lambda i, te, nv: (jnp.minimum(i, nv[0] - 1), 0))
    grid_spec = pltpu.PrefetchScalarGridSpec(
        num_scalar_prefetch=2,
        grid=(p // t,),
        in_specs=[tile,
                  pl.BlockSpec((1, d, 2 * D_FF), lambda i, te, nv: (te[i], 0, 0)),
                  pl.BlockSpec((1, 1, 2 * D_FF), lambda i, te, nv: (te[i], 0, 0)),
                  pl.BlockSpec((1, D_FF, d), lambda i, te, nv: (te[i], 0, 0)),
                  pl.BlockSpec((1, 1, d), lambda i, te, nv: (te[i], 0, 0))],
        out_specs=tile,
        scratch_shapes=[pltpu.VMEM((d, 2 * D_FF), BF16), pltpu.VMEM((D_FF, d), BF16)])
    return pl.pallas_call(
        _ffn_kernel,
        grid_spec=grid_spec,
        out_shape=jax.ShapeDtypeStruct((p, d), BF16),
        compiler_params=_cparams(1),
        name="moe_experts",
    )(tile_weight, n_valid, xs, w_gu, b_gu, w_down, b_down)


def _route(top_i, n_experts):
    n = top_i.shape[0]
    t = MOE_TILE
    n_pairs = n * TOP_K
    n_tiles = n_pairs // t + n_experts
    flat_e = top_i.reshape(-1)
    onehot = flat_e[:, None] == jnp.arange(n_experts, dtype=jnp.int32)[None, :]
    counts = jnp.sum(onehot, axis=0, dtype=jnp.int32)
    order = jnp.argsort(flat_e, stable=True).astype(jnp.int32)
    _, pair_pos = lax.sort_key_val(order, jnp.arange(n_pairs, dtype=jnp.int32))
    raw_start = jnp.cumsum(counts) - counts
    tiles_e = (counts + t - 1) // t
    tile_end = jnp.cumsum(tiles_e)
    pad_start = (tile_end - tiles_e) * t
    n_valid = tile_end[-1]
    pair_slot = jnp.sum(jnp.where(onehot, (pad_start - raw_start)[None, :], 0), axis=1) + pair_pos

    tile_ids = jnp.minimum(jnp.arange(n_tiles, dtype=jnp.int32), n_valid - 1)
    tile_onehot = (jnp.sum(tile_end[None, :] <= tile_ids[:, None], axis=1, dtype=jnp.int32)[:, None]
                   == jnp.arange(n_experts, dtype=jnp.int32)[None, :])
    tile_expert = jnp.argmax(tile_onehot, axis=1).astype(jnp.int32)
    per_tile = lambda table: jnp.sum(jnp.where(tile_onehot, table[None, :], 0), axis=1)
    rank = (jnp.arange(n_tiles * t, dtype=jnp.int32).reshape(n_tiles, t) - per_tile(pad_start)[:, None])
    src = jnp.where(rank < per_tile(counts)[:, None], per_tile(raw_start)[:, None] + rank, 0)
    slot_token = jnp.take(order, src.reshape(-1), mode="clip") // TOP_K
    return slot_token, pair_slot.reshape(n, TOP_K), tile_expert, n_valid.reshape(1).astype(jnp.int32)


def _moe_sum(x_ref, mod_ref, yg_ref, p_ref):
    d = D_MODEL
    gate2 = mod_ref[0][:, 5 * d:6 * d]
    p = p_ref[...]
    moe = p[:, 0:1] * yg_ref[0].astype(F32)
    for kk in range(1, TOP_K):
        moe = moe + p[:, kk:kk + 1] * yg_ref[kk].astype(F32)
    return x_ref[...] + gate2 * moe


def _combine_kernel(x_ref, mod_ref, yg_ref, p_ref, xo_ref):
    xo_ref[...] = _moe_sum(x_ref, mod_ref, yg_ref, p_ref)


def _combine_final_kernel(x_ref, mod_ref, yg_ref, p_ref, gf_ref, yc_ref, yl_ref, *, nbc):
    i = pl.program_id(0)
    y = _rms(_moe_sum(x_ref, mod_ref, yg_ref, p_ref)) * gf_ref[...]

    @pl.when(i < nbc)
    def _():
        yc_ref[...] = y

    @pl.when(i >= nbc)
    def _():
        yl_ref[...] = y


def _combine(geo, layer, x, mods, yg, top_p, final_g=None):
    d = D_MODEL
    tb = TOKEN_BLOCK
    row = pl.BlockSpec((tb, d), lambda i: (i, 0))
    in_specs = [row, geo.mod_spec(layer, tb),
                pl.BlockSpec((TOP_K, tb, d), lambda i: (0, i, 0)),
                pl.BlockSpec((tb, TOP_K), lambda i: (i, 0))]
    if final_g is None:
        return pl.pallas_call(
            _combine_kernel, grid=(geo.n // tb,), in_specs=in_specs, out_specs=row,
            out_shape=jax.ShapeDtypeStruct((geo.n, d), F32),
            compiler_params=_cparams(1), name="moe_combine",
        )(x, mods, yg, top_p)
    ctx, lat = geo.split_specs(tb, d)
    return pl.pallas_call(
        functools.partial(_combine_final_kernel, nbc=geo.nc // tb),
        grid=(geo.n // tb,),
        in_specs=in_specs + [pl.BlockSpec(final_g.shape, lambda i: (0, 0))],
        out_specs=[ctx, lat],
        out_shape=[jax.ShapeDtypeStruct((geo.nc, d), F32), jax.ShapeDtypeStruct((geo.nl, d), F32)],
        compiler_params=_cparams(1), name="moe_combine_final",
    )(x, mods, yg, top_p, final_g)


@functools.lru_cache(maxsize=None)
def _rope_tables(length):
    rows = length // GRID_W
    row = np.repeat(np.arange(rows), GRID_W).astype(np.float32)
    col = np.tile(np.arange(GRID_W), rows).astype(np.float32)
    ax = DA_HEAD_DIM // 2
    inv = (ROPE_BASE ** (-(np.arange(ax // 2, dtype=np.float32) * 2.0 / ax))).astype(np.float32)
    ang_r = (row[:, None] * inv).astype(np.float32)
    ang_c = (col[:, None] * inv).astype(np.float32)
    ang = np.concatenate([ang_r, ang_r, ang_c, ang_c], axis=-1).astype(np.float64)
    sign = np.where((np.arange(DA_HEAD_DIM) % 32) < 16, -1.0, 1.0)
    reps = DA_WIDTH // DA_HEAD_DIM
    cos = np.tile(np.cos(ang), (1, reps)).astype(np.float32)
    sin_signed = np.tile(np.sin(ang) * sign[None, :], (1, reps)).astype(np.float32)
    return jnp.asarray(cos), jnp.asarray(sin_signed)


def kernel(x_prompt, x_sample, cache_k, cache_v, c, c_ctx, w_ada, b_ada, norm1, norm2, w_in, w_pool, pool_scale, da_lambda, da_subln, hy_conv_w, hy_conv_b, hy_f_w1, hy_f_b1, hy_f_w2, hy_f_b2, hy_f_w3, hy_sin_freq, hy_skip, w_branch, w_gate, b_gate, w_o, w_router, b_router, w_gu, b_gu, w_down, b_down, final_norm):
    bc, lc, d = x_prompt.shape
    bl, ll, _ = x_sample.shape
    depth = w_in.shape[0]
    n_experts = w_router.shape[-1]
    past = cache_k.shape[2]
    geo = _Geom(bc, lc, bl, ll)
    assert 1 + bl <= MOD_ROWS

    x = jnp.concatenate([x_prompt.reshape(bc * lc, d), x_sample.reshape(bl * ll, d)], axis=0)
    cond = jnp.concatenate([c_ctx[None], c, jnp.zeros((MOD_ROWS - 1 - bl, d), F32)], axis=0)
    mods = _ada(cond, w_ada, b_ada).reshape(depth * MOD_ROWS, 1, N_MOD * d)
    cos_t, sin_t = _rope_tables(ll)
    ck = cache_k.reshape(bl, depth, past, DA_WIDTH)
    cv = cache_v.reshape(bl, depth, past, DA_WIDTH)

    row3 = lambda a: a.reshape(depth, 1, a.shape[-1])
    g1, g2 = row3(norm1), row3(norm2)
    w_in_b = w_in.astype(BF16)
    w_pool_b = w_pool.astype(BF16)
    w_gate_b = w_gate.astype(BF16)
    w_branch_b = w_branch.astype(BF16)
    w_o_b = w_o.astype(BF16)
    w1p = jnp.pad(hy_f_w1, ((0, 0), (0, 64 - HY_EMB), (0, 0)))
    fargs = (w1p, row3(hy_f_b1), hy_f_w2, row3(hy_f_b2), hy_f_w3, hy_sin_freq)
    w_gu_s = w_gu.reshape(depth * n_experts, d, 2 * D_FF)
    b_gu_s = b_gu.reshape(depth * n_experts, 1, 2 * D_FF)
    w_down_s = w_down.reshape(depth * n_experts, D_FF, d)
    b_down_s = b_down.reshape(depth * n_experts, 1, d)

    new_k = jnp.zeros((bc, depth, lc, DA_WIDTH), F32)
    new_v = jnp.zeros((bc, depth, lc, DA_WIDTH), F32)
    for l in range(depth):
        u_pool, q, kb, vb, new_k, new_v, u_hy = _inproj(geo, l, x, mods, g1, w_in_b, cos_t, sin_t, new_k, new_v)
        a_out = _pool(geo, l, u_pool, w_pool_b, row3(pool_scale))

        lam_init = 0.8 - 0.6 * math.exp(-0.3 * l)
        subln = row3(da_subln)
        b_ctx = _attn_ctx(geo, l, q, kb, vb, da_lambda, subln, lam_init)
        b_lat = _attn_lat(geo, l, q, kb, vb, ck, cv, da_lambda, subln, lam_init)

        z, zb, x0 = _hyconv(geo, l, u_hy, hy_conv_w, row3(hy_conv_b))
        skip = row3(hy_skip)
        c_ctx_out = _hyena_long_conv(lc, l, bc, 0, zb, z, x0, _hyfilter(lc, l, *fargs), skip)
        c_lat_out = _hyena_long_conv(ll, l, bl, geo.nc // ll, zb, z, x0, _hyfilter(ll, l, *fargs), skip)

        x, h2, top_i, top_p = _merge(geo, l, x, mods, g1, a_out, b_ctx, b_lat, c_ctx_out, c_lat_out,
                                     w_gate_b, row3(b_gate), w_branch_b, w_o_b, g2, w_router, row3(b_router))

        slot_token, pair_slot, tile_expert, n_valid = _route(top_i, n_experts)
        xs = jnp.take(h2, slot_token, axis=0, mode="clip")
        ys = _ffn(xs, tile_expert + l * n_experts, n_valid, w_gu_s, b_gu_s, w_down_s, b_down_s)
        yg = jnp.take(ys, pair_slot.T, axis=0, mode="clip")
        if l + 1 < depth:
            x = _combine(geo, l, x, mods, yg, top_p)
        else:
            y_ctx, y_lat = _combine(geo, l, x, mods, yg, top_p, final_norm[None])

    return (y_ctx.reshape(bc, lc, d), y_lat.reshape(bl, ll, d),
            new_k.reshape(bc, depth, lc, DA_HEADS, 2 * DA_HEAD_DIM),
            new_v.reshape(bc, depth, lc, DA_HEADS, DA_VDIM))
```

```python
import functools
import math

import numpy as np
import jax
import jax.numpy as jnp
from jax import lax
from jax.experimental import pallas as pl
from jax.experimental.pallas import tpu as pltpu

F32 = jnp.float32
BF16 = jnp.bfloat16

D_MODEL = 1024
GRID_W = 64
NORM_EPS = 1e-6
POOL_WIDTH = 512
POOL_WINDOWS = (2, 4, 8, 16)
POOL_GC = POOL_WIDTH // len(POOL_WINDOWS)
DA_HEADS = 4
DA_HEAD_DIM = 64
DA_VDIM = 2 * DA_HEAD_DIM
DA_WIDTH = DA_HEADS * DA_VDIM
ROPE_BASE = 10000.0
HY_WIDTH = 512
HY_EMB = 33
HY_BANDS = (HY_EMB - 1) // 2
HY_HIDDEN = 64
HY_FAST = 0.3
HY_SLOW = 1.5
HY_TARGET = 1e-2
N_BRANCH = 3
D_IN = POOL_WIDTH + 3 * DA_WIDTH + 3 * HY_WIDTH
TOP_K = 4
D_FF = 1024
SWIGLU_ALPHA = 1.702
SWIGLU_LIMIT = 7.0
N_MOD = 6
MOD_ROWS = 8

TOKEN_BLOCK = 256
WIDE_BLOCK = 512
HALO = 8
MOE_TILE = 256
MOE_MAX_SUB = 4
V7X_VMEM_LIMIT = 56 * 1024 * 1024


def _cparams(n_axes):
    return pltpu.CompilerParams(
        dimension_semantics=("arbitrary",) * n_axes,
        vmem_limit_bytes=V7X_VMEM_LIMIT)


def _dot(a, b):
    return jnp.dot(a, b, preferred_element_type=F32)


def _dot_nt(a, b):
    return lax.dot_general(a, b, (((1,), (1,)), ((), ())), preferred_element_type=F32)


def _split_bf16(a):
    hi = a.astype(BF16)
    lo = (a - hi.astype(F32)).astype(BF16)
    return hi, lo


def _dot3(a, b):
    ah, al = _split_bf16(a)
    bh, bl = _split_bf16(b)
    return _dot(ah, bh) + _dot(al, bh) + _dot(ah, bl)


def _sigmoid(x):
    return 1.0 / (1.0 + jnp.exp(-x))


def _rms(x):
    return x * lax.rsqrt(jnp.mean(x * x, axis=-1, keepdims=True) + NORM_EPS)


class _Geom:
    def __init__(self, n_ctx_seq, ctx_len, n_lat_seq, lat_len):
        assert ctx_len == TOKEN_BLOCK, "one context sequence per token block"
        assert lat_len % WIDE_BLOCK == 0 and (n_ctx_seq * ctx_len) % lat_len == 0
        self.bc, self.lc, self.bl, self.ll = n_ctx_seq, ctx_len, n_lat_seq, lat_len
        self.nc = n_ctx_seq * ctx_len
        self.nl = n_lat_seq * lat_len
        self.n = self.nc + self.nl

    def group(self, i, tb):
        nbc = self.nc // tb
        return jnp.where(i < nbc, 0, 1 + (i - nbc) // (self.ll // tb))

    def pos_block(self, i, tb):
        nbc = self.nc // tb
        return jnp.where(i < nbc, 0, (i - nbc) % (self.ll // tb))

    def is_start(self, i):
        nbc, bpl = self.nc // TOKEN_BLOCK, self.ll // TOKEN_BLOCK
        return jnp.logical_or(i < nbc, (i - nbc) % bpl == 0)

    def is_end(self, i):
        nbc, bpl = self.nc // TOKEN_BLOCK, self.ll // TOKEN_BLOCK
        return jnp.logical_or(i < nbc, (i - nbc) % bpl == bpl - 1)

    def mod_spec(self, layer, tb):
        return pl.BlockSpec((1, 1, N_MOD * D_MODEL),
                            lambda i: (layer * MOD_ROWS + self.group(i, tb), 0, 0))

    def halo_specs(self, width):
        per = TOKEN_BLOCK // HALO
        last = self.n // HALO - 1
        before = pl.BlockSpec((HALO, width), lambda i: (jnp.maximum(i * per - 1, 0), 0))
        after = pl.BlockSpec((HALO, width), lambda i: (jnp.minimum((i + 1) * per, last), 0))
        return before, after

    def split_specs(self, tb, width):
        nbc = self.nc // tb
        last_lat = self.nl // tb - 1
        ctx = pl.BlockSpec((tb, width), lambda i: (jnp.minimum(i, nbc - 1), 0))
        lat = pl.BlockSpec((tb, width), lambda i: (jnp.clip(i - nbc, 0, last_lat), 0))
        return ctx, lat


def _layer_spec(a, layer, single_buffer=False):
    kw = dict(pipeline_mode=pl.Buffered(1)) if single_buffer else {}
    return pl.BlockSpec((None,) + a.shape[1:], lambda *_: (layer,) + (0,) * (a.ndim - 1), **kw)


def _ada_kernel(c_ref, w_ref, b_ref, o_ref):
    c = c_ref[...]
    s = c * _sigmoid(c)
    o_ref[0] = _dot(s.astype(BF16), w_ref[0].astype(BF16)) + b_ref[0]


def _ada(cond, w_ada, b_ada):
    depth, d, n6 = w_ada.shape
    rows = cond.shape[0]
    tn = 1024
    return pl.pallas_call(
        _ada_kernel,
        grid=(depth, n6 // tn),
        in_specs=[pl.BlockSpec((rows, d), lambda l, j: (0, 0)),
                  pl.BlockSpec((1, d, tn), lambda l, j: (l, 0, j)),
                  pl.BlockSpec((1, 1, tn), lambda l, j: (l, 0, j))],
        out_specs=pl.BlockSpec((1, rows, tn), lambda l, j: (l, 0, j)),
        out_shape=jax.ShapeDtypeStruct((depth, rows, n6), F32),
        compiler_params=_cparams(2),
        name="ada_mod",
    )(cond, w_ada, b_ada.reshape(depth, 1, n6))


def _rope(x, cos, sin_signed, first_half):
    d = x.shape[-1]
    partner = jnp.where(first_half, pltpu.roll(x, d - 16, 1), pltpu.roll(x, 16, 1))
    return x * cos + partner * sin_signed


def _inproj_kernel(x_ref, mod_ref, g_ref, w_ref, cos_ref, sin_ref, kacc_ref, vacc_ref,
                   up_ref, q_ref, kb_ref, vb_ref, kf_ref, vf_ref, uh_ref, *, nbc, lc):
    del kacc_ref, vacc_ref
    i = pl.program_id(0)
    d = D_MODEL
    mod = mod_ref[0]
    shift, scale = mod[:, 0:d], mod[:, d:2 * d]
    h = (_rms(x_ref[...]) * g_ref[...] * (1.0 + scale) + shift).astype(BF16)

    c1 = POOL_WIDTH
    c2 = c1 + DA_WIDTH
    c3 = c2 + DA_WIDTH
    c4 = c3 + DA_WIDTH
    up_ref[...] = _dot(h, w_ref[:, 0:c1])
    uh_ref[...] = _dot(h, w_ref[:, c4:D_IN])
    q = _dot(h, w_ref[:, c1:c2]) * (DA_HEAD_DIM ** -0.5)
    k = _dot(h, w_ref[:, c2:c3])
    v = _dot(h, w_ref[:, c3:c4])
    vb_ref[...] = v.astype(BF16)

    @pl.when(i < nbc)
    def _():
        q_ref[...] = q.astype(BF16)
        kb_ref[...] = k.astype(BF16)
        for s in range(k.shape[0] // lc):
            for hh in range(DA_HEADS):
                rows = pl.ds(hh, lc, stride=DA_HEADS)
                cols = slice(hh * DA_VDIM, (hh + 1) * DA_VDIM)
                kf_ref[s, 0, rows, :] = k[s * lc:(s + 1) * lc, cols]
                vf_ref[s, 0, rows, :] = v[s * lc:(s + 1) * lc, cols]

    @pl.when(i >= nbc)
    def _():
        cos, sin_signed = cos_ref[...], sin_ref[...]
        lane = lax.broadcasted_iota(jnp.int32, q.shape, 1)
        first_half = (lane % 32) < 16
        q_ref[...] = _rope(q, cos, sin_signed, first_half).astype(BF16)
        kb_ref[...] = _rope(k, cos, sin_signed, first_half).astype(BF16)


def _inproj(geo, layer, x, mods, g1, w_in_b, cos_t, sin_t, kacc, vacc):
    d = D_MODEL
    tb = WIDE_BLOCK
    nbc = geo.nc // tb
    spb = tb // geo.lc
    row = lambda w: pl.BlockSpec((tb, w), lambda i: (i, 0))
    cache = pl.BlockSpec((spb, 1, geo.lc * DA_HEADS, DA_VDIM), lambda i: (jnp.minimum(i, nbc - 1), layer, 0, 0))
    tab = pl.BlockSpec((tb, DA_WIDTH), lambda i: (geo.pos_block(i, tb), 0))
    hbm = pl.BlockSpec(memory_space=pl.ANY)
    return pl.pallas_call(
        functools.partial(_inproj_kernel, nbc=nbc, lc=geo.lc),
        grid=(geo.n // tb,),
        in_specs=[row(d), geo.mod_spec(layer, tb), _layer_spec(g1, layer),
                  _layer_spec(w_in_b, layer, single_buffer=True), tab, tab, hbm, hbm],
        out_specs=[row(POOL_WIDTH), row(DA_WIDTH), row(DA_WIDTH), row(DA_WIDTH),
                   cache, cache, row(3 * HY_WIDTH)],
        out_shape=[jax.ShapeDtypeStruct((geo.n, POOL_WIDTH), F32),
                   jax.ShapeDtypeStruct((geo.n, DA_WIDTH), BF16),
                   jax.ShapeDtypeStruct((geo.n, DA_WIDTH), BF16),
                   jax.ShapeDtypeStruct((geo.n, DA_WIDTH), BF16),
                   jax.ShapeDtypeStruct(kacc.shape, F32),
                   jax.ShapeDtypeStruct(vacc.shape, F32),
                   jax.ShapeDtypeStruct((geo.n, 3 * HY_WIDTH), F32)],
        input_output_aliases={6: 4, 7: 5},
        compiler_params=_cparams(1),
        name="in_proj",
    )(x, mods, g1, w_in_b, cos_t, sin_t, kacc, vacc)


def _fill_padded(pad_ref, before_ref, main_ref, after_ref, start, end):
    tb = TOKEN_BLOCK
    zero = jnp.zeros(before_ref.shape, F32)
    pad_ref[0:HALO, :] = jnp.where(start, zero, before_ref[...])
    pad_ref[HALO:HALO + tb, :] = main_ref[...]
    pad_ref[HALO + tb:2 * HALO + tb, :] = jnp.where(end, zero, after_ref[...])


def _pool_kernel(main_ref, before_ref, after_ref, w_ref, s_ref, o_ref, pad_ref, *, geo):
    i = pl.program_id(0)
    tb = TOKEN_BLOCK
    start, end = geo.is_start(i), geo.is_end(i)
    _fill_padded(pad_ref, before_ref, main_ref, after_ref, start, end)
    r = lax.broadcasted_iota(jnp.int32, (tb, 1), 0)
    for g, w in enumerate(POOL_WINDOWS):
        cols = slice(g * POOL_GC, (g + 1) * POOL_GC)
        acc = pad_ref[HALO - w // 2:HALO - w // 2 + tb, cols]
        for j in range(-w // 2 + 1, w // 2):
            acc = acc + pad_ref[HALO + j:HALO + j + tb, cols]
        lo = jnp.where(start, jnp.maximum(r - w // 2, 0), r - w // 2)
        hi = jnp.where(end, jnp.minimum(r + w // 2, tb), r + w // 2)
        mean = acc / (hi - lo).astype(F32)
        dlt = mean - main_ref[:, cols]
        y = _dot(dlt.astype(BF16), w_ref[g])
        o_ref[:, cols] = (y * s_ref[:, cols]).astype(BF16)


def _pool(geo, layer, u_pool, w_pool_b, pool_scale):
    tb = TOKEN_BLOCK
    before, after = geo.halo_specs(POOL_WIDTH)
    return pl.pallas_call(
        functools.partial(_pool_kernel, geo=geo),
        grid=(geo.n // tb,),
        in_specs=[pl.BlockSpec((tb, POOL_WIDTH), lambda i: (i, 0)), before, after,
                  _layer_spec(w_pool_b, layer), _layer_spec(pool_scale, layer)],
        out_specs=pl.BlockSpec((tb, POOL_WIDTH), lambda i: (i, 0)),
        out_shape=jax.ShapeDtypeStruct((geo.n, POOL_WIDTH), BF16),
        scratch_shapes=[pltpu.VMEM((tb + 2 * HALO, POOL_WIDTH), F32)],
        compiler_params=_cparams(1),
        name="pool_branch",
    )(u_pool, u_pool, u_pool, w_pool_b, pool_scale)


def _attn_kernel(*refs, lam_init, has_ctx):
    if has_ctx:
        q_ref, k_ref, v_ref, kc_ref, vc_ref, lam_ref, g_ref, o_ref = refs
    else:
        q_ref, k_ref, v_ref, lam_ref, g_ref, o_ref = refs
    lp = lam_ref[...]
    lam = (jnp.exp(jnp.sum(lp[0:1] * lp[1:2], axis=-1, keepdims=True))
           - jnp.exp(jnp.sum(lp[2:3] * lp[3:4], axis=-1, keepdims=True)) + lam_init)
    for h in range(DA_HEADS):
        vcols = slice(h * DA_VDIM, (h + 1) * DA_VDIM)
        v = v_ref[0, :, vcols]
        if has_ctx:
            vc = vc_ref[0, 0, :, vcols].astype(BF16)
        outs = []
        for sub in range(2):
            c0 = h * DA_VDIM + sub * DA_HEAD_DIM
            cols = slice(c0, c0 + DA_HEAD_DIM)
            qs = q_ref[0, :, cols]
            s1 = _dot_nt(qs, k_ref[0, :, cols])
            m = jnp.max(s1, axis=-1, keepdims=True)
            if has_ctx:
                s2 = _dot_nt(qs, kc_ref[0, 0, :, cols].astype(BF16))
                m = jnp.maximum(m, jnp.max(s2, axis=-1, keepdims=True))
            e1 = jnp.exp(s1 - m)
            den = jnp.sum(e1, axis=-1, keepdims=True)
            pv = _dot(e1.astype(BF16), v)
            if has_ctx:
                e2 = jnp.exp(s2 - m)
                den = den + jnp.sum(e2, axis=-1, keepdims=True)
                pv = pv + _dot(e2.astype(BF16), vc)
            outs.append(pv / den)
        o = outs[0] - lam * outs[1]
        o = _rms(o) * g_ref[...] * (1.0 - lam_init)
        o_ref[0, :, vcols] = o.astype(BF16)


def _attn_ctx(geo, layer, q, kb, vb, lam_p, subln, lam_init):
    bc, lc = geo.bc, geo.lc
    seq = pl.BlockSpec((1, lc, DA_WIDTH), lambda b: (b, 0, 0))
    view = lambda a: a.reshape(geo.n // lc, lc, DA_WIDTH)
    out = pl.pallas_call(
        functools.partial(_attn_kernel, lam_init=lam_init, has_ctx=False),
        grid=(bc,),
        in_specs=[seq, seq, seq, _layer_spec(lam_p, layer), _layer_spec(subln, layer)],
        out_specs=seq,
        out_shape=jax.ShapeDtypeStruct((bc, lc, DA_WIDTH), BF16),
        compiler_params=_cparams(1),
        name="attn_ctx",
    )(view(q), view(kb), view(vb), lam_p, subln)
    return out.reshape(bc * lc, DA_WIDTH)


def _attn_lat(geo, layer, q, kb, vb, cache_k, cache_v, lam_p, subln, lam_init):
    bl, ll = geo.bl, geo.ll
    tq = TOKEN_BLOCK
    past = cache_k.shape[2]
    off = geo.nc // ll
    qblk = pl.BlockSpec((1, tq, DA_WIDTH), lambda b, j: (b + off, j, 0))
    seq = pl.BlockSpec((1, ll, DA_WIDTH), lambda b, j: (b + off, 0, 0))
    cache = pl.BlockSpec((1, 1, past, DA_WIDTH), lambda b, j: (b, layer, 0, 0))
    view = lambda a: a.reshape(geo.n // ll, ll, DA_WIDTH)
    out = pl.pallas_call(
        functools.partial(_attn_kernel, lam_init=lam_init, has_ctx=True),
        grid=(bl, ll // tq),
        in_specs=[qblk, seq, seq, cache, cache, _layer_spec(lam_p, layer), _layer_spec(subln, layer)],
        out_specs=pl.BlockSpec((1, tq, DA_WIDTH), lambda b, j: (b, j, 0)),
        out_shape=jax.ShapeDtypeStruct((bl, ll, DA_WIDTH), BF16),
        compiler_params=_cparams(2),
        name="attn_lat",
    )(view(q), view(kb), view(vb), cache_k, cache_v, lam_p, subln)
    return out.reshape(bl * ll, DA_WIDTH)


@functools.lru_cache(maxsize=None)
def _dft_tables(length):
    k = np.arange(length, dtype=np.int64)
    ks = (k[:, None] * k[None, :]) % (2 * length)
    ang = ks.astype(np.float64) * (np.pi / length)
    cmat = np.cos(ang)
    smat = -np.sin(ang)
    smat[0, :] = 1.0 - 2.0 * (k % 2)
    to_bf16 = lambda a: jnp.asarray(a.astype(np.float32)).astype(BF16)
    return to_bf16(cmat), to_bf16(smat), to_bf16(smat.T)


@functools.lru_cache(maxsize=None)
def _filter_features(length):
    t = np.linspace(0.0, 1.0, length, dtype=np.float32)
    w_ang = (2.0 * math.pi * np.arange(length, dtype=np.float32) / length).astype(np.float32)
    f = np.linspace(1e-4, HY_BANDS - 1, HY_BANDS, dtype=np.float32)
    arg = (w_ang[:, None] * f[None, :]).astype(np.float32).astype(np.float64)
    z = np.concatenate([t[:, None].astype(np.float64), np.cos(arg), -np.sin(arg)], axis=-1)
    z = np.pad(z, ((0, 0), (0, 64 - HY_EMB))).astype(np.float32)
    rev = np.concatenate([z[:1], z[:0:-1]], axis=0)
    deltas = np.linspace(math.log(HY_TARGET) / HY_FAST, math.log(HY_TARGET) / HY_SLOW,
                         HY_WIDTH, dtype=np.float32)
    return jnp.asarray(z), jnp.asarray(rev), jnp.asarray(np.abs(deltas)[None, :])


def _hyfilter_kernel(z_ref, zr_ref, dl_ref, w1_ref, b1_ref, w2_ref, b2_ref, w3_ref, fr_ref, o_ref):
    c = pl.program_id(0)
    fr = fr_ref[...]

    def mlp(z, w3):
        hid = jnp.sin(fr[0:1] * (_dot3(z, w1_ref[...]) + b1_ref[...]))
        hid = jnp.sin(fr[1:2] * (_dot3(hid, w2_ref[...]) + b2_ref[...]))
        return _dot3(hid, w3) * jnp.exp(-z[:, 0:1] * dl_ref[...])

    z, zr = z_ref[...], zr_ref[...]
    o_ref[0, :, 0:HY_WIDTH] = mlp(z, w3_ref[:, 0:HY_WIDTH]).astype(BF16)
    bwd = mlp(zr, w3_ref[:, HY_WIDTH:2 * HY_WIDTH])
    row = lax.broadcasted_iota(jnp.int32, bwd.shape, 0) + c * z.shape[0]
    o_ref[0, :, HY_WIDTH:2 * HY_WIDTH] = jnp.where(row == 0, 0.0, bwd).astype(BF16)


def _hyfilter(length, layer, w1p, b1, w2, b2, w3, fr):
    z, zr, dl = _filter_features(length)
    tb = TOKEN_BLOCK
    blk = pl.BlockSpec((tb, 64), lambda c: (c, 0))
    lay = lambda a: _layer_spec(a, layer)
    return pl.pallas_call(
        _hyfilter_kernel,
        grid=(length // tb,),
        in_specs=[blk, blk, pl.BlockSpec(dl.shape, lambda c: (0, 0)),
                  lay(w1p), lay(b1), lay(w2), lay(b2), lay(w3), lay(fr)],
        out_specs=pl.BlockSpec((1, tb, 2 * HY_WIDTH), lambda c: (0, c, 0)),
        out_shape=jax.ShapeDtypeStruct((1, length, 2 * HY_WIDTH), BF16),
        compiler_params=_cparams(1),
        name="hyena_filter",
    )(z, zr, dl, w1p, b1, w2, b2, w3, fr)


def _hyconv_kernel(main_ref, before_ref, after_ref, cw_ref, cb_ref, z_ref, zb_ref, x0_ref, pad_ref, *, geo):
    i = pl.program_id(0)
    tb = TOKEN_BLOCK
    _fill_padded(pad_ref, before_ref, main_ref, after_ref, geo.is_start(i), geo.is_end(i))
    w = HY_WIDTH
    parts = []
    for p in range(3):
        cols = slice(p * w, (p + 1) * w)
        uc = (pad_ref[HALO - 1:HALO - 1 + tb, cols] * cw_ref[0:1, cols]
              + main_ref[:, cols] * cw_ref[1:2, cols]
              + pad_ref[HALO + 1:HALO + 1 + tb, cols] * cw_ref[2:3, cols]
              + cb_ref[:, cols])
        parts.append(uc)
    x0, x1, v = parts
    z = v * x1
    z_ref[...] = z
    zb_ref[...] = z.astype(BF16)
    x0_ref[...] = x0


def _hyconv(geo, layer, u_hy, conv_w, conv_b):
    tb = TOKEN_BLOCK
    w3 = 3 * HY_WIDTH
    before, after = geo.halo_specs(w3)
    row = pl.BlockSpec((tb, HY_WIDTH), lambda i: (i, 0))
    return pl.pallas_call(
        functools.partial(_hyconv_kernel, geo=geo),
        grid=(geo.n // tb,),
        in_specs=[pl.BlockSpec((tb, w3), lambda i: (i, 0)), before, after,
                  _layer_spec(conv_w, layer), _layer_spec(conv_b, layer)],
        out_specs=[row, row, row],
        out_shape=[jax.ShapeDtypeStruct((geo.n, HY_WIDTH), F32),
                   jax.ShapeDtypeStruct((geo.n, HY_WIDTH), BF16),
                   jax.ShapeDtypeStruct((geo.n, HY_WIDTH), F32)],
        scratch_shapes=[pltpu.VMEM((tb + 2 * HALO, w3), F32)],
        compiler_params=_cparams(1),
        name="hyena_conv_gate",
    )(u_hy, u_hy, u_hy, conv_w, conv_b)


def _dft_fwd_kernel(c_ref, s_ref, x_ref, re_ref, im_ref):
    x = x_ref[0]
    re_ref[0] = _dot(c_ref[...], x)
    im_ref[0] = _dot(s_ref[...], x)


def _dft_fwd(length, x, nseq, seq_off):
    cmat, smat, _ = _dft_tables(length)
    n = x.shape[-1]
    tm = min(length, 512)
    tn = 512
    a_spec = pl.BlockSpec((tm, length), lambda b, j, m: (m, 0))
    o_spec = pl.BlockSpec((1, tm, tn), lambda b, j, m: (b, m, j))
    return pl.pallas_call(
        _dft_fwd_kernel,
        grid=(nseq, n // tn, length // tm),
        in_specs=[a_spec, a_spec, pl.BlockSpec((1, length, tn), lambda b, j, m: (b + seq_off, 0, j))],
        out_specs=[o_spec, o_spec],
        out_shape=[jax.ShapeDtypeStruct((nseq, length, n), F32)] * 2,
        compiler_params=_cparams(3),
        name="hyena_dft",
    )(cmat, smat, x)


def _hyprod_kernel(zr_ref, zi_ref, hfr_ref, hgr_ref, hfi_ref, hgi_ref, yr_ref, yi_ref, *, length):
    c = pl.program_id(1)
    shape = zr_ref.shape[1:]
    k = lax.broadcasted_iota(jnp.int32, shape, 0) + c * shape[0]
    sgn = (1 - 2 * (k % 2)).astype(F32)
    hr = hfr_ref[0] + sgn * hgr_ref[0]
    hi = hfi_ref[0] + sgn * hgi_ref[0]
    zr, zi = zr_ref[0], zi_ref[0]
    inv = 1.0 / length
    yr = (zr * hr - zi * hi) * inv
    yi = (zr * hi + zi * hr) * inv
    first = k == 0
    yr_ref[0] = jnp.where(first, zr * hr * (0.5 * inv), yr).astype(BF16)
    yi_ref[0] = jnp.where(first, zi * hi * (0.5 * inv), yi).astype(BF16)


def _hyprod(length, zre, zim, hre, him):
    nb = zre.shape[0]
    tb = TOKEN_BLOCK
    zs = pl.BlockSpec((1, tb, HY_WIDTH), lambda b, c: (b, c, 0))
    hf = pl.BlockSpec((1, tb, HY_WIDTH), lambda b, c: (0, c, 0))
    hg = pl.BlockSpec((1, tb, HY_WIDTH), lambda b, c: (0, c, 1))
    return pl.pallas_call(
        functools.partial(_hyprod_kernel, length=length),
        grid=(nb, length // tb),
        in_specs=[zs, zs, hf, hg, hf, hg],
        out_specs=[zs, zs],
        out_shape=[jax.ShapeDtypeStruct((nb, length, HY_WIDTH), BF16)] * 2,
        compiler_params=_cparams(2),
        name="hyena_spectral_product",
    )(zre, zim, hre, hre, him, him)


def _hyinv_kernel(c_ref, st_ref, yr_ref, yi_ref, z_ref, x0_ref, skip_ref, o_ref):
    y = _dot(c_ref[...], yr_ref[0]) + _dot(st_ref[...], yi_ref[0])
    o_ref[0] = ((y + z_ref[0] * skip_ref[...]) * x0_ref[0]).astype(BF16)


def _hyinv(length, layer, yr, yi, z, x0, skip, seq_off):
    cmat, _, smat_t = _dft_tables(length)
    nseq = yr.shape[0]
    tm = min(length, 512)
    a_spec = pl.BlockSpec((tm, length), lambda b, m: (m, 0))
    y_spec = pl.BlockSpec((1, length, HY_WIDTH), lambda b, m: (b, 0, 0))
    t_spec = pl.BlockSpec((1, tm, HY_WIDTH), lambda b, m: (b + seq_off, m, 0))
    return pl.pallas_call(
        _hyinv_kernel,
        grid=(nseq, length // tm),
        in_specs=[a_spec, a_spec, y_spec, y_spec, t_spec, t_spec, _layer_spec(skip, layer)],
        out_specs=pl.BlockSpec((1, tm, HY_WIDTH), lambda b, m: (b, m, 0)),
        out_shape=jax.ShapeDtypeStruct((nseq, length, HY_WIDTH), BF16),
        compiler_params=_cparams(2),
        name="hyena_idft",
    )(cmat, smat_t, yr, yi, z, x0, skip)


def _hyena_long_conv(length, layer, nseq, seq_off, zb, z, x0, filt, skip):
    view = lambda a: a.reshape(a.shape[0] // length, length, HY_WIDTH)
    zre, zim = _dft_fwd(length, view(zb), nseq, seq_off)
    hre, him = _dft_fwd(length, filt, 1, 0)
    yr, yi = _hyprod(length, zre, zim, hre, him)
    out = _hyinv(length, layer, yr, yi, view(z), view(x0), skip, seq_off)
    return out.reshape(nseq * length, HY_WIDTH)


def _merge_kernel(x_ref, mod_ref, g1_ref, a_ref, bc_ref, bl_ref, cc_ref, cl_ref, wg_ref, bg_ref, wbr_ref,
                  wo_ref, g2_ref, wr_ref, br_ref, xo_ref, h2_ref, ti_ref, tp_ref, *, nbc):
    d = D_MODEL
    is_ctx = pl.program_id(0) < nbc
    mod = mod_ref[0]
    shift1, scale1, gate1 = mod[:, 0:d], mod[:, d:2 * d], mod[:, 2 * d:3 * d]
    shift2, scale2 = mod[:, 3 * d:4 * d], mod[:, 4 * d:5 * d]
    x = x_ref[...]
    h = (_rms(x) * g1_ref[...] * (1.0 + scale1) + shift1).astype(BF16)
    branches = (a_ref[...],
                jnp.where(is_ctx, bc_ref[...], bl_ref[...]),
                jnp.where(is_ctx, cc_ref[...], cl_ref[...]))
    merged = None
    for n, br_n in enumerate(branches):
        cols = slice(n * d, (n + 1) * d)
        gate = _sigmoid(_dot(h, wg_ref[:, cols]) + bg_ref[:, cols])
        term = gate * _dot(br_n, wbr_ref[n])
        merged = term if merged is None else merged + term
    x = x + gate1 * _dot(merged.astype(BF16), wo_ref[...])
    xo_ref[...] = x
    h2 = _rms(x) * g2_ref[...] * (1.0 + scale2) + shift2
    h2_ref[...] = h2.astype(BF16)

    logits = _dot3(h2, wr_ref[...]) + br_ref[...]
    ne = logits.shape[-1]
    lane = lax.broadcasted_iota(jnp.int32, logits.shape, 1).astype(F32)
    vals = logits
    top_v, top_i = [], []
    for _ in range(TOP_K):
        m = jnp.max(vals, axis=-1, keepdims=True)
        idx = jnp.min(jnp.where(vals == m, lane, float(ne)), axis=-1, keepdims=True)
        top_v.append(m)
        top_i.append(idx)
        vals = jnp.where(lane == idx, -jnp.inf, vals)
    es = [jnp.exp(v - top_v[0]) for v in top_v]
    den = es[0] + es[1] + es[2] + es[3]
    for kk in range(TOP_K):
        ti_ref[:, kk:kk + 1] = top_i[kk].astype(jnp.int32)
        tp_ref[:, kk:kk + 1] = es[kk] / den


def _merge(geo, layer, x, mods, g1, a_out, b_ctx, b_lat, c_ctx, c_lat, wg_b, bg, wbr_b, wo_b, g2, wr, br):
    d = D_MODEL
    tb = WIDE_BLOCK
    row = lambda w: pl.BlockSpec((tb, w), lambda i: (i, 0))
    ctx, lat = geo.split_specs(tb, DA_WIDTH)
    lay = lambda a: _layer_spec(a, layer, single_buffer=True)
    return pl.pallas_call(
        functools.partial(_merge_kernel, nbc=geo.nc // tb),
        grid=(geo.n // tb,),
        in_specs=[row(d), geo.mod_spec(layer, tb), lay(g1), row(POOL_WIDTH), ctx, lat, ctx, lat,
                  lay(wg_b), lay(bg), lay(wbr_b), lay(wo_b), lay(g2), lay(wr), lay(br)],
        out_specs=[row(d), row(d), row(TOP_K), row(TOP_K)],
        out_shape=[jax.ShapeDtypeStruct((geo.n, d), F32),
                   jax.ShapeDtypeStruct((geo.n, d), BF16),
                   jax.ShapeDtypeStruct((geo.n, TOP_K), jnp.int32),
                   jax.ShapeDtypeStruct((geo.n, TOP_K), F32)],
        compiler_params=_cparams(1),
        name="merge_route",
    )(x, mods, g1, a_out, b_ctx, b_lat, c_ctx, c_lat, wg_b, bg, wbr_b, wo_b, g2, wr, br)


def _ffn_kernel(sw_ref, st_ref, sn_ref, nv_ref, xs_hbm, wgu_ref, bgu_ref, wd_ref, bd_ref, ys_hbm,
                xbuf, ybuf, semx, semy):
    i = pl.program_id(0)
    nv = nv_ref[0]
    slot = i % 2
    t = MOE_TILE
    half = D_FF // 2

    def rows_of(step, k):
        return pl.ds(pl.multiple_of(st_ref[step] * t, t), k * t)

    def x_copy(step, buf, k):
        return pltpu.make_async_copy(xs_hbm.at[rows_of(step, k)], xbuf.at[buf, pl.ds(0, k * t)], semx.at[buf])

    def y_copy(step, buf, k):
        return pltpu.make_async_copy(ybuf.at[buf, pl.ds(0, k * t)], ys_hbm.at[rows_of(step, k)], semy.at[buf])

    def for_size(step, fn):
        n = sn_ref[step]
        for k in range(1, MOE_MAX_SUB + 1):
            @pl.when(n == k)
            def _(k=k):
                fn(k)

    def compute(k):
        m = k * t
        x = xbuf[slot, 0:m, :]
        y = None
        for c in range(2):
            gcols = slice(c * half, (c + 1) * half)
            ucols = slice(D_FF + c * half, D_FF + (c + 1) * half)
            gate = _dot(x, wgu_ref[0, :, gcols].astype(BF16)) + bgu_ref[0, :, gcols]
            up = _dot(x, wgu_ref[0, :, ucols].astype(BF16)) + bgu_ref[0, :, ucols]
            gate = jnp.minimum(gate, SWIGLU_LIMIT)
            up = jnp.clip(up, -SWIGLU_LIMIT, SWIGLU_LIMIT)
            act = ((up + 1.0) * gate * _sigmoid(SWIGLU_ALPHA * gate)).astype(BF16)
            part = _dot(act, wd_ref[0, gcols, :].astype(BF16))
            y = part if y is None else y + part
        ybuf[slot, 0:m, :] = (y + bd_ref[0]).astype(BF16)
        y_copy(i, slot, k).start()

    @pl.when(i < nv)
    def _():
        @pl.when(i == 0)
        def _():
            for_size(0, lambda k: x_copy(0, 0, k).start())

        for_size(i, lambda k: x_copy(i, slot, k).wait())

        @pl.when(i + 1 < nv)
        def _():
            for_size(i + 1, lambda k: x_copy(i + 1, 1 - slot, k).start())

        @pl.when(i >= 2)
        def _():
            for_size(i - 2, lambda k: y_copy(i - 2, slot, k).wait())

        for_size(i, compute)

        @pl.when(i == nv - 1)
        def _():
            for_size(i, lambda k: y_copy(i, slot, k).wait())

            @pl.when(i >= 1)
            def _():
                for_size(i - 1, lambda k: y_copy(i - 1, 1 - slot, k).wait())


def _ffn(xs, step_weight, step_tile, step_sub, n_valid, w_gu, b_gu, w_down, b_down):
    p, d = xs.shape
    t = MOE_TILE
    hbm = pl.BlockSpec(memory_space=pl.ANY)
    wmap = lambda i, sw, st, sn, nv: (sw[i], 0, 0)
    grid_spec = pltpu.PrefetchScalarGridSpec(
        num_scalar_prefetch=4,
        grid=(step_weight.shape[0],),
        in_specs=[hbm,
                  pl.BlockSpec((1, d, 2 * D_FF), wmap), pl.BlockSpec((1, 1, 2 * D_FF), wmap),
                  pl.BlockSpec((1, D_FF, d), wmap), pl.BlockSpec((1, 1, d), wmap)],
        out_specs=hbm,
        scratch_shapes=[pltpu.VMEM((2, MOE_MAX_SUB * t, d), BF16), pltpu.VMEM((2, MOE_MAX_SUB * t, d), BF16),
                        pltpu.SemaphoreType.DMA((2,)), pltpu.SemaphoreType.DMA((2,))])
    return pl.pallas_call(
        _ffn_kernel,
        grid_spec=grid_spec,
        out_shape=jax.ShapeDtypeStruct((p, d), BF16),
        compiler_params=_cparams(1),
        name="moe_experts",
    )(step_weight, step_tile, step_sub, n_valid, xs, w_gu, b_gu, w_down, b_down)


def _route(top_i, n_experts):
    n = top_i.shape[0]
    t = MOE_TILE
    n_pairs = n * TOP_K
    n_tiles = n_pairs // t + n_experts
    experts = jnp.arange(n_experts, dtype=jnp.int32)[None, :]
    flat_e = top_i.reshape(-1)
    onehot = flat_e[:, None] == experts
    counts = jnp.sum(onehot, axis=0, dtype=jnp.int32)
    order = jnp.argsort(flat_e, stable=True).astype(jnp.int32)
    _, pair_pos = lax.sort_key_val(order, jnp.arange(n_pairs, dtype=jnp.int32))
    raw_start = jnp.cumsum(counts) - counts
    tiles_e = (counts + t - 1) // t
    tile_end = jnp.cumsum(tiles_e)
    tile_start = tile_end - tiles_e
    pad_start = tile_start * t
    pair_slot = jnp.sum(jnp.where(onehot, (pad_start - raw_start)[None, :], 0), axis=1) + pair_pos

    def owner(ends, ids):
        return jnp.sum(ends[None, :] <= ids[:, None], axis=1, dtype=jnp.int32)[:, None] == experts

    pick = lambda oh, table: jnp.sum(jnp.where(oh, table[None, :], 0), axis=1)

    tile_ids = jnp.minimum(jnp.arange(n_tiles, dtype=jnp.int32), tile_end[-1] - 1)
    tile_oh = owner(tile_end, tile_ids)
    rank = jnp.arange(n_tiles * t, dtype=jnp.int32).reshape(n_tiles, t) - pick(tile_oh, pad_start)[:, None]
    src = jnp.where(rank < pick(tile_oh, counts)[:, None], pick(tile_oh, raw_start)[:, None] + rank, 0)
    slot_token = jnp.take(order, src.reshape(-1), mode="clip") // TOP_K

    steps_e = (tiles_e + MOE_MAX_SUB - 1) // MOE_MAX_SUB
    step_end = jnp.cumsum(steps_e)
    n_steps = n_pairs // (t * MOE_MAX_SUB) + n_experts
    n_valid = step_end[-1]
    step_ids = jnp.minimum(jnp.arange(n_steps, dtype=jnp.int32), n_valid - 1)
    step_oh = owner(step_end, step_ids)
    j = step_ids - pick(step_oh, step_end - steps_e)
    step_expert = jnp.argmax(step_oh, axis=1).astype(jnp.int32)
    step_tile = pick(step_oh, tile_start) + MOE_MAX_SUB * j
    step_sub = jnp.clip(pick(step_oh, tiles_e) - MOE_MAX_SUB * j, 1, MOE_MAX_SUB)
    return (slot_token, pair_slot.reshape(n, TOP_K), step_expert, step_tile.astype(jnp.int32),
            step_sub.astype(jnp.int32), n_valid.reshape(1).astype(jnp.int32))


def _moe_sum(x_ref, mod_ref, yg_ref, p_ref):
    d = D_MODEL
    gate2 = mod_ref[0][:, 5 * d:6 * d]
    p = p_ref[...]
    moe = p[:, 0:1] * yg_ref[0].astype(F32)
    for kk in range(1, TOP_K):
        moe = moe + p[:, kk:kk + 1] * yg_ref[kk].astype(F32)
    return x_ref[...] + gate2 * moe


def _combine_kernel(x_ref, mod_ref, yg_ref, p_ref, xo_ref):
    xo_ref[...] = _moe_sum(x_ref, mod_ref, yg_ref, p_ref)


def _combine_final_kernel(x_ref, mod_ref, yg_ref, p_ref, gf_ref, yc_ref, yl_ref, *, nbc):
    i = pl.program_id(0)
    y = _rms(_moe_sum(x_ref, mod_ref, yg_ref, p_ref)) * gf_ref[...]

    @pl.when(i < nbc)
    def _():
        yc_ref[...] = y

    @pl.when(i >= nbc)
    def _():
        yl_ref[...] = y


def _combine(geo, layer, x, mods, yg, top_p, final_g=None):
    d = D_MODEL
    tb = TOKEN_BLOCK
    row = pl.BlockSpec((tb, d), lambda i: (i, 0))
    in_specs = [row, geo.mod_spec(layer, tb),
                pl.BlockSpec((TOP_K, tb, d), lambda i: (0, i, 0)),
                pl.BlockSpec((tb, TOP_K), lambda i: (i, 0))]
    if final_g is None:
        return pl.pallas_call(
            _combine_kernel, grid=(geo.n // tb,), in_specs=in_specs, out_specs=row,
            out_shape=jax.ShapeDtypeStruct((geo.n, d), F32),
            compiler_params=_cparams(1), name="moe_combine",
        )(x, mods, yg, top_p)
    ctx, lat = geo.split_specs(tb, d)
    return pl.pallas_call(
        functools.partial(_combine_final_kernel, nbc=geo.nc // tb),
        grid=(geo.n // tb,),
        in_specs=in_specs + [pl.BlockSpec(final_g.shape, lambda i: (0, 0))],
        out_specs=[ctx, lat],
        out_shape=[jax.ShapeDtypeStruct((geo.nc, d), F32), jax.ShapeDtypeStruct((geo.nl, d), F32)],
        compiler_params=_cparams(1), name="moe_combine_final",
    )(x, mods, yg, top_p, final_g)


@functools.lru_cache(maxsize=None)
def _rope_tables(length):
    rows = length // GRID_W
    row = np.repeat(np.arange(rows), GRID_W).astype(np.float32)
    col = np.tile(np.arange(GRID_W), rows).astype(np.float32)
    ax = DA_HEAD_DIM // 2
    inv = (ROPE_BASE ** (-(np.arange(ax // 2, dtype=np.float32) * 2.0 / ax))).astype(np.float32)
    ang_r = (row[:, None] * inv).astype(np.float32)
    ang_c = (col[:, None] * inv).astype(np.float32)
    ang = np.concatenate([ang_r, ang_r, ang_c, ang_c], axis=-1).astype(np.float64)
    sign = np.where((np.arange(DA_HEAD_DIM) % 32) < 16, -1.0, 1.0)
    reps = DA_WIDTH // DA_HEAD_DIM
    cos = np.tile(np.cos(ang), (1, reps)).astype(np.float32)
    sin_signed = np.tile(np.sin(ang) * sign[None, :], (1, reps)).astype(np.float32)
    return jnp.asarray(cos), jnp.asarray(sin_signed)


def kernel(x_prompt, x_sample, cache_k, cache_v, c, c_ctx, w_ada, b_ada, norm1, norm2, w_in, w_pool, pool_scale, da_lambda, da_subln, hy_conv_w, hy_conv_b, hy_f_w1, hy_f_b1, hy_f_w2, hy_f_b2, hy_f_w3, hy_sin_freq, hy_skip, w_branch, w_gate, b_gate, w_o, w_router, b_router, w_gu, b_gu, w_down, b_down, final_norm):
    bc, lc, d = x_prompt.shape
    bl, ll, _ = x_sample.shape
    depth = w_in.shape[0]
    n_experts = w_router.shape[-1]
    past = cache_k.shape[2]
    geo = _Geom(bc, lc, bl, ll)
    assert 1 + bl <= MOD_ROWS

    x = jnp.concatenate([x_prompt.reshape(bc * lc, d), x_sample.reshape(bl * ll, d)], axis=0)
    cond = jnp.concatenate([c_ctx[None], c, jnp.zeros((MOD_ROWS - 1 - bl, d), F32)], axis=0)
    mods = _ada(cond, w_ada, b_ada).reshape(depth * MOD_ROWS, 1, N_MOD * d)
    cos_t, sin_t = _rope_tables(ll)
    ck = cache_k.reshape(bl, depth, past, DA_WIDTH)
    cv = cache_v.reshape(bl, depth, past, DA_WIDTH)

    row3 = lambda a: a.reshape(depth, 1, a.shape[-1])
    g1, g2 = row3(norm1), row3(norm2)
    w_in_b = w_in.astype(BF16)
    w_pool_b = w_pool.astype(BF16)
    w_gate_b = w_gate.astype(BF16)
    w_branch_b = w_branch.astype(BF16)
    w_o_b = w_o.astype(BF16)
    w1p = jnp.pad(hy_f_w1, ((0, 0), (0, 64 - HY_EMB), (0, 0)))
    fargs = (w1p, row3(hy_f_b1), hy_f_w2, row3(hy_f_b2), hy_f_w3, hy_sin_freq)
    w_gu_s = w_gu.reshape(depth * n_experts, d, 2 * D_FF)
    b_gu_s = b_gu.reshape(depth * n_experts, 1, 2 * D_FF)
    w_down_s = w_down.reshape(depth * n_experts, D_FF, d)
    b_down_s = b_down.reshape(depth * n_experts, 1, d)

    new_k = jnp.zeros((bc, depth, lc * DA_HEADS, DA_VDIM), F32)
    new_v = jnp.zeros((bc, depth, lc * DA_HEADS, DA_VDIM), F32)
    for l in range(depth):
        u_pool, q, kb, vb, new_k, new_v, u_hy = _inproj(geo, l, x, mods, g1, w_in_b, cos_t, sin_t, new_k, new_v)
        a_out = _pool(geo, l, u_pool, w_pool_b, row3(pool_scale))

        lam_init = 0.8 - 0.6 * math.exp(-0.3 * l)
        subln = row3(da_subln)
        b_ctx = _attn_ctx(geo, l, q, kb, vb, da_lambda, subln, lam_init)
        b_lat = _attn_lat(geo, l, q, kb, vb, ck, cv, da_lambda, subln, lam_init)

        z, zb, x0 = _hyconv(geo, l, u_hy, hy_conv_w, row3(hy_conv_b))
        skip = row3(hy_skip)
        c_ctx_out = _hyena_long_conv(lc, l, bc, 0, zb, z, x0, _hyfilter(lc, l, *fargs), skip)
        c_lat_out = _hyena_long_conv(ll, l, bl, geo.nc // ll, zb, z, x0, _hyfilter(ll, l, *fargs), skip)

        x, h2, top_i, top_p = _merge(geo, l, x, mods, g1, a_out, b_ctx, b_lat, c_ctx_out, c_lat_out,
                                     w_gate_b, row3(b_gate), w_branch_b, w_o_b, g2, w_router, row3(b_router))

        slot_token, pair_slot, step_expert, step_tile, step_sub, n_valid = _route(top_i, n_experts)
        xs = jnp.take(h2, slot_token, axis=0, mode="clip")
        ys = _ffn(xs, step_expert + l * n_experts, step_tile, step_sub, n_valid,
                  w_gu_s, b_gu_s, w_down_s, b_down_s)
        yg = jnp.take(ys, pair_slot.T, axis=0, mode="clip")
        if l + 1 < depth:
            x = _combine(geo, l, x, mods, yg, top_p)
        else:
            y_ctx, y_lat = _combine(geo, l, x, mods, yg, top_p, final_norm[None])

    return (y_ctx.reshape(bc, lc, d), y_lat.reshape(bl, ll, d),
            new_k.reshape(bc, depth, lc, DA_HEADS, 2 * DA_HEAD_DIM),
            new_v.reshape(bc, depth, lc, DA_HEADS, DA_VDIM))
```

```python
import functools
import math

import numpy as np
import jax
import jax.numpy as jnp
from jax import lax
from jax.experimental import pallas as pl
from jax.experimental.pallas import tpu as pltpu

F32 = jnp.float32
BF16 = jnp.bfloat16

D_MODEL = 1024
GRID_W = 64
NORM_EPS = 1e-6
POOL_WIDTH = 512
POOL_WINDOWS = (2, 4, 8, 16)
POOL_GC = POOL_WIDTH // len(POOL_WINDOWS)
DA_HEADS = 4
DA_HEAD_DIM = 64
DA_VDIM = 2 * DA_HEAD_DIM
DA_WIDTH = DA_HEADS * DA_VDIM
ROPE_BASE = 10000.0
HY_WIDTH = 512
HY_EMB = 33
HY_BANDS = (HY_EMB - 1) // 2
HY_HIDDEN = 64
HY_FAST = 0.3
HY_SLOW = 1.5
HY_TARGET = 1e-2
N_BRANCH = 3
D_IN = POOL_WIDTH + 3 * DA_WIDTH + 3 * HY_WIDTH
TOP_K = 4
D_FF = 1024
SWIGLU_ALPHA = 1.702
SWIGLU_LIMIT = 7.0
N_MOD = 6
MOD_ROWS = 8

TOKEN_BLOCK = 256
WIDE_BLOCK = 512
HALO = 8
MOE_TILE = 256
MOE_MAX_SUB = 4
DISPATCH_BLOCK = 512
RUN_ALIGN = 16
V7X_VMEM_LIMIT = 56 * 1024 * 1024


def _cparams(n_axes):
    return pltpu.CompilerParams(
        dimension_semantics=("arbitrary",) * n_axes,
        vmem_limit_bytes=V7X_VMEM_LIMIT)


def _dot(a, b):
    return jnp.dot(a, b, preferred_element_type=F32)


def _dot_nt(a, b):
    return lax.dot_general(a, b, (((1,), (1,)), ((), ())), preferred_element_type=F32)


def _split_bf16(a):
    hi = a.astype(BF16)
    lo = (a - hi.astype(F32)).astype(BF16)
    return hi, lo


def _dot3(a, b):
    ah, al = _split_bf16(a)
    bh, bl = _split_bf16(b)
    return _dot(ah, bh) + _dot(al, bh) + _dot(ah, bl)


def _sigmoid(x):
    return 1.0 / (1.0 + jnp.exp(-x))


def _rms(x):
    return x * lax.rsqrt(jnp.mean(x * x, axis=-1, keepdims=True) + NORM_EPS)


class _Geom:
    def __init__(self, n_ctx_seq, ctx_len, n_lat_seq, lat_len):
        assert ctx_len == TOKEN_BLOCK, "one context sequence per token block"
        assert lat_len % WIDE_BLOCK == 0 and (n_ctx_seq * ctx_len) % lat_len == 0
        self.bc, self.lc, self.bl, self.ll = n_ctx_seq, ctx_len, n_lat_seq, lat_len
        self.nc = n_ctx_seq * ctx_len
        self.nl = n_lat_seq * lat_len
        self.n = self.nc + self.nl

    def group(self, i, tb):
        nbc = self.nc // tb
        return jnp.where(i < nbc, 0, 1 + (i - nbc) // (self.ll // tb))

    def pos_block(self, i, tb):
        nbc = self.nc // tb
        return jnp.where(i < nbc, 0, (i - nbc) % (self.ll // tb))

    def is_start(self, i):
        nbc, bpl = self.nc // TOKEN_BLOCK, self.ll // TOKEN_BLOCK
        return jnp.logical_or(i < nbc, (i - nbc) % bpl == 0)

    def is_end(self, i):
        nbc, bpl = self.nc // TOKEN_BLOCK, self.ll // TOKEN_BLOCK
        return jnp.logical_or(i < nbc, (i - nbc) % bpl == bpl - 1)

    def mod_spec(self, layer, tb):
        return pl.BlockSpec((1, 1, N_MOD * D_MODEL),
                            lambda i: (layer * MOD_ROWS + self.group(i, tb), 0, 0))

    def halo_specs(self, width):
        per = TOKEN_BLOCK // HALO
        last = self.n // HALO - 1
        before = pl.BlockSpec((HALO, width), lambda i: (jnp.maximum(i * per - 1, 0), 0))
        after = pl.BlockSpec((HALO, width), lambda i: (jnp.minimum((i + 1) * per, last), 0))
        return before, after

    def split_specs(self, tb, width):
        nbc = self.nc // tb
        last_lat = self.nl // tb - 1
        ctx = pl.BlockSpec((tb, width), lambda i: (jnp.minimum(i, nbc - 1), 0))
        lat = pl.BlockSpec((tb, width), lambda i: (jnp.clip(i - nbc, 0, last_lat), 0))
        return ctx, lat


def _layer_spec(a, layer, single_buffer=False):
    kw = dict(pipeline_mode=pl.Buffered(1)) if single_buffer else {}
    return pl.BlockSpec((None,) + a.shape[1:], lambda *_: (layer,) + (0,) * (a.ndim - 1), **kw)


def _ada_kernel(c_ref, w_ref, b_ref, o_ref):
    c = c_ref[...]
    s = c * _sigmoid(c)
    o_ref[0] = _dot(s.astype(BF16), w_ref[0].astype(BF16)) + b_ref[0]


def _ada(cond, w_ada, b_ada):
    depth, d, n6 = w_ada.shape
    rows = cond.shape[0]
    tn = 1024
    return pl.pallas_call(
        _ada_kernel,
        grid=(depth, n6 // tn),
        in_specs=[pl.BlockSpec((rows, d), lambda l, j: (0, 0)),
                  pl.BlockSpec((1, d, tn), lambda l, j: (l, 0, j)),
                  pl.BlockSpec((1, 1, tn), lambda l, j: (l, 0, j))],
        out_specs=pl.BlockSpec((1, rows, tn), lambda l, j: (l, 0, j)),
        out_shape=jax.ShapeDtypeStruct((depth, rows, n6), F32),
        compiler_params=_cparams(2),
        name="ada_mod",
    )(cond, w_ada, b_ada.reshape(depth, 1, n6))


def _rope(x, cos, sin_signed, first_half):
    d = x.shape[-1]
    partner = jnp.where(first_half, pltpu.roll(x, d - 16, 1), pltpu.roll(x, 16, 1))
    return x * cos + partner * sin_signed


def _inproj_kernel(x_ref, mod_ref, g_ref, w_ref, cos_ref, sin_ref, kacc_ref, vacc_ref,
                   up_ref, q_ref, kb_ref, vb_ref, kf_ref, vf_ref, uh_ref, *, nbc, lc):
    del kacc_ref, vacc_ref
    i = pl.program_id(0)
    d = D_MODEL
    mod = mod_ref[0]
    shift, scale = mod[:, 0:d], mod[:, d:2 * d]
    h = (_rms(x_ref[...]) * g_ref[...] * (1.0 + scale) + shift).astype(BF16)

    c1 = POOL_WIDTH
    c2 = c1 + DA_WIDTH
    c3 = c2 + DA_WIDTH
    c4 = c3 + DA_WIDTH
    up_ref[...] = _dot(h, w_ref[:, 0:c1])
    uh_ref[...] = _dot(h, w_ref[:, c4:D_IN])
    q = _dot(h, w_ref[:, c1:c2]) * (DA_HEAD_DIM ** -0.5)
    k = _dot(h, w_ref[:, c2:c3])
    v = _dot(h, w_ref[:, c3:c4])
    vb_ref[...] = v.astype(BF16)

    @pl.when(i < nbc)
    def _():
        q_ref[...] = q.astype(BF16)
        kb_ref[...] = k.astype(BF16)
        for s in range(k.shape[0] // lc):
            for hh in range(DA_HEADS):
                rows = pl.ds(hh, lc, stride=DA_HEADS)
                cols = slice(hh * DA_VDIM, (hh + 1) * DA_VDIM)
                kf_ref[s, 0, rows, :] = k[s * lc:(s + 1) * lc, cols]
                vf_ref[s, 0, rows, :] = v[s * lc:(s + 1) * lc, cols]

    @pl.when(i >= nbc)
    def _():
        cos, sin_signed = cos_ref[...], sin_ref[...]
        lane = lax.broadcasted_iota(jnp.int32, q.shape, 1)
        first_half = (lane % 32) < 16
        q_ref[...] = _rope(q, cos, sin_signed, first_half).astype(BF16)
        kb_ref[...] = _rope(k, cos, sin_signed, first_half).astype(BF16)


def _inproj(geo, layer, x, mods, g1, w_in_b, cos_t, sin_t, kacc, vacc):
    d = D_MODEL
    tb = WIDE_BLOCK
    nbc = geo.nc // tb
    spb = tb // geo.lc
    row = lambda w: pl.BlockSpec((tb, w), lambda i: (i, 0))
    cache = pl.BlockSpec((spb, 1, geo.lc * DA_HEADS, DA_VDIM), lambda i: (jnp.minimum(i, nbc - 1), layer, 0, 0))
    tab = pl.BlockSpec((tb, DA_WIDTH), lambda i: (geo.pos_block(i, tb), 0))
    hbm = pl.BlockSpec(memory_space=pl.ANY)
    return pl.pallas_call(
        functools.partial(_inproj_kernel, nbc=nbc, lc=geo.lc),
        grid=(geo.n // tb,),
        in_specs=[row(d), geo.mod_spec(layer, tb), _layer_spec(g1, layer),
                  _layer_spec(w_in_b, layer, single_buffer=True), tab, tab, hbm, hbm],
        out_specs=[row(POOL_WIDTH), row(DA_WIDTH), row(DA_WIDTH), row(DA_WIDTH),
                   cache, cache, row(3 * HY_WIDTH)],
        out_shape=[jax.ShapeDtypeStruct((geo.n, POOL_WIDTH), F32),
                   jax.ShapeDtypeStruct((geo.n, DA_WIDTH), BF16),
                   jax.ShapeDtypeStruct((geo.n, DA_WIDTH), BF16),
                   jax.ShapeDtypeStruct((geo.n, DA_WIDTH), BF16),
                   jax.ShapeDtypeStruct(kacc.shape, F32),
                   jax.ShapeDtypeStruct(vacc.shape, F32),
                   jax.ShapeDtypeStruct((geo.n, 3 * HY_WIDTH), F32)],
        input_output_aliases={6: 4, 7: 5},
        compiler_params=_cparams(1),
        name="in_proj",
    )(x, mods, g1, w_in_b, cos_t, sin_t, kacc, vacc)


def _fill_padded(pad_ref, before_ref, main_ref, after_ref, start, end):
    tb = TOKEN_BLOCK
    zero = jnp.zeros(before_ref.shape, F32)
    pad_ref[0:HALO, :] = jnp.where(start, zero, before_ref[...])
    pad_ref[HALO:HALO + tb, :] = main_ref[...]
    pad_ref[HALO + tb:2 * HALO + tb, :] = jnp.where(end, zero, after_ref[...])


def _pool_kernel(main_ref, before_ref, after_ref, w_ref, s_ref, o_ref, pad_ref, *, geo):
    i = pl.program_id(0)
    tb = TOKEN_BLOCK
    start, end = geo.is_start(i), geo.is_end(i)
    _fill_padded(pad_ref, before_ref, main_ref, after_ref, start, end)
    r = lax.broadcasted_iota(jnp.int32, (tb, 1), 0)
    for g, w in enumerate(POOL_WINDOWS):
        cols = slice(g * POOL_GC, (g + 1) * POOL_GC)
        acc = pad_ref[HALO - w // 2:HALO - w // 2 + tb, cols]
        for j in range(-w // 2 + 1, w // 2):
            acc = acc + pad_ref[HALO + j:HALO + j + tb, cols]
        lo = jnp.where(start, jnp.maximum(r - w // 2, 0), r - w // 2)
        hi = jnp.where(end, jnp.minimum(r + w // 2, tb), r + w // 2)
        mean = acc / (hi - lo).astype(F32)
        dlt = mean - main_ref[:, cols]
        y = _dot(dlt.astype(BF16), w_ref[g])
        o_ref[:, cols] = (y * s_ref[:, cols]).astype(BF16)


def _pool(geo, layer, u_pool, w_pool_b, pool_scale):
    tb = TOKEN_BLOCK
    before, after = geo.halo_specs(POOL_WIDTH)
    return pl.pallas_call(
        functools.partial(_pool_kernel, geo=geo),
        grid=(geo.n // tb,),
        in_specs=[pl.BlockSpec((tb, POOL_WIDTH), lambda i: (i, 0)), before, after,
                  _layer_spec(w_pool_b, layer), _layer_spec(pool_scale, layer)],
        out_specs=pl.BlockSpec((tb, POOL_WIDTH), lambda i: (i, 0)),
        out_shape=jax.ShapeDtypeStruct((geo.n, POOL_WIDTH), BF16),
        scratch_shapes=[pltpu.VMEM((tb + 2 * HALO, POOL_WIDTH), F32)],
        compiler_params=_cparams(1),
        name="pool_branch",
    )(u_pool, u_pool, u_pool, w_pool_b, pool_scale)


def _attn_kernel(*refs, lam_init, has_ctx):
    if has_ctx:
        q_ref, k_ref, v_ref, kc_ref, vc_ref, lam_ref, g_ref, o_ref = refs
    else:
        q_ref, k_ref, v_ref, lam_ref, g_ref, o_ref = refs
    lp = lam_ref[...]
    lam = (jnp.exp(jnp.sum(lp[0:1] * lp[1:2], axis=-1, keepdims=True))
           - jnp.exp(jnp.sum(lp[2:3] * lp[3:4], axis=-1, keepdims=True)) + lam_init)
    for h in range(DA_HEADS):
        vcols = slice(h * DA_VDIM, (h + 1) * DA_VDIM)
        v = v_ref[0, :, vcols]
        if has_ctx:
            vc = vc_ref[0, 0, :, vcols].astype(BF16)
        outs = []
        for sub in range(2):
            c0 = h * DA_VDIM + sub * DA_HEAD_DIM
            cols = slice(c0, c0 + DA_HEAD_DIM)
            qs = q_ref[0, :, cols]
            s1 = _dot_nt(qs, k_ref[0, :, cols])
            m = jnp.max(s1, axis=-1, keepdims=True)
            if has_ctx:
                s2 = _dot_nt(qs, kc_ref[0, 0, :, cols].astype(BF16))
                m = jnp.maximum(m, jnp.max(s2, axis=-1, keepdims=True))
            e1 = jnp.exp(s1 - m)
            den = jnp.sum(e1, axis=-1, keepdims=True)
            pv = _dot(e1.astype(BF16), v)
            if has_ctx:
                e2 = jnp.exp(s2 - m)
                den = den + jnp.sum(e2, axis=-1, keepdims=True)
                pv = pv + _dot(e2.astype(BF16), vc)
            outs.append(pv / den)
        o = outs[0] - lam * outs[1]
        o = _rms(o) * g_ref[...] * (1.0 - lam_init)
        o_ref[0, :, vcols] = o.astype(BF16)


def _attn_ctx(geo, layer, q, kb, vb, lam_p, subln, lam_init):
    bc, lc = geo.bc, geo.lc
    seq = pl.BlockSpec((1, lc, DA_WIDTH), lambda b: (b, 0, 0))
    view = lambda a: a.reshape(geo.n // lc, lc, DA_WIDTH)
    out = pl.pallas_call(
        functools.partial(_attn_kernel, lam_init=lam_init, has_ctx=False),
        grid=(bc,),
        in_specs=[seq, seq, seq, _layer_spec(lam_p, layer), _layer_spec(subln, layer)],
        out_specs=seq,
        out_shape=jax.ShapeDtypeStruct((bc, lc, DA_WIDTH), BF16),
        compiler_params=_cparams(1),
        name="attn_ctx",
    )(view(q), view(kb), view(vb), lam_p, subln)
    return out.reshape(bc * lc, DA_WIDTH)


def _attn_lat(geo, layer, q, kb, vb, cache_k, cache_v, lam_p, subln, lam_init):
    bl, ll = geo.bl, geo.ll
    tq = TOKEN_BLOCK
    past = cache_k.shape[2]
    off = geo.nc // ll
    qblk = pl.BlockSpec((1, tq, DA_WIDTH), lambda b, j: (b + off, j, 0))
    seq = pl.BlockSpec((1, ll, DA_WIDTH), lambda b, j: (b + off, 0, 0))
    cache = pl.BlockSpec((1, 1, past, DA_WIDTH), lambda b, j: (b, layer, 0, 0))
    view = lambda a: a.reshape(geo.n // ll, ll, DA_WIDTH)
    out = pl.pallas_call(
        functools.partial(_attn_kernel, lam_init=lam_init, has_ctx=True),
        grid=(bl, ll // tq),
        in_specs=[qblk, seq, seq, cache, cache, _layer_spec(lam_p, layer), _layer_spec(subln, layer)],
        out_specs=pl.BlockSpec((1, tq, DA_WIDTH), lambda b, j: (b, j, 0)),
        out_shape=jax.ShapeDtypeStruct((bl, ll, DA_WIDTH), BF16),
        compiler_params=_cparams(2),
        name="attn_lat",
    )(view(q), view(kb), view(vb), cache_k, cache_v, lam_p, subln)
    return out.reshape(bl * ll, DA_WIDTH)


@functools.lru_cache(maxsize=None)
def _dft_tables(length):
    k = np.arange(length, dtype=np.int64)
    ks = (k[:, None] * k[None, :]) % (2 * length)
    ang = ks.astype(np.float64) * (np.pi / length)
    cmat = np.cos(ang)
    smat = -np.sin(ang)
    smat[0, :] = 1.0 - 2.0 * (k % 2)
    to_bf16 = lambda a: jnp.asarray(a.astype(np.float32)).astype(BF16)
    return to_bf16(cmat), to_bf16(smat), to_bf16(smat.T)


@functools.lru_cache(maxsize=None)
def _filter_features(length):
    t = np.linspace(0.0, 1.0, length, dtype=np.float32)
    w_ang = (2.0 * math.pi * np.arange(length, dtype=np.float32) / length).astype(np.float32)
    f = np.linspace(1e-4, HY_BANDS - 1, HY_BANDS, dtype=np.float32)
    arg = (w_ang[:, None] * f[None, :]).astype(np.float32).astype(np.float64)
    z = np.concatenate([t[:, None].astype(np.float64), np.cos(arg), -np.sin(arg)], axis=-1)
    z = np.pad(z, ((0, 0), (0, 64 - HY_EMB))).astype(np.float32)
    rev = np.concatenate([z[:1], z[:0:-1]], axis=0)
    deltas = np.linspace(math.log(HY_TARGET) / HY_FAST, math.log(HY_TARGET) / HY_SLOW,
                         HY_WIDTH, dtype=np.float32)
    return jnp.asarray(z), jnp.asarray(rev), jnp.asarray(np.abs(deltas)[None, :])


def _hyfilter_kernel(z_ref, zr_ref, dl_ref, w1_ref, b1_ref, w2_ref, b2_ref, w3_ref, fr_ref, o_ref):
    c = pl.program_id(0)
    fr = fr_ref[...]

    def mlp(z, w3):
        hid = jnp.sin(fr[0:1] * (_dot3(z, w1_ref[...]) + b1_ref[...]))
        hid = jnp.sin(fr[1:2] * (_dot3(hid, w2_ref[...]) + b2_ref[...]))
        return _dot3(hid, w3) * jnp.exp(-z[:, 0:1] * dl_ref[...])

    z, zr = z_ref[...], zr_ref[...]
    o_ref[0, :, 0:HY_WIDTH] = mlp(z, w3_ref[:, 0:HY_WIDTH]).astype(BF16)
    bwd = mlp(zr, w3_ref[:, HY_WIDTH:2 * HY_WIDTH])
    row = lax.broadcasted_iota(jnp.int32, bwd.shape, 0) + c * z.shape[0]
    o_ref[0, :, HY_WIDTH:2 * HY_WIDTH] = jnp.where(row == 0, 0.0, bwd).astype(BF16)


def _hyfilter(length, layer, w1p, b1, w2, b2, w3, fr):
    z, zr, dl = _filter_features(length)
    tb = TOKEN_BLOCK
    blk = pl.BlockSpec((tb, 64), lambda c: (c, 0))
    lay = lambda a: _layer_spec(a, layer)
    return pl.pallas_call(
        _hyfilter_kernel,
        grid=(length // tb,),
        in_specs=[blk, blk, pl.BlockSpec(dl.shape, lambda c: (0, 0)),
                  lay(w1p), lay(b1), lay(w2), lay(b2), lay(w3), lay(fr)],
        out_specs=pl.BlockSpec((1, tb, 2 * HY_WIDTH), lambda c: (0, c, 0)),
        out_shape=jax.ShapeDtypeStruct((1, length, 2 * HY_WIDTH), BF16),
        compiler_params=_cparams(1),
        name="hyena_filter",
    )(z, zr, dl, w1p, b1, w2, b2, w3, fr)


def _hyconv_kernel(main_ref, before_ref, after_ref, cw_ref, cb_ref, z_ref, zb_ref, x0_ref, pad_ref, *, geo):
    i = pl.program_id(0)
    tb = TOKEN_BLOCK
    _fill_padded(pad_ref, before_ref, main_ref, after_ref, geo.is_start(i), geo.is_end(i))
    w = HY_WIDTH
    parts = []
    for p in range(3):
        cols = slice(p * w, (p + 1) * w)
        uc = (pad_ref[HALO - 1:HALO - 1 + tb, cols] * cw_ref[0:1, cols]
              + main_ref[:, cols] * cw_ref[1:2, cols]
              + pad_ref[HALO + 1:HALO + 1 + tb, cols] * cw_ref[2:3, cols]
              + cb_ref[:, cols])
        parts.append(uc)
    x0, x1, v = parts
    z = v * x1
    z_ref[...] = z
    zb_ref[...] = z.astype(BF16)
    x0_ref[...] = x0


def _hyconv(geo, layer, u_hy, conv_w, conv_b):
    tb = TOKEN_BLOCK
    w3 = 3 * HY_WIDTH
    before, after = geo.halo_specs(w3)
    row = pl.BlockSpec((tb, HY_WIDTH), lambda i: (i, 0))
    return pl.pallas_call(
        functools.partial(_hyconv_kernel, geo=geo),
        grid=(geo.n // tb,),
        in_specs=[pl.BlockSpec((tb, w3), lambda i: (i, 0)), before, after,
                  _layer_spec(conv_w, layer), _layer_spec(conv_b, layer)],
        out_specs=[row, row, row],
        out_shape=[jax.ShapeDtypeStruct((geo.n, HY_WIDTH), F32),
                   jax.ShapeDtypeStruct((geo.n, HY_WIDTH), BF16),
                   jax.ShapeDtypeStruct((geo.n, HY_WIDTH), F32)],
        scratch_shapes=[pltpu.VMEM((tb + 2 * HALO, w3), F32)],
        compiler_params=_cparams(1),
        name="hyena_conv_gate",
    )(u_hy, u_hy, u_hy, conv_w, conv_b)


def _dft_fwd_kernel(c_ref, s_ref, x_ref, re_ref, im_ref):
    x = x_ref[0]
    re_ref[0] = _dot(c_ref[...], x)
    im_ref[0] = _dot(s_ref[...], x)


def _dft_fwd(length, x, nseq, seq_off):
    cmat, smat, _ = _dft_tables(length)
    n = x.shape[-1]
    tm = min(length, 512)
    tn = 512
    a_spec = pl.BlockSpec((tm, length), lambda b, j, m: (m, 0))
    o_spec = pl.BlockSpec((1, tm, tn), lambda b, j, m: (b, m, j))
    return pl.pallas_call(
        _dft_fwd_kernel,
        grid=(nseq, n // tn, length // tm),
        in_specs=[a_spec, a_spec, pl.BlockSpec((1, length, tn), lambda b, j, m: (b + seq_off, 0, j))],
        out_specs=[o_spec, o_spec],
        out_shape=[jax.ShapeDtypeStruct((nseq, length, n), F32)] * 2,
        compiler_params=_cparams(3),
        name="hyena_dft",
    )(cmat, smat, x)


def _hyprod_kernel(zr_ref, zi_ref, hfr_ref, hgr_ref, hfi_ref, hgi_ref, yr_ref, yi_ref, *, length):
    c = pl.program_id(1)
    shape = zr_ref.shape[1:]
    k = lax.broadcasted_iota(jnp.int32, shape, 0) + c * shape[0]
    sgn = (1 - 2 * (k % 2)).astype(F32)
    hr = hfr_ref[0] + sgn * hgr_ref[0]
    hi = hfi_ref[0] + sgn * hgi_ref[0]
    zr, zi = zr_ref[0], zi_ref[0]
    inv = 1.0 / length
    yr = (zr * hr - zi * hi) * inv
    yi = (zr * hi + zi * hr) * inv
    first = k == 0
    yr_ref[0] = jnp.where(first, zr * hr * (0.5 * inv), yr).astype(BF16)
    yi_ref[0] = jnp.where(first, zi * hi * (0.5 * inv), yi).astype(BF16)


def _hyprod(length, zre, zim, hre, him):
    nb = zre.shape[0]
    tb = TOKEN_BLOCK
    zs = pl.BlockSpec((1, tb, HY_WIDTH), lambda b, c: (b, c, 0))
    hf = pl.BlockSpec((1, tb, HY_WIDTH), lambda b, c: (0, c, 0))
    hg = pl.BlockSpec((1, tb, HY_WIDTH), lambda b, c: (0, c, 1))
    return pl.pallas_call(
        functools.partial(_hyprod_kernel, length=length),
        grid=(nb, length // tb),
        in_specs=[zs, zs, hf, hg, hf, hg],
        out_specs=[zs, zs],
        out_shape=[jax.ShapeDtypeStruct((nb, length, HY_WIDTH), BF16)] * 2,
        compiler_params=_cparams(2),
        name="hyena_spectral_product",
    )(zre, zim, hre, hre, him, him)


def _hyinv_kernel(c_ref, st_ref, yr_ref, yi_ref, z_ref, x0_ref, skip_ref, o_ref):
    y = _dot(c_ref[...], yr_ref[0]) + _dot(st_ref[...], yi_ref[0])
    o_ref[0] = ((y + z_ref[0] * skip_ref[...]) * x0_ref[0]).astype(BF16)


def _hyinv(length, layer, yr, yi, z, x0, skip, seq_off):
    cmat, _, smat_t = _dft_tables(length)
    nseq = yr.shape[0]
    tm = min(length, 512)
    a_spec = pl.BlockSpec((tm, length), lambda b, m: (m, 0))
    y_spec = pl.BlockSpec((1, length, HY_WIDTH), lambda b, m: (b, 0, 0))
    t_spec = pl.BlockSpec((1, tm, HY_WIDTH), lambda b, m: (b + seq_off, m, 0))
    return pl.pallas_call(
        _hyinv_kernel,
        grid=(nseq, length // tm),
        in_specs=[a_spec, a_spec, y_spec, y_spec, t_spec, t_spec, _layer_spec(skip, layer)],
        out_specs=pl.BlockSpec((1, tm, HY_WIDTH), lambda b, m: (b, m, 0)),
        out_shape=jax.ShapeDtypeStruct((nseq, length, HY_WIDTH), BF16),
        compiler_params=_cparams(2),
        name="hyena_idft",
    )(cmat, smat_t, yr, yi, z, x0, skip)


def _hyena_long_conv(length, layer, nseq, seq_off, zb, z, x0, filt, skip):
    view = lambda a: a.reshape(a.shape[0] // length, length, HY_WIDTH)
    zre, zim = _dft_fwd(length, view(zb), nseq, seq_off)
    hre, him = _dft_fwd(length, filt, 1, 0)
    yr, yi = _hyprod(length, zre, zim, hre, him)
    out = _hyinv(length, layer, yr, yi, view(z), view(x0), skip, seq_off)
    return out.reshape(nseq * length, HY_WIDTH)


def _merge_kernel(x_ref, mod_ref, g1_ref, a_ref, bc_ref, bl_ref, cc_ref, cl_ref, wg_ref, bg_ref, wbr_ref,
                  wo_ref, g2_ref, wr_ref, br_ref, xo_ref, h2_ref, ti_ref, tp_ref, *, nbc):
    d = D_MODEL
    is_ctx = pl.program_id(0) < nbc
    mod = mod_ref[0]
    shift1, scale1, gate1 = mod[:, 0:d], mod[:, d:2 * d], mod[:, 2 * d:3 * d]
    shift2, scale2 = mod[:, 3 * d:4 * d], mod[:, 4 * d:5 * d]
    x = x_ref[...]
    h = (_rms(x) * g1_ref[...] * (1.0 + scale1) + shift1).astype(BF16)
    branches = (a_ref[...],
                jnp.where(is_ctx, bc_ref[...], bl_ref[...]),
                jnp.where(is_ctx, cc_ref[...], cl_ref[...]))
    merged = None
    for n, br_n in enumerate(branches):
        cols = slice(n * d, (n + 1) * d)
        gate = _sigmoid(_dot(h, wg_ref[:, cols]) + bg_ref[:, cols])
        term = gate * _dot(br_n, wbr_ref[n])
        merged = term if merged is None else merged + term
    x = x + gate1 * _dot(merged.astype(BF16), wo_ref[...])
    xo_ref[...] = x
    h2 = _rms(x) * g2_ref[...] * (1.0 + scale2) + shift2
    h2_ref[...] = h2.astype(BF16)

    logits = _dot3(h2, wr_ref[...]) + br_ref[...]
    ne = logits.shape[-1]
    lane = lax.broadcasted_iota(jnp.int32, logits.shape, 1).astype(F32)
    vals = logits
    top_v, top_i = [], []
    for _ in range(TOP_K):
        m = jnp.max(vals, axis=-1, keepdims=True)
        idx = jnp.min(jnp.where(vals == m, lane, float(ne)), axis=-1, keepdims=True)
        top_v.append(m)
        top_i.append(idx)
        vals = jnp.where(lane == idx, -jnp.inf, vals)
    es = [jnp.exp(v - top_v[0]) for v in top_v]
    den = es[0] + es[1] + es[2] + es[3]
    for kk in range(TOP_K):
        ti_ref[:, kk:kk + 1] = top_i[kk].astype(jnp.int32)
        tp_ref[:, kk:kk + 1] = es[kk] / den


def _merge(geo, layer, x, mods, g1, a_out, b_ctx, b_lat, c_ctx, c_lat, wg_b, bg, wbr_b, wo_b, g2, wr, br):
    d = D_MODEL
    tb = WIDE_BLOCK
    row = lambda w: pl.BlockSpec((tb, w), lambda i: (i, 0))
    ctx, lat = geo.split_specs(tb, DA_WIDTH)
    lay = lambda a: _layer_spec(a, layer, single_buffer=True)
    return pl.pallas_call(
        functools.partial(_merge_kernel, nbc=geo.nc // tb),
        grid=(geo.n // tb,),
        in_specs=[row(d), geo.mod_spec(layer, tb), lay(g1), row(POOL_WIDTH), ctx, lat, ctx, lat,
                  lay(wg_b), lay(bg), lay(wbr_b), lay(wo_b), lay(g2), lay(wr), lay(br)],
        out_specs=[row(d), row(d), row(TOP_K), row(TOP_K)],
        out_shape=[jax.ShapeDtypeStruct((geo.n, d), F32),
                   jax.ShapeDtypeStruct((geo.n, d), BF16),
                   jax.ShapeDtypeStruct((geo.n, TOP_K), jnp.int32),
                   jax.ShapeDtypeStruct((geo.n, TOP_K), F32)],
        compiler_params=_cparams(1),
        name="merge_route",
    )(x, mods, g1, a_out, b_ctx, b_lat, c_ctx, c_lat, wg_b, bg, wbr_b, wo_b, g2, wr, br)


def _ffn_kernel(sw_ref, st_ref, sn_ref, nv_ref, xs_hbm, wgu_ref, bgu_ref, wd_ref, bd_ref, ys_hbm,
                xbuf, ybuf, semx, semy):
    i = pl.program_id(0)
    nv = nv_ref[0]
    slot = i % 2
    t = MOE_TILE
    half = D_FF // 2

    def rows_of(step, k):
        return pl.ds(pl.multiple_of(st_ref[step] * t, t), k * t)

    def x_copy(step, buf, k):
        return pltpu.make_async_copy(xs_hbm.at[rows_of(step, k)], xbuf.at[buf, pl.ds(0, k * t)], semx.at[buf])

    def y_copy(step, buf, k):
        return pltpu.make_async_copy(ybuf.at[buf, pl.ds(0, k * t)], ys_hbm.at[rows_of(step, k)], semy.at[buf])

    def for_size(step, fn):
        n = sn_ref[step]
        for k in range(1, MOE_MAX_SUB + 1):
            @pl.when(n == k)
            def _(k=k):
                fn(k)

    def compute(k):
        m = k * t
        x = xbuf[slot, 0:m, :]
        y = None
        for c in range(2):
            gcols = slice(c * half, (c + 1) * half)
            ucols = slice(D_FF + c * half, D_FF + (c + 1) * half)
            gate = _dot(x, wgu_ref[0, :, gcols].astype(BF16)) + bgu_ref[0, :, gcols]
            up = _dot(x, wgu_ref[0, :, ucols].astype(BF16)) + bgu_ref[0, :, ucols]
            gate = jnp.minimum(gate, SWIGLU_LIMIT)
            up = jnp.clip(up, -SWIGLU_LIMIT, SWIGLU_LIMIT)
            act = ((up + 1.0) * gate * _sigmoid(SWIGLU_ALPHA * gate)).astype(BF16)
            part = _dot(act, wd_ref[0, gcols, :].astype(BF16))
            y = part if y is None else y + part
        ybuf[slot, 0:m, :] = (y + bd_ref[0]).astype(BF16)
        y_copy(i, slot, k).start()

    @pl.when(i < nv)
    def _():
        @pl.when(i == 0)
        def _():
            for_size(0, lambda k: x_copy(0, 0, k).start())

        for_size(i, lambda k: x_copy(i, slot, k).wait())

        @pl.when(i + 1 < nv)
        def _():
            for_size(i + 1, lambda k: x_copy(i + 1, 1 - slot, k).start())

        @pl.when(i >= 2)
        def _():
            for_size(i - 2, lambda k: y_copy(i - 2, slot, k).wait())

        for_size(i, compute)

        @pl.when(i == nv - 1)
        def _():
            for_size(i, lambda k: y_copy(i, slot, k).wait())

            @pl.when(i >= 1)
            def _():
                for_size(i - 1, lambda k: y_copy(i - 1, 1 - slot, k).wait())


def _ffn(xs, step_weight, step_tile, step_sub, n_valid, w_gu, b_gu, w_down, b_down):
    p = xs.shape[0]
    d = D_MODEL
    t = MOE_TILE
    hbm = pl.BlockSpec(memory_space=pl.ANY)
    wmap = lambda i, sw, st, sn, nv: (sw[i], 0, 0)
    grid_spec = pltpu.PrefetchScalarGridSpec(
        num_scalar_prefetch=4,
        grid=(step_weight.shape[0],),
        in_specs=[hbm,
                  pl.BlockSpec((1, d, 2 * D_FF), wmap), pl.BlockSpec((1, 1, 2 * D_FF), wmap),
                  pl.BlockSpec((1, D_FF, d), wmap), pl.BlockSpec((1, 1, d), wmap)],
        out_specs=hbm,
        scratch_shapes=[pltpu.VMEM((2, MOE_MAX_SUB * t, d), BF16), pltpu.VMEM((2, MOE_MAX_SUB * t, d), BF16),
                        pltpu.SemaphoreType.DMA((2,)), pltpu.SemaphoreType.DMA((2,))])
    return pl.pallas_call(
        _ffn_kernel,
        grid_spec=grid_spec,
        out_shape=jax.ShapeDtypeStruct((p, d), BF16),
        compiler_params=_cparams(1),
        name="moe_experts",
    )(step_weight, step_tile, step_sub, n_valid, xs, w_gu, b_gu, w_down, b_down)


def _moe_rows(n, n_experts):
    worst = n * TOP_K + n_experts * ((n // DISPATCH_BLOCK) * (RUN_ALIGN - 1) + MOE_TILE - 1)
    return -(-worst // MOE_TILE) * MOE_TILE


def _route(top_i, n_experts):
    n = top_i.shape[0]
    t = MOE_TILE
    nb = n // DISPATCH_BLOCK
    experts = jnp.arange(n_experts, dtype=jnp.int32)
    onehot = top_i.reshape(nb, DISPATCH_BLOCK * TOP_K)[:, :, None] == experts[None, None, :]
    n_be = jnp.sum(onehot, axis=1, dtype=jnp.int32)
    cap = (n_be + RUN_ALIGN - 1) // RUN_ALIGN * RUN_ALIGN
    rows_e = jnp.sum(cap, axis=0)
    tiles_e = (rows_e + t - 1) // t
    tile_end = jnp.cumsum(tiles_e)
    tile_start = tile_end - tiles_e
    run_dst = tile_start[None, :] * t + jnp.cumsum(cap, axis=0) - cap
    run_loc = jnp.cumsum(cap, axis=1) - cap
    tail_dst = tile_start * t + rows_e
    tail = tiles_e * t - rows_e

    def owner(ends, ids):
        return jnp.sum(ends[None, :] <= ids[:, None], axis=1, dtype=jnp.int32)[:, None] == experts[None, :]

    pick = lambda oh, table: jnp.sum(jnp.where(oh, table[None, :], 0), axis=1)

    steps_e = (tiles_e + MOE_MAX_SUB - 1) // MOE_MAX_SUB
    step_end = jnp.cumsum(steps_e)
    n_steps = _moe_rows(n, n_experts) // (t * MOE_MAX_SUB) + n_experts
    n_valid = step_end[-1]
    step_ids = jnp.minimum(jnp.arange(n_steps, dtype=jnp.int32), n_valid - 1)
    step_oh = owner(step_end, step_ids)
    j = step_ids - pick(step_oh, step_end - steps_e)
    step_expert = jnp.argmax(step_oh, axis=1).astype(jnp.int32)
    n_st = jnp.maximum(pick(step_oh, steps_e), 1)
    n_ti = pick(step_oh, tiles_e)
    base, rem = n_ti // n_st, n_ti % n_st
    extra = jnp.maximum(j - (n_st - rem), 0)
    step_tile = pick(step_oh, tile_start) + j * base + extra
    step_sub = jnp.clip(base + (j >= n_st - rem).astype(jnp.int32), 1, MOE_MAX_SUB)
    units = lambda a: (a // RUN_ALIGN).reshape(-1).astype(jnp.int32)
    runs = (units(run_dst), units(run_loc), units(cap), units(tail_dst), units(tail))
    cols = (run_dst.reshape(nb, n_experts, 1).astype(F32), run_loc.reshape(nb, n_experts, 1).astype(F32))
    steps = (step_expert, step_tile.astype(jnp.int32), step_sub.astype(jnp.int32),
             n_valid.reshape(1).astype(jnp.int32))
    return runs, cols, steps


def _sized_copies(units, src_at, dst_at, sem, max_bits, fn):
    for bit in range(max_bits):
        size = RUN_ALIGN << bit

        @pl.when((units >> bit) & 1 == 1)
        def _(bit=bit, size=size):
            off = (units & ((1 << bit) - 1)) * RUN_ALIGN
            fn(pltpu.make_async_copy(src_at(off, size), dst_at(off, size), sem))


def _dispatch_kernel(rdst_ref, rloc_ref, rcap_ref, tdst_ref, tlen_ref,
                     h_ref, tt_ref, dcol_ref, lcol_ref, xs_hbm, slot_ref, stage, zeros, sem, zsem,
                     *, n_experts):
    b = pl.program_id(0)
    nb = pl.num_programs(0)
    tb = DISPATCH_BLOCK
    buf = b % 2
    run_bits = (tb // RUN_ALIGN).bit_length()
    tail_bits = (MOE_TILE // RUN_ALIGN).bit_length()

    def run_copies(blk, slot, fn):
        def body(e, carry):
            idx = blk * n_experts + e
            src0 = pl.multiple_of(rloc_ref[idx] * RUN_ALIGN, RUN_ALIGN)
            dst0 = pl.multiple_of(rdst_ref[idx] * RUN_ALIGN, RUN_ALIGN)
            _sized_copies(rcap_ref[idx],
                          lambda off, size: stage.at[slot, pl.ds(pl.multiple_of(src0 + off, RUN_ALIGN), size)],
                          lambda off, size: xs_hbm.at[pl.ds(pl.multiple_of(dst0 + off, RUN_ALIGN), size)],
                          sem.at[slot], run_bits, fn)
            return carry
        lax.fori_loop(0, n_experts, body, 0)

    def tail_copies(fn):
        def body(e, carry):
            dst0 = pl.multiple_of(tdst_ref[e] * RUN_ALIGN, RUN_ALIGN)
            _sized_copies(tlen_ref[e],
                          lambda off, size: zeros.at[pl.ds(0, size)],
                          lambda off, size: xs_hbm.at[pl.ds(pl.multiple_of(dst0 + off, RUN_ALIGN), size)],
                          zsem, tail_bits, fn)
            return carry
        lax.fori_loop(0, n_experts, body, 0)

    @pl.when(b == 0)
    def _():
        zeros[...] = jnp.zeros(zeros.shape, zeros.dtype)
        tail_copies(lambda cp: cp.start())

    @pl.when(b >= 2)
    def _():
        run_copies(b - 2, buf, lambda cp: cp.wait())

    tt = tt_ref[...]
    e_iota = lax.broadcasted_iota(jnp.int32, (n_experts, tb), 0)
    hit = [e_iota == tt[k:k + 1, :] for k in range(TOP_K)]
    member = jnp.where(hit[0] | hit[1] | hit[2] | hit[3], 1.0, 0.0)
    before = (lax.broadcasted_iota(jnp.int32, (tb, tb), 0)
              < lax.broadcasted_iota(jnp.int32, (tb, tb), 1))
    rank = _dot(member.astype(BF16), jnp.where(before, 1.0, 0.0).astype(BF16))
    loc = [jnp.sum(jnp.where(hit[k], lcol_ref[0] + rank, 0.0), axis=0, keepdims=True) for k in range(TOP_K)]
    for k in range(TOP_K):
        slot_ref[k:k + 1, :] = jnp.sum(jnp.where(hit[k], dcol_ref[0] + rank, 0.0),
                                       axis=0, keepdims=True).astype(jnp.int32)

    rows = stage.shape[1]
    r_iota = lax.broadcasted_iota(jnp.int32, (rows, tb), 0).astype(F32)
    place = jnp.zeros((rows, tb), F32)
    for k in range(TOP_K):
        place = jnp.where(r_iota == loc[k], 1.0, place)
    stage[buf] = _dot(place.astype(BF16), h_ref[...]).astype(BF16)
    run_copies(b, buf, lambda cp: cp.start())

    @pl.when(b == nb - 1)
    def _():
        run_copies(b, buf, lambda cp: cp.wait())

        @pl.when(b >= 1)
        def _():
            run_copies(b - 1, 1 - buf, lambda cp: cp.wait())
        tail_copies(lambda cp: cp.wait())


def _dispatch(h2, top_t, runs, cols, n_experts):
    n, d = h2.shape
    tb = DISPATCH_BLOCK
    stage_rows = tb * TOP_K + n_experts * RUN_ALIGN
    col = pl.BlockSpec((1, n_experts, 1), lambda b, *_: (b, 0, 0))
    grid_spec = pltpu.PrefetchScalarGridSpec(
        num_scalar_prefetch=5,
        grid=(n // tb,),
        in_specs=[pl.BlockSpec((tb, d), lambda b, *_: (b, 0)),
                  pl.BlockSpec((TOP_K, tb), lambda b, *_: (0, b)), col, col],
        out_specs=[pl.BlockSpec(memory_space=pl.ANY), pl.BlockSpec((TOP_K, tb), lambda b, *_: (0, b))],
        scratch_shapes=[pltpu.VMEM((2, stage_rows, d), BF16), pltpu.VMEM((MOE_TILE, d), BF16),
                        pltpu.SemaphoreType.DMA((2,)), pltpu.SemaphoreType.DMA(())])
    return pl.pallas_call(
        functools.partial(_dispatch_kernel, n_experts=n_experts),
        grid_spec=grid_spec,
        out_shape=[jax.ShapeDtypeStruct((_moe_rows(n, n_experts), d), BF16),
                   jax.ShapeDtypeStruct((TOP_K, n), jnp.int32)],
        compiler_params=_cparams(1),
        name="moe_dispatch",
    )(*runs, h2, top_t, *cols)


def _moe_sum(x_ref, mod_ref, yg_ref, p_ref):
    d = D_MODEL
    gate2 = mod_ref[0][:, 5 * d:6 * d]
    p = p_ref[...]
    moe = p[:, 0:1] * yg_ref[0].astype(F32)
    for kk in range(1, TOP_K):
        moe = moe + p[:, kk:kk + 1] * yg_ref[kk].astype(F32)
    return x_ref[...] + gate2 * moe


def _combine_kernel(x_ref, mod_ref, yg_ref, p_ref, xo_ref):
    xo_ref[...] = _moe_sum(x_ref, mod_ref, yg_ref, p_ref)


def _combine_final_kernel(x_ref, mod_ref, yg_ref, p_ref, gf_ref, yc_ref, yl_ref, *, nbc):
    i = pl.program_id(0)
    y = _rms(_moe_sum(x_ref, mod_ref, yg_ref, p_ref)) * gf_ref[...]

    @pl.when(i < nbc)
    def _():
        yc_ref[...] = y

    @pl.when(i >= nbc)
    def _():
        yl_ref[...] = y


def _combine(geo, layer, x, mods, yg, top_p, final_g=None):
    d = D_MODEL
    tb = TOKEN_BLOCK
    row = pl.BlockSpec((tb, d), lambda i: (i, 0))
    in_specs = [row, geo.mod_spec(layer, tb),
                pl.BlockSpec((TOP_K, tb, d), lambda i: (0, i, 0)),
                pl.BlockSpec((tb, TOP_K), lambda i: (i, 0))]
    if final_g is None:
        return pl.pallas_call(
            _combine_kernel, grid=(geo.n // tb,), in_specs=in_specs, out_specs=row,
            out_shape=jax.ShapeDtypeStruct((geo.n, d), F32),
            compiler_params=_cparams(1), name="moe_combine",
        )(x, mods, yg, top_p)
    ctx, lat = geo.split_specs(tb, d)
    return pl.pallas_call(
        functools.partial(_combine_final_kernel, nbc=geo.nc // tb),
        grid=(geo.n // tb,),
        in_specs=in_specs + [pl.BlockSpec(final_g.shape, lambda i: (0, 0))],
        out_specs=[ctx, lat],
        out_shape=[jax.ShapeDtypeStruct((geo.nc, d), F32), jax.ShapeDtypeStruct((geo.nl, d), F32)],
        compiler_params=_cparams(1), name="moe_combine_final",
    )(x, mods, yg, top_p, final_g)


@functools.lru_cache(maxsize=None)
def _rope_tables(length):
    rows = length // GRID_W
    row = np.repeat(np.arange(rows), GRID_W).astype(np.float32)
    col = np.tile(np.arange(GRID_W), rows).astype(np.float32)
    ax = DA_HEAD_DIM // 2
    inv = (ROPE_BASE ** (-(np.arange(ax // 2, dtype=np.float32) * 2.0 / ax))).astype(np.float32)
    ang_r = (row[:, None] * inv).astype(np.float32)
    ang_c = (col[:, None] * inv).astype(np.float32)
    ang = np.concatenate([ang_r, ang_r, ang_c, ang_c], axis=-1).astype(np.float64)
    sign = np.where((np.arange(DA_HEAD_DIM) % 32) < 16, -1.0, 1.0)
    reps = DA_WIDTH // DA_HEAD_DIM
    cos = np.tile(np.cos(ang), (1, reps)).astype(np.float32)
    sin_signed = np.tile(np.sin(ang) * sign[None, :], (1, reps)).astype(np.float32)
    return jnp.asarray(cos), jnp.asarray(sin_signed)


def kernel(x_prompt, x_sample, cache_k, cache_v, c, c_ctx, w_ada, b_ada, norm1, norm2, w_in, w_pool, pool_scale, da_lambda, da_subln, hy_conv_w, hy_conv_b, hy_f_w1, hy_f_b1, hy_f_w2, hy_f_b2, hy_f_w3, hy_sin_freq, hy_skip, w_branch, w_gate, b_gate, w_o, w_router, b_router, w_gu, b_gu, w_down, b_down, final_norm):
    bc, lc, d = x_prompt.shape
    bl, ll, _ = x_sample.shape
    depth = w_in.shape[0]
    n_experts = w_router.shape[-1]
    past = cache_k.shape[2]
    geo = _Geom(bc, lc, bl, ll)
    assert 1 + bl <= MOD_ROWS

    x = jnp.concatenate([x_prompt.reshape(bc * lc, d), x_sample.reshape(bl * ll, d)], axis=0)
    cond = jnp.concatenate([c_ctx[None], c, jnp.zeros((MOD_ROWS - 1 - bl, d), F32)], axis=0)
    mods = _ada(cond, w_ada, b_ada).reshape(depth * MOD_ROWS, 1, N_MOD * d)
    cos_t, sin_t = _rope_tables(ll)
    ck = cache_k.reshape(bl, depth, past, DA_WIDTH)
    cv = cache_v.reshape(bl, depth, past, DA_WIDTH)

    row3 = lambda a: a.reshape(depth, 1, a.shape[-1])
    g1, g2 = row3(norm1), row3(norm2)
    w_in_b = w_in.astype(BF16)
    w_pool_b = w_pool.astype(BF16)
    w_gate_b = w_gate.astype(BF16)
    w_branch_b = w_branch.astype(BF16)
    w_o_b = w_o.astype(BF16)
    w1p = jnp.pad(hy_f_w1, ((0, 0), (0, 64 - HY_EMB), (0, 0)))
    fargs = (w1p, row3(hy_f_b1), hy_f_w2, row3(hy_f_b2), hy_f_w3, hy_sin_freq)
    w_gu_s = w_gu.reshape(depth * n_experts, d, 2 * D_FF)
    b_gu_s = b_gu.reshape(depth * n_experts, 1, 2 * D_FF)
    w_down_s = w_down.reshape(depth * n_experts, D_FF, d)
    b_down_s = b_down.reshape(depth * n_experts, 1, d)

    new_k = jnp.zeros((bc, depth, lc * DA_HEADS, DA_VDIM), F32)
    new_v = jnp.zeros((bc, depth, lc * DA_HEADS, DA_VDIM), F32)
    for l in range(depth):
        u_pool, q, kb, vb, new_k, new_v, u_hy = _inproj(geo, l, x, mods, g1, w_in_b, cos_t, sin_t, new_k, new_v)
        a_out = _pool(geo, l, u_pool, w_pool_b, row3(pool_scale))

        lam_init = 0.8 - 0.6 * math.exp(-0.3 * l)
        subln = row3(da_subln)
        b_ctx = _attn_ctx(geo, l, q, kb, vb, da_lambda, subln, lam_init)
        b_lat = _attn_lat(geo, l, q, kb, vb, ck, cv, da_lambda, subln, lam_init)

        z, zb, x0 = _hyconv(geo, l, u_hy, hy_conv_w, row3(hy_conv_b))
        skip = row3(hy_skip)
        c_ctx_out = _hyena_long_conv(lc, l, bc, 0, zb, z, x0, _hyfilter(lc, l, *fargs), skip)
        c_lat_out = _hyena_long_conv(ll, l, bl, geo.nc // ll, zb, z, x0, _hyfilter(ll, l, *fargs), skip)

        x, h2, top_i, top_p = _merge(geo, l, x, mods, g1, a_out, b_ctx, b_lat, c_ctx_out, c_lat_out,
                                     w_gate_b, row3(b_gate), w_branch_b, w_o_b, g2, w_router, row3(b_router))

        runs, cols, (step_expert, step_tile, step_sub, n_valid) = _route(top_i, n_experts)
        xs, pair_slot = _dispatch(h2, top_i.T, runs, cols, n_experts)
        ys = _ffn(xs, step_expert + l * n_experts, step_tile, step_sub, n_valid,
                  w_gu_s, b_gu_s, w_down_s, b_down_s)
        yg = jnp.take(ys, pair_slot, axis=0, mode="clip")
        if l + 1 < depth:
            x = _combine(geo, l, x, mods, yg, top_p)
        else:
            y_ctx, y_lat = _combine(geo, l, x, mods, yg, top_p, final_norm[None])

    return (y_ctx.reshape(bc, lc, d), y_lat.reshape(bl, ll, d),
            new_k.reshape(bc, depth, lc, DA_HEADS, 2 * DA_HEAD_DIM),
            new_v.reshape(bc, depth, lc, DA_HEADS, DA_VDIM))
```

```python
import functools
import math

import numpy as np
import jax
import jax.numpy as jnp
from jax import lax
from jax.experimental import pallas as pl
from jax.experimental.pallas import tpu as pltpu

F32 = jnp.float32
BF16 = jnp.bfloat16

D_MODEL = 1024
GRID_W = 64
NORM_EPS = 1e-6
POOL_WIDTH = 512
POOL_WINDOWS = (2, 4, 8, 16)
POOL_GC = POOL_WIDTH // len(POOL_WINDOWS)
DA_HEADS = 4
DA_HEAD_DIM = 64
DA_VDIM = 2 * DA_HEAD_DIM
DA_WIDTH = DA_HEADS * DA_VDIM
ROPE_BASE = 10000.0
HY_WIDTH = 512
HY_EMB = 33
HY_BANDS = (HY_EMB - 1) // 2
HY_HIDDEN = 64
HY_FAST = 0.3
HY_SLOW = 1.5
HY_TARGET = 1e-2
N_BRANCH = 3
D_IN = POOL_WIDTH + 3 * DA_WIDTH + 3 * HY_WIDTH
TOP_K = 4
D_FF = 1024
SWIGLU_ALPHA = 1.702
SWIGLU_LIMIT = 7.0
N_MOD = 6
MOD_ROWS = 8

TOKEN_BLOCK = 256
WIDE_BLOCK = 512
HALO = 8
MOE_TILE = 256
MOE_MAX_SUB = 4
DISPATCH_BLOCK = 512
RUN_ALIGN = 16
V7X_VMEM_LIMIT = 56 * 1024 * 1024


def _cparams(n_axes):
    return pltpu.CompilerParams(
        dimension_semantics=("arbitrary",) * n_axes,
        vmem_limit_bytes=V7X_VMEM_LIMIT)


def _dot(a, b):
    return jnp.dot(a, b, preferred_element_type=F32)


def _dot_nt(a, b):
    return lax.dot_general(a, b, (((1,), (1,)), ((), ())), preferred_element_type=F32)


def _split_bf16(a):
    hi = a.astype(BF16)
    lo = (a - hi.astype(F32)).astype(BF16)
    return hi, lo


def _dot3(a, b):
    ah, al = _split_bf16(a)
    bh, bl = _split_bf16(b)
    return _dot(ah, bh) + _dot(al, bh) + _dot(ah, bl)


def _sigmoid(x):
    return 1.0 / (1.0 + jnp.exp(-x))


def _rms(x):
    return x * lax.rsqrt(jnp.mean(x * x, axis=-1, keepdims=True) + NORM_EPS)


class _Geom:
    def __init__(self, n_ctx_seq, ctx_len, n_lat_seq, lat_len):
        assert ctx_len == TOKEN_BLOCK, "one context sequence per token block"
        assert lat_len % WIDE_BLOCK == 0 and (n_ctx_seq * ctx_len) % lat_len == 0
        self.bc, self.lc, self.bl, self.ll = n_ctx_seq, ctx_len, n_lat_seq, lat_len
        self.nc = n_ctx_seq * ctx_len
        self.nl = n_lat_seq * lat_len
        self.n = self.nc + self.nl

    def group(self, i, tb):
        nbc = self.nc // tb
        return jnp.where(i < nbc, 0, 1 + (i - nbc) // (self.ll // tb))

    def pos_block(self, i, tb):
        nbc = self.nc // tb
        return jnp.where(i < nbc, 0, (i - nbc) % (self.ll // tb))

    def is_start(self, i):
        nbc, bpl = self.nc // TOKEN_BLOCK, self.ll // TOKEN_BLOCK
        return jnp.logical_or(i < nbc, (i - nbc) % bpl == 0)

    def is_end(self, i):
        nbc, bpl = self.nc // TOKEN_BLOCK, self.ll // TOKEN_BLOCK
        return jnp.logical_or(i < nbc, (i - nbc) % bpl == bpl - 1)

    def mod_spec(self, layer, tb):
        return pl.BlockSpec((1, 1, N_MOD * D_MODEL),
                            lambda i: (layer * MOD_ROWS + self.group(i, tb), 0, 0))

    def halo_specs(self, width):
        per = TOKEN_BLOCK // HALO
        last = self.n // HALO - 1
        before = pl.BlockSpec((HALO, width), lambda i: (jnp.maximum(i * per - 1, 0), 0))
        after = pl.BlockSpec((HALO, width), lambda i: (jnp.minimum((i + 1) * per, last), 0))
        return before, after

    def split_specs(self, tb, width):
        nbc = self.nc // tb
        last_lat = self.nl // tb - 1
        ctx = pl.BlockSpec((tb, width), lambda i: (jnp.minimum(i, nbc - 1), 0))
        lat = pl.BlockSpec((tb, width), lambda i: (jnp.clip(i - nbc, 0, last_lat), 0))
        return ctx, lat


def _layer_spec(a, layer, single_buffer=False):
    kw = dict(pipeline_mode=pl.Buffered(1)) if single_buffer else {}
    return pl.BlockSpec((None,) + a.shape[1:], lambda *_: (layer,) + (0,) * (a.ndim - 1), **kw)


def _ada_kernel(c_ref, w_ref, b_ref, o_ref):
    c = c_ref[...]
    s = c * _sigmoid(c)
    o_ref[0] = _dot(s.astype(BF16), w_ref[0].astype(BF16)) + b_ref[0]


def _ada(cond, w_ada, b_ada):
    depth, d, n6 = w_ada.shape
    rows = cond.shape[0]
    tn = 1024
    return pl.pallas_call(
        _ada_kernel,
        grid=(depth, n6 // tn),
        in_specs=[pl.BlockSpec((rows, d), lambda l, j: (0, 0)),
                  pl.BlockSpec((1, d, tn), lambda l, j: (l, 0, j)),
                  pl.BlockSpec((1, 1, tn), lambda l, j: (l, 0, j))],
        out_specs=pl.BlockSpec((1, rows, tn), lambda l, j: (l, 0, j)),
        out_shape=jax.ShapeDtypeStruct((depth, rows, n6), F32),
        compiler_params=_cparams(2),
        name="ada_mod",
    )(cond, w_ada, b_ada.reshape(depth, 1, n6))


def _rope(x, cos, sin_signed, first_half):
    d = x.shape[-1]
    partner = jnp.where(first_half, pltpu.roll(x, d - 16, 1), pltpu.roll(x, 16, 1))
    return x * cos + partner * sin_signed


def _inproj_kernel(*refs, nbc, lc, add_moe):
    if add_moe:
        x_ref, pmod_ref, yg_ref, p_ref = refs[:4]
        refs = refs[4:]
        (mod_ref, g_ref, w_ref, cos_ref, sin_ref, _, _,
         xo_ref, up_ref, q_ref, kb_ref, vb_ref, kf_ref, vf_ref, uh_ref) = refs
        x = _moe_sum(x_ref, pmod_ref, yg_ref, p_ref)
        xo_ref[...] = x
    else:
        (x_ref, mod_ref, g_ref, w_ref, cos_ref, sin_ref, _, _,
         up_ref, q_ref, kb_ref, vb_ref, kf_ref, vf_ref, uh_ref) = refs
        x = x_ref[...]
    i = pl.program_id(0)
    d = D_MODEL
    mod = mod_ref[0]
    shift, scale = mod[:, 0:d], mod[:, d:2 * d]
    h = (_rms(x) * g_ref[...] * (1.0 + scale) + shift).astype(BF16)

    c1 = POOL_WIDTH
    c2 = c1 + DA_WIDTH
    c3 = c2 + DA_WIDTH
    c4 = c3 + DA_WIDTH
    up_ref[...] = _dot(h, w_ref[:, 0:c1])
    uh_ref[...] = _dot(h, w_ref[:, c4:D_IN])
    q = _dot(h, w_ref[:, c1:c2]) * (DA_HEAD_DIM ** -0.5)
    k = _dot(h, w_ref[:, c2:c3])
    v = _dot(h, w_ref[:, c3:c4])
    vb_ref[...] = v.astype(BF16)

    @pl.when(i < nbc)
    def _():
        q_ref[...] = q.astype(BF16)
        kb_ref[...] = k.astype(BF16)
        for s in range(k.shape[0] // lc):
            for hh in range(DA_HEADS):
                rows = pl.ds(hh, lc, stride=DA_HEADS)
                cols = slice(hh * DA_VDIM, (hh + 1) * DA_VDIM)
                kf_ref[s, 0, rows, :] = k[s * lc:(s + 1) * lc, cols]
                vf_ref[s, 0, rows, :] = v[s * lc:(s + 1) * lc, cols]

    @pl.when(i >= nbc)
    def _():
        cos, sin_signed = cos_ref[...], sin_ref[...]
        lane = lax.broadcasted_iota(jnp.int32, q.shape, 1)
        first_half = (lane % 32) < 16
        q_ref[...] = _rope(q, cos, sin_signed, first_half).astype(BF16)
        kb_ref[...] = _rope(k, cos, sin_signed, first_half).astype(BF16)


def _inproj(geo, layer, x, mods, g1, w_in_b, cos_t, sin_t, kacc, vacc, moe=None):
    d = D_MODEL
    tb = WIDE_BLOCK
    nbc = geo.nc // tb
    spb = tb // geo.lc
    row = lambda w: pl.BlockSpec((tb, w), lambda i: (i, 0))
    cache = pl.BlockSpec((spb, 1, geo.lc * DA_HEADS, DA_VDIM), lambda i: (jnp.minimum(i, nbc - 1), layer, 0, 0))
    tab = pl.BlockSpec((tb, DA_WIDTH), lambda i: (geo.pos_block(i, tb), 0))
    hbm = pl.BlockSpec(memory_space=pl.ANY)
    in_specs = [geo.mod_spec(layer, tb), _layer_spec(g1, layer),
                _layer_spec(w_in_b, layer, single_buffer=True), tab, tab, hbm, hbm]
    args = [mods, g1, w_in_b, cos_t, sin_t, kacc, vacc]
    out_specs = [row(POOL_WIDTH), row(DA_WIDTH), row(DA_WIDTH), row(DA_WIDTH), cache, cache, row(3 * HY_WIDTH)]
    out_shape = [jax.ShapeDtypeStruct((geo.n, POOL_WIDTH), F32),
                 jax.ShapeDtypeStruct((geo.n, DA_WIDTH), BF16),
                 jax.ShapeDtypeStruct((geo.n, DA_WIDTH), BF16),
                 jax.ShapeDtypeStruct((geo.n, DA_WIDTH), BF16),
                 jax.ShapeDtypeStruct(kacc.shape, F32),
                 jax.ShapeDtypeStruct(vacc.shape, F32),
                 jax.ShapeDtypeStruct((geo.n, 3 * HY_WIDTH), F32)]
    if moe is None:
        in_specs = [row(d)] + in_specs
        args = [x] + args
    else:
        yg, top_p = moe
        in_specs = [row(d), geo.mod_spec(layer - 1, tb),
                    pl.BlockSpec((TOP_K, tb, d), lambda i: (0, i, 0)),
                    pl.BlockSpec((tb, TOP_K), lambda i: (i, 0))] + in_specs
        args = [x, mods, yg, top_p] + args
        out_specs = [row(d)] + out_specs
        out_shape = [jax.ShapeDtypeStruct((geo.n, d), F32)] + out_shape
    n_in, n_out = len(args), len(out_shape)
    outs = pl.pallas_call(
        functools.partial(_inproj_kernel, nbc=nbc, lc=geo.lc, add_moe=moe is not None),
        grid=(geo.n // tb,),
        in_specs=in_specs, out_specs=out_specs, out_shape=out_shape,
        input_output_aliases={n_in - 2: n_out - 3, n_in - 1: n_out - 2},
        compiler_params=_cparams(1),
        name="in_proj",
    )(*args)
    return outs if moe is not None else [x] + list(outs)


def _fill_padded(pad_ref, before_ref, main_ref, after_ref, start, end):
    tb = TOKEN_BLOCK
    zero = jnp.zeros(before_ref.shape, F32)
    pad_ref[0:HALO, :] = jnp.where(start, zero, before_ref[...])
    pad_ref[HALO:HALO + tb, :] = main_ref[...]
    pad_ref[HALO + tb:2 * HALO + tb, :] = jnp.where(end, zero, after_ref[...])


def _pool_kernel(main_ref, before_ref, after_ref, w_ref, s_ref, o_ref, pad_ref, *, geo):
    i = pl.program_id(0)
    tb = TOKEN_BLOCK
    start, end = geo.is_start(i), geo.is_end(i)
    _fill_padded(pad_ref, before_ref, main_ref, after_ref, start, end)
    r = lax.broadcasted_iota(jnp.int32, (tb, 1), 0)
    for g, w in enumerate(POOL_WINDOWS):
        cols = slice(g * POOL_GC, (g + 1) * POOL_GC)
        acc = pad_ref[HALO - w // 2:HALO - w // 2 + tb, cols]
        for j in range(-w // 2 + 1, w // 2):
            acc = acc + pad_ref[HALO + j:HALO + j + tb, cols]
        lo = jnp.where(start, jnp.maximum(r - w // 2, 0), r - w // 2)
        hi = jnp.where(end, jnp.minimum(r + w // 2, tb), r + w // 2)
        mean = acc / (hi - lo).astype(F32)
        dlt = mean - main_ref[:, cols]
        y = _dot(dlt.astype(BF16), w_ref[g])
        o_ref[:, cols] = (y * s_ref[:, cols]).astype(BF16)


def _pool(geo, layer, u_pool, w_pool_b, pool_scale):
    tb = TOKEN_BLOCK
    before, after = geo.halo_specs(POOL_WIDTH)
    return pl.pallas_call(
        functools.partial(_pool_kernel, geo=geo),
        grid=(geo.n // tb,),
        in_specs=[pl.BlockSpec((tb, POOL_WIDTH), lambda i: (i, 0)), before, after,
                  _layer_spec(w_pool_b, layer), _layer_spec(pool_scale, layer)],
        out_specs=pl.BlockSpec((tb, POOL_WIDTH), lambda i: (i, 0)),
        out_shape=jax.ShapeDtypeStruct((geo.n, POOL_WIDTH), BF16),
        scratch_shapes=[pltpu.VMEM((tb + 2 * HALO, POOL_WIDTH), F32)],
        compiler_params=_cparams(1),
        name="pool_branch",
    )(u_pool, u_pool, u_pool, w_pool_b, pool_scale)


def _attn_kernel(*refs, lam_init, has_ctx):
    if has_ctx:
        q_ref, k_ref, v_ref, kc_ref, vc_ref, lam_ref, g_ref, o_ref = refs
    else:
        q_ref, k_ref, v_ref, lam_ref, g_ref, o_ref = refs
    lp = lam_ref[...]
    lam = (jnp.exp(jnp.sum(lp[0:1] * lp[1:2], axis=-1, keepdims=True))
           - jnp.exp(jnp.sum(lp[2:3] * lp[3:4], axis=-1, keepdims=True)) + lam_init)
    for h in range(DA_HEADS):
        vcols = slice(h * DA_VDIM, (h + 1) * DA_VDIM)
        v = v_ref[0, :, vcols]
        if has_ctx:
            vc = vc_ref[0, 0, :, vcols].astype(BF16)
        outs = []
        for sub in range(2):
            c0 = h * DA_VDIM + sub * DA_HEAD_DIM
            cols = slice(c0, c0 + DA_HEAD_DIM)
            qs = q_ref[0, :, cols]
            s1 = _dot_nt(qs, k_ref[0, :, cols])
            m = jnp.max(s1, axis=-1, keepdims=True)
            if has_ctx:
                s2 = _dot_nt(qs, kc_ref[0, 0, :, cols].astype(BF16))
                m = jnp.maximum(m, jnp.max(s2, axis=-1, keepdims=True))
            e1 = jnp.exp(s1 - m)
            den = jnp.sum(e1, axis=-1, keepdims=True)
            pv = _dot(e1.astype(BF16), v)
            if has_ctx:
                e2 = jnp.exp(s2 - m)
                den = den + jnp.sum(e2, axis=-1, keepdims=True)
                pv = pv + _dot(e2.astype(BF16), vc)
            outs.append(pv / den)
        o = outs[0] - lam * outs[1]
        o = _rms(o) * g_ref[...] * (1.0 - lam_init)
        o_ref[0, :, vcols] = o.astype(BF16)


def _attn_ctx(geo, layer, q, kb, vb, lam_p, subln, lam_init):
    bc, lc = geo.bc, geo.lc
    seq = pl.BlockSpec((1, lc, DA_WIDTH), lambda b: (b, 0, 0))
    view = lambda a: a.reshape(geo.n // lc, lc, DA_WIDTH)
    out = pl.pallas_call(
        functools.partial(_attn_kernel, lam_init=lam_init, has_ctx=False),
        grid=(bc,),
        in_specs=[seq, seq, seq, _layer_spec(lam_p, layer), _layer_spec(subln, layer)],
        out_specs=seq,
        out_shape=jax.ShapeDtypeStruct((bc, lc, DA_WIDTH), BF16),
        compiler_params=_cparams(1),
        name="attn_ctx",
    )(view(q), view(kb), view(vb), lam_p, subln)
    return out.reshape(bc * lc, DA_WIDTH)


def _attn_lat(geo, layer, q, kb, vb, cache_k, cache_v, lam_p, subln, lam_init):
    bl, ll = geo.bl, geo.ll
    tq = TOKEN_BLOCK
    past = cache_k.shape[2]
    off = geo.nc // ll
    qblk = pl.BlockSpec((1, tq, DA_WIDTH), lambda b, j: (b + off, j, 0))
    seq = pl.BlockSpec((1, ll, DA_WIDTH), lambda b, j: (b + off, 0, 0))
    cache = pl.BlockSpec((1, 1, past, DA_WIDTH), lambda b, j: (b, layer, 0, 0))
    view = lambda a: a.reshape(geo.n // ll, ll, DA_WIDTH)
    out = pl.pallas_call(
        functools.partial(_attn_kernel, lam_init=lam_init, has_ctx=True),
        grid=(bl, ll // tq),
        in_specs=[qblk, seq, seq, cache, cache, _layer_spec(lam_p, layer), _layer_spec(subln, layer)],
        out_specs=pl.BlockSpec((1, tq, DA_WIDTH), lambda b, j: (b, j, 0)),
        out_shape=jax.ShapeDtypeStruct((bl, ll, DA_WIDTH), BF16),
        compiler_params=_cparams(2),
        name="attn_lat",
    )(view(q), view(kb), view(vb), cache_k, cache_v, lam_p, subln)
    return out.reshape(bl * ll, DA_WIDTH)


@functools.lru_cache(maxsize=None)
def _dft_tables(length):
    k = np.arange(length, dtype=np.int64)
    ks = (k[:, None] * k[None, :]) % (2 * length)
    ang = ks.astype(np.float64) * (np.pi / length)
    cmat = np.cos(ang)
    smat = -np.sin(ang)
    smat[0, :] = 1.0 - 2.0 * (k % 2)
    to_bf16 = lambda a: jnp.asarray(a.astype(np.float32)).astype(BF16)
    return to_bf16(cmat), to_bf16(smat), to_bf16(smat.T)


@functools.lru_cache(maxsize=None)
def _filter_features(length):
    t = np.linspace(0.0, 1.0, length, dtype=np.float32)
    w_ang = (2.0 * math.pi * np.arange(length, dtype=np.float32) / length).astype(np.float32)
    f = np.linspace(1e-4, HY_BANDS - 1, HY_BANDS, dtype=np.float32)
    arg = (w_ang[:, None] * f[None, :]).astype(np.float32).astype(np.float64)
    z = np.concatenate([t[:, None].astype(np.float64), np.cos(arg), -np.sin(arg)], axis=-1)
    z = np.pad(z, ((0, 0), (0, 64 - HY_EMB))).astype(np.float32)
    rev = np.concatenate([z[:1], z[:0:-1]], axis=0)
    deltas = np.linspace(math.log(HY_TARGET) / HY_FAST, math.log(HY_TARGET) / HY_SLOW,
                         HY_WIDTH, dtype=np.float32)
    return jnp.asarray(z), jnp.asarray(rev), jnp.asarray(np.abs(deltas)[None, :])


def _hyfilter_kernel(z_ref, zr_ref, dl_ref, w1_ref, b1_ref, w2_ref, b2_ref, w3_ref, fr_ref, o_ref):
    c = pl.program_id(0)
    fr = fr_ref[...]

    def mlp(z, w3):
        hid = jnp.sin(fr[0:1] * (_dot3(z, w1_ref[...]) + b1_ref[...]))
        hid = jnp.sin(fr[1:2] * (_dot3(hid, w2_ref[...]) + b2_ref[...]))
        return _dot3(hid, w3) * jnp.exp(-z[:, 0:1] * dl_ref[...])

    z, zr = z_ref[...], zr_ref[...]
    o_ref[0, :, 0:HY_WIDTH] = mlp(z, w3_ref[:, 0:HY_WIDTH]).astype(BF16)
    bwd = mlp(zr, w3_ref[:, HY_WIDTH:2 * HY_WIDTH])
    row = lax.broadcasted_iota(jnp.int32, bwd.shape, 0) + c * z.shape[0]
    o_ref[0, :, HY_WIDTH:2 * HY_WIDTH] = jnp.where(row == 0, 0.0, bwd).astype(BF16)


def _hyfilter(length, layer, w1p, b1, w2, b2, w3, fr):
    z, zr, dl = _filter_features(length)
    tb = TOKEN_BLOCK
    blk = pl.BlockSpec((tb, 64), lambda c: (c, 0))
    lay = lambda a: _layer_spec(a, layer)
    return pl.pallas_call(
        _hyfilter_kernel,
        grid=(length // tb,),
        in_specs=[blk, blk, pl.BlockSpec(dl.shape, lambda c: (0, 0)),
                  lay(w1p), lay(b1), lay(w2), lay(b2), lay(w3), lay(fr)],
        out_specs=pl.BlockSpec((1, tb, 2 * HY_WIDTH), lambda c: (0, c, 0)),
        out_shape=jax.ShapeDtypeStruct((1, length, 2 * HY_WIDTH), BF16),
        compiler_params=_cparams(1),
        name="hyena_filter",
    )(z, zr, dl, w1p, b1, w2, b2, w3, fr)


def _hyconv_kernel(main_ref, before_ref, after_ref, cw_ref, cb_ref, zb_ref, x0_ref, pad_ref, *, geo):
    i = pl.program_id(0)
    tb = TOKEN_BLOCK
    _fill_padded(pad_ref, before_ref, main_ref, after_ref, geo.is_start(i), geo.is_end(i))
    w = HY_WIDTH
    parts = []
    for p in range(3):
        cols = slice(p * w, (p + 1) * w)
        uc = (pad_ref[HALO - 1:HALO - 1 + tb, cols] * cw_ref[0:1, cols]
              + main_ref[:, cols] * cw_ref[1:2, cols]
              + pad_ref[HALO + 1:HALO + 1 + tb, cols] * cw_ref[2:3, cols]
              + cb_ref[:, cols])
        parts.append(uc)
    x0, x1, v = parts
    zb_ref[...] = (v * x1).astype(BF16)
    x0_ref[...] = x0.astype(BF16)


def _hyconv(geo, layer, u_hy, conv_w, conv_b):
    tb = TOKEN_BLOCK
    w3 = 3 * HY_WIDTH
    before, after = geo.halo_specs(w3)
    row = pl.BlockSpec((tb, HY_WIDTH), lambda i: (i, 0))
    return pl.pallas_call(
        functools.partial(_hyconv_kernel, geo=geo),
        grid=(geo.n // tb,),
        in_specs=[pl.BlockSpec((tb, w3), lambda i: (i, 0)), before, after,
                  _layer_spec(conv_w, layer), _layer_spec(conv_b, layer)],
        out_specs=[row, row],
        out_shape=[jax.ShapeDtypeStruct((geo.n, HY_WIDTH), BF16),
                   jax.ShapeDtypeStruct((geo.n, HY_WIDTH), BF16)],
        scratch_shapes=[pltpu.VMEM((tb + 2 * HALO, w3), F32)],
        compiler_params=_cparams(1),
        name="hyena_conv_gate",
    )(u_hy, u_hy, u_hy, conv_w, conv_b)


def _dft_fwd_kernel(c_ref, s_ref, x_ref, re_ref, im_ref):
    x = x_ref[0]
    re_ref[0] = _dot(c_ref[...], x)
    im_ref[0] = _dot(s_ref[...], x)


def _dft_fwd(length, x, nseq, seq_off):
    cmat, smat, _ = _dft_tables(length)
    n = x.shape[-1]
    tm = min(length, 512)
    tn = 512
    a_spec = pl.BlockSpec((tm, length), lambda b, j, m: (m, 0))
    o_spec = pl.BlockSpec((1, tm, tn), lambda b, j, m: (b, m, j))
    return pl.pallas_call(
        _dft_fwd_kernel,
        grid=(nseq, n // tn, length // tm),
        in_specs=[a_spec, a_spec, pl.BlockSpec((1, length, tn), lambda b, j, m: (b + seq_off, 0, j))],
        out_specs=[o_spec, o_spec],
        out_shape=[jax.ShapeDtypeStruct((nseq, length, n), F32)] * 2,
        compiler_params=_cparams(3),
        name="hyena_dft",
    )(cmat, smat, x)


def _hyprod_kernel(zr_ref, zi_ref, hfr_ref, hgr_ref, hfi_ref, hgi_ref, yr_ref, yi_ref, *, length):
    c = pl.program_id(1)
    shape = zr_ref.shape[1:]
    k = lax.broadcasted_iota(jnp.int32, shape, 0) + c * shape[0]
    sgn = (1 - 2 * (k % 2)).astype(F32)
    hr = hfr_ref[0] + sgn * hgr_ref[0]
    hi = hfi_ref[0] + sgn * hgi_ref[0]
    zr, zi = zr_ref[0], zi_ref[0]
    inv = 1.0 / length
    yr = (zr * hr - zi * hi) * inv
    yi = (zr * hi + zi * hr) * inv
    first = k == 0
    yr_ref[0] = jnp.where(first, zr * hr * (0.5 * inv), yr).astype(BF16)
    yi_ref[0] = jnp.where(first, zi * hi * (0.5 * inv), yi).astype(BF16)


def _hyprod(length, zre, zim, hre, him):
    nb = zre.shape[0]
    tb = TOKEN_BLOCK
    zs = pl.BlockSpec((1, tb, HY_WIDTH), lambda b, c: (b, c, 0))
    hf = pl.BlockSpec((1, tb, HY_WIDTH), lambda b, c: (0, c, 0))
    hg = pl.BlockSpec((1, tb, HY_WIDTH), lambda b, c: (0, c, 1))
    return pl.pallas_call(
        functools.partial(_hyprod_kernel, length=length),
        grid=(nb, length // tb),
        in_specs=[zs, zs, hf, hg, hf, hg],
        out_specs=[zs, zs],
        out_shape=[jax.ShapeDtypeStruct((nb, length, HY_WIDTH), BF16)] * 2,
        compiler_params=_cparams(2),
        name="hyena_spectral_product",
    )(zre, zim, hre, hre, him, him)


def _hyinv_kernel(c_ref, st_ref, yr_ref, yi_ref, z_ref, x0_ref, skip_ref, o_ref):
    y = _dot(c_ref[...], yr_ref[0]) + _dot(st_ref[...], yi_ref[0])
    o_ref[0] = ((y + z_ref[0].astype(F32) * skip_ref[...]) * x0_ref[0].astype(F32)).astype(BF16)


def _hyinv(length, layer, yr, yi, z, x0, skip, seq_off):
    cmat, _, smat_t = _dft_tables(length)
    nseq = yr.shape[0]
    tm = min(length, 512)
    a_spec = pl.BlockSpec((tm, length), lambda b, m: (m, 0))
    y_spec = pl.BlockSpec((1, length, HY_WIDTH), lambda b, m: (b, 0, 0))
    t_spec = pl.BlockSpec((1, tm, HY_WIDTH), lambda b, m: (b + seq_off, m, 0))
    return pl.pallas_call(
        _hyinv_kernel,
        grid=(nseq, length // tm),
        in_specs=[a_spec, a_spec, y_spec, y_spec, t_spec, t_spec, _layer_spec(skip, layer)],
        out_specs=pl.BlockSpec((1, tm, HY_WIDTH), lambda b, m: (b, m, 0)),
        out_shape=jax.ShapeDtypeStruct((nseq, length, HY_WIDTH), BF16),
        compiler_params=_cparams(2),
        name="hyena_idft",
    )(cmat, smat_t, yr, yi, z, x0, skip)


def _hyena_long_conv(length, layer, nseq, seq_off, zb, x0, filt, skip):
    view = lambda a: a.reshape(a.shape[0] // length, length, HY_WIDTH)
    zre, zim = _dft_fwd(length, view(zb), nseq, seq_off)
    hre, him = _dft_fwd(length, filt, 1, 0)
    yr, yi = _hyprod(length, zre, zim, hre, him)
    out = _hyinv(length, layer, yr, yi, view(zb), view(x0), skip, seq_off)
    return out.reshape(nseq * length, HY_WIDTH)


def _merge_kernel(x_ref, mod_ref, g1_ref, a_ref, bc_ref, bl_ref, cc_ref, cl_ref, wg_ref, bg_ref, wbr_ref,
                  wo_ref, g2_ref, wr_ref, br_ref, xo_ref, h2_ref, ti_ref, tp_ref, *, nbc):
    d = D_MODEL
    is_ctx = pl.program_id(0) < nbc
    mod = mod_ref[0]
    shift1, scale1, gate1 = mod[:, 0:d], mod[:, d:2 * d], mod[:, 2 * d:3 * d]
    shift2, scale2 = mod[:, 3 * d:4 * d], mod[:, 4 * d:5 * d]
    x = x_ref[...]
    h = (_rms(x) * g1_ref[...] * (1.0 + scale1) + shift1).astype(BF16)
    branches = (a_ref[...],
                jnp.where(is_ctx, bc_ref[...], bl_ref[...]),
                jnp.where(is_ctx, cc_ref[...], cl_ref[...]))
    merged = None
    for n, br_n in enumerate(branches):
        cols = slice(n * d, (n + 1) * d)
        gate = _sigmoid(_dot(h, wg_ref[:, cols]) + bg_ref[:, cols])
        term = gate * _dot(br_n, wbr_ref[n])
        merged = term if merged is None else merged + term
    x = x + gate1 * _dot(merged.astype(BF16), wo_ref[...])
    xo_ref[...] = x
    h2 = _rms(x) * g2_ref[...] * (1.0 + scale2) + shift2
    h2_ref[...] = h2.astype(BF16)

    logits = _dot3(h2, wr_ref[...]) + br_ref[...]
    ne = logits.shape[-1]
    lane = lax.broadcasted_iota(jnp.int32, logits.shape, 1).astype(F32)
    vals = logits
    top_v, top_i = [], []
    for _ in range(TOP_K):
        m = jnp.max(vals, axis=-1, keepdims=True)
        idx = jnp.min(jnp.where(vals == m, lane, float(ne)), axis=-1, keepdims=True)
        top_v.append(m)
        top_i.append(idx)
        vals = jnp.where(lane == idx, -jnp.inf, vals)
    es = [jnp.exp(v - top_v[0]) for v in top_v]
    den = es[0] + es[1] + es[2] + es[3]
    for kk in range(TOP_K):
        ti_ref[:, kk:kk + 1] = top_i[kk].astype(jnp.int32)
        tp_ref[:, kk:kk + 1] = es[kk] / den


def _merge(geo, layer, x, mods, g1, a_out, b_ctx, b_lat, c_ctx, c_lat, wg_b, bg, wbr_b, wo_b, g2, wr, br):
    d = D_MODEL
    tb = WIDE_BLOCK
    row = lambda w: pl.BlockSpec((tb, w), lambda i: (i, 0))
    ctx, lat = geo.split_specs(tb, DA_WIDTH)
    lay = lambda a: _layer_spec(a, layer, single_buffer=True)
    return pl.pallas_call(
        functools.partial(_merge_kernel, nbc=geo.nc // tb),
        grid=(geo.n // tb,),
        in_specs=[row(d), geo.mod_spec(layer, tb), lay(g1), row(POOL_WIDTH), ctx, lat, ctx, lat,
                  lay(wg_b), lay(bg), lay(wbr_b), lay(wo_b), lay(g2), lay(wr), lay(br)],
        out_specs=[row(d), row(d), row(TOP_K), row(TOP_K)],
        out_shape=[jax.ShapeDtypeStruct((geo.n, d), F32),
                   jax.ShapeDtypeStruct((geo.n, d), BF16),
                   jax.ShapeDtypeStruct((geo.n, TOP_K), jnp.int32),
                   jax.ShapeDtypeStruct((geo.n, TOP_K), F32)],
        compiler_params=_cparams(1),
        name="merge_route",
    )(x, mods, g1, a_out, b_ctx, b_lat, c_ctx, c_lat, wg_b, bg, wbr_b, wo_b, g2, wr, br)


def _ffn_kernel(sw_ref, st_ref, sn_ref, nv_ref, xs_hbm, wgu_ref, bgu_ref, wd_ref, bd_ref, ys_hbm,
                xbuf, ybuf, semx, semy):
    i = pl.program_id(0)
    nv = nv_ref[0]
    slot = i % 2
    t = MOE_TILE
    half = D_FF // 2

    def rows_of(step, k):
        return pl.ds(pl.multiple_of(st_ref[step] * t, t), k * t)

    def x_copy(step, buf, k):
        return pltpu.make_async_copy(xs_hbm.at[rows_of(step, k)], xbuf.at[buf, pl.ds(0, k * t)], semx.at[buf])

    def y_copy(step, buf, k):
        return pltpu.make_async_copy(ybuf.at[buf, pl.ds(0, k * t)], ys_hbm.at[rows_of(step, k)], semy.at[buf])

    def for_size(step, fn):
        n = sn_ref[step]
        for k in range(1, MOE_MAX_SUB + 1):
            @pl.when(n == k)
            def _(k=k):
                fn(k)

    def compute(k):
        m = k * t
        x = xbuf[slot, 0:m, :]
        y = None
        for c in range(2):
            gcols = slice(c * half, (c + 1) * half)
            ucols = slice(D_FF + c * half, D_FF + (c + 1) * half)
            gate = _dot(x, wgu_ref[0, :, gcols].astype(BF16)) + bgu_ref[0, :, gcols]
            up = _dot(x, wgu_ref[0, :, ucols].astype(BF16)) + bgu_ref[0, :, ucols]
            gate = jnp.minimum(gate, SWIGLU_LIMIT)
            up = jnp.clip(up, -SWIGLU_LIMIT, SWIGLU_LIMIT)
            act = ((up + 1.0) * gate * _sigmoid(SWIGLU_ALPHA * gate)).astype(BF16)
            part = _dot(act, wd_ref[0, gcols, :].astype(BF16))
            y = part if y is None else y + part
        ybuf[slot, 0:m, :] = (y + bd_ref[0]).astype(BF16)
        y_copy(i, slot, k).start()

    @pl.when(i < nv)
    def _():
        @pl.when(i == 0)
        def _():
            for_size(0, lambda k: x_copy(0, 0, k).start())

        for_size(i, lambda k: x_copy(i, slot, k).wait())

        @pl.when(i + 1 < nv)
        def _():
            for_size(i + 1, lambda k: x_copy(i + 1, 1 - slot, k).start())

        @pl.when(i >= 2)
        def _():
            for_size(i - 2, lambda k: y_copy(i - 2, slot, k).wait())

        for_size(i, compute)

        @pl.when(i == nv - 1)
        def _():
            for_size(i, lambda k: y_copy(i, slot, k).wait())

            @pl.when(i >= 1)
            def _():
                for_size(i - 1, lambda k: y_copy(i - 1, 1 - slot, k).wait())


def _ffn(xs, step_weight, step_tile, step_sub, n_valid, w_gu, b_gu, w_down, b_down):
    p = xs.shape[0]
    d = D_MODEL
    t = MOE_TILE
    hbm = pl.BlockSpec(memory_space=pl.ANY)
    wmap = lambda i, sw, st, sn, nv: (sw[i], 0, 0)
    grid_spec = pltpu.PrefetchScalarGridSpec(
        num_scalar_prefetch=4,
        grid=(step_weight.shape[0],),
        in_specs=[hbm,
                  pl.BlockSpec((1, d, 2 * D_FF), wmap), pl.BlockSpec((1, 1, 2 * D_FF), wmap),
                  pl.BlockSpec((1, D_FF, d), wmap), pl.BlockSpec((1, 1, d), wmap)],
        out_specs=hbm,
        scratch_shapes=[pltpu.VMEM((2, MOE_MAX_SUB * t, d), BF16), pltpu.VMEM((2, MOE_MAX_SUB * t, d), BF16),
                        pltpu.SemaphoreType.DMA((2,)), pltpu.SemaphoreType.DMA((2,))])
    return pl.pallas_call(
        _ffn_kernel,
        grid_spec=grid_spec,
        out_shape=jax.ShapeDtypeStruct((p, d), BF16),
        compiler_params=_cparams(1),
        name="moe_experts",
    )(step_weight, step_tile, step_sub, n_valid, xs, w_gu, b_gu, w_down, b_down)


def _moe_rows(n, n_experts):
    worst = n * TOP_K + n_experts * ((n // DISPATCH_BLOCK) * (RUN_ALIGN - 1) + MOE_TILE - 1)
    return -(-worst // MOE_TILE) * MOE_TILE


def _route(top_i, n_experts):
    n = top_i.shape[0]
    t = MOE_TILE
    nb = n // DISPATCH_BLOCK
    experts = jnp.arange(n_experts, dtype=jnp.int32)
    onehot = top_i.reshape(nb, DISPATCH_BLOCK * TOP_K)[:, :, None] == experts[None, None, :]
    n_be = jnp.sum(onehot, axis=1, dtype=jnp.int32)
    cap = (n_be + RUN_ALIGN - 1) // RUN_ALIGN * RUN_ALIGN
    rows_e = jnp.sum(cap, axis=0)
    tiles_e = (rows_e + t - 1) // t
    tile_end = jnp.cumsum(tiles_e)
    tile_start = tile_end - tiles_e
    run_dst = tile_start[None, :] * t + jnp.cumsum(cap, axis=0) - cap
    run_loc = jnp.cumsum(cap, axis=1) - cap
    tail_dst = tile_start * t + rows_e
    tail = tiles_e * t - rows_e

    def owner(ends, ids):
        return jnp.sum(ends[None, :] <= ids[:, None], axis=1, dtype=jnp.int32)[:, None] == experts[None, :]

    pick = lambda oh, table: jnp.sum(jnp.where(oh, table[None, :], 0), axis=1)

    steps_e = (tiles_e + MOE_MAX_SUB - 1) // MOE_MAX_SUB
    step_end = jnp.cumsum(steps_e)
    n_steps = _moe_rows(n, n_experts) // (t * MOE_MAX_SUB) + n_experts
    n_valid = step_end[-1]
    step_ids = jnp.minimum(jnp.arange(n_steps, dtype=jnp.int32), n_valid - 1)
    step_oh = owner(step_end, step_ids)
    j = step_ids - pick(step_oh, step_end - steps_e)
    step_expert = jnp.argmax(step_oh, axis=1).astype(jnp.int32)
    n_st = jnp.maximum(pick(step_oh, steps_e), 1)
    n_ti = pick(step_oh, tiles_e)
    base, rem = n_ti // n_st, n_ti % n_st
    extra = jnp.maximum(j - (n_st - rem), 0)
    step_tile = pick(step_oh, tile_start) + j * base + extra
    step_sub = jnp.clip(base + (j >= n_st - rem).astype(jnp.int32), 1, MOE_MAX_SUB)
    units = lambda a: (a // RUN_ALIGN).reshape(-1).astype(jnp.int32)
    runs = (units(run_dst), units(run_loc), units(cap), units(jnp.sum(cap, axis=1)), units(tail_dst), units(tail))
    cols = (run_dst.reshape(nb, n_experts, 1).astype(F32), run_loc.reshape(nb, n_experts, 1).astype(F32))
    steps = (step_expert, step_tile.astype(jnp.int32), step_sub.astype(jnp.int32),
             n_valid.reshape(1).astype(jnp.int32))
    return runs, cols, steps


def _sized_copies(units, src_at, dst_at, sem, max_bits, fn):
    for bit in range(max_bits):
        size = RUN_ALIGN << bit

        @pl.when((units >> bit) & 1 == 1)
        def _(bit=bit, size=size):
            off = (units & ((1 << bit) - 1)) * RUN_ALIGN
            fn(pltpu.make_async_copy(src_at(off, size), dst_at(off, size), sem))


def _dispatch_kernel(rdst_ref, rloc_ref, rcap_ref, rtot_ref, tdst_ref, tlen_ref,
                     h_ref, tt_ref, dcol_ref, lcol_ref, xs_hbm, slot_ref, stage, zeros, sem, zsem,
                     *, n_experts):
    b = pl.program_id(0)
    nb = pl.num_programs(0)
    tb = DISPATCH_BLOCK
    buf = b % 2
    run_bits = (tb // RUN_ALIGN).bit_length()
    tail_bits = (MOE_TILE // RUN_ALIGN).bit_length()
    stage_bits = (stage.shape[1] // RUN_ALIGN).bit_length()

    def drain_runs(blk, slot):
        _sized_copies(rtot_ref[blk],
                      lambda off, size: stage.at[slot, pl.ds(0, size)],
                      lambda off, size: xs_hbm.at[pl.ds(0, size)],
                      sem.at[slot], stage_bits, lambda cp: cp.wait())

    def run_copies(blk, slot, fn):
        def body(e, carry):
            idx = blk * n_experts + e
            src0 = pl.multiple_of(rloc_ref[idx] * RUN_ALIGN, RUN_ALIGN)
            dst0 = pl.multiple_of(rdst_ref[idx] * RUN_ALIGN, RUN_ALIGN)
            _sized_copies(rcap_ref[idx],
                          lambda off, size: stage.at[slot, pl.ds(pl.multiple_of(src0 + off, RUN_ALIGN), size)],
                          lambda off, size: xs_hbm.at[pl.ds(pl.multiple_of(dst0 + off, RUN_ALIGN), size)],
                          sem.at[slot], run_bits, fn)
            return carry
        lax.fori_loop(0, n_experts, body, 0)

    def tail_copies(fn):
        def body(e, carry):
            dst0 = pl.multiple_of(tdst_ref[e] * RUN_ALIGN, RUN_ALIGN)
            _sized_copies(tlen_ref[e],
                          lambda off, size: zeros.at[pl.ds(0, size)],
                          lambda off, size: xs_hbm.at[pl.ds(pl.multiple_of(dst0 + off, RUN_ALIGN), size)],
                          zsem, tail_bits, fn)
            return carry
        lax.fori_loop(0, n_experts, body, 0)

    @pl.when(b == 0)
    def _():
        zeros[...] = jnp.zeros(zeros.shape, zeros.dtype)
        tail_copies(lambda cp: cp.start())

    @pl.when(b >= 2)
    def _():
        drain_runs(b - 2, buf)

    tt = tt_ref[...]
    e_iota = lax.broadcasted_iota(jnp.int32, (n_experts, tb), 0)
    hit = [e_iota == tt[k:k + 1, :] for k in range(TOP_K)]
    member = jnp.where(hit[0] | hit[1] | hit[2] | hit[3], 1.0, 0.0)
    before = (lax.broadcasted_iota(jnp.int32, (tb, tb), 0)
              < lax.broadcasted_iota(jnp.int32, (tb, tb), 1))
    rank = _dot(member.astype(BF16), jnp.where(before, 1.0, 0.0).astype(BF16))
    loc = [jnp.sum(jnp.where(hit[k], lcol_ref[0] + rank, 0.0), axis=0, keepdims=True) for k in range(TOP_K)]
    for k in range(TOP_K):
        slot_ref[k:k + 1, :] = jnp.sum(jnp.where(hit[k], dcol_ref[0] + rank, 0.0),
                                       axis=0, keepdims=True).astype(jnp.int32)

    rows = stage.shape[1]
    r_iota = lax.broadcasted_iota(jnp.int32, (rows, tb), 0).astype(F32)
    place = jnp.zeros((rows, tb), F32)
    for k in range(TOP_K):
        place = jnp.where(r_iota == loc[k], 1.0, place)
    stage[buf] = _dot(place.astype(BF16), h_ref[...]).astype(BF16)
    run_copies(b, buf, lambda cp: cp.start())

    @pl.when(b == nb - 1)
    def _():
        drain_runs(b, buf)

        @pl.when(b >= 1)
        def _():
            drain_runs(b - 1, 1 - buf)
        tail_copies(lambda cp: cp.wait())


def _dispatch(h2, top_t, runs, cols, n_experts):
    n, d = h2.shape
    tb = DISPATCH_BLOCK
    stage_rows = tb * TOP_K + n_experts * RUN_ALIGN
    col = pl.BlockSpec((1, n_experts, 1), lambda b, *_: (b, 0, 0))
    grid_spec = pltpu.PrefetchScalarGridSpec(
        num_scalar_prefetch=6,
        grid=(n // tb,),
        in_specs=[pl.BlockSpec((tb, d), lambda b, *_: (b, 0)),
                  pl.BlockSpec((TOP_K, tb), lambda b, *_: (0, b)), col, col],
        out_specs=[pl.BlockSpec(memory_space=pl.ANY), pl.BlockSpec((TOP_K, tb), lambda b, *_: (0, b))],
        scratch_shapes=[pltpu.VMEM((2, stage_rows, d), BF16), pltpu.VMEM((MOE_TILE, d), BF16),
                        pltpu.SemaphoreType.DMA((2,)), pltpu.SemaphoreType.DMA(())])
    return pl.pallas_call(
        functools.partial(_dispatch_kernel, n_experts=n_experts),
        grid_spec=grid_spec,
        out_shape=[jax.ShapeDtypeStruct((_moe_rows(n, n_experts), d), BF16),
                   jax.ShapeDtypeStruct((TOP_K, n), jnp.int32)],
        compiler_params=_cparams(1),
        name="moe_dispatch",
    )(*runs, h2, top_t, *cols)


def _moe_sum(x_ref, mod_ref, yg_ref, p_ref):
    d = D_MODEL
    gate2 = mod_ref[0][:, 5 * d:6 * d]
    p = p_ref[...]
    moe = p[:, 0:1] * yg_ref[0].astype(F32)
    for kk in range(1, TOP_K):
        moe = moe + p[:, kk:kk + 1] * yg_ref[kk].astype(F32)
    return x_ref[...] + gate2 * moe


def _combine_final_kernel(x_ref, mod_ref, yg_ref, p_ref, gf_ref, yc_ref, yl_ref, *, nbc):
    i = pl.program_id(0)
    y = _rms(_moe_sum(x_ref, mod_ref, yg_ref, p_ref)) * gf_ref[...]

    @pl.when(i < nbc)
    def _():
        yc_ref[...] = y

    @pl.when(i >= nbc)
    def _():
        yl_ref[...] = y


def _combine(geo, layer, x, mods, yg, top_p, final_g):
    d = D_MODEL
    tb = TOKEN_BLOCK
    row = pl.BlockSpec((tb, d), lambda i: (i, 0))
    in_specs = [row, geo.mod_spec(layer, tb),
                pl.BlockSpec((TOP_K, tb, d), lambda i: (0, i, 0)),
                pl.BlockSpec((tb, TOP_K), lambda i: (i, 0))]
    ctx, lat = geo.split_specs(tb, d)
    return pl.pallas_call(
        functools.partial(_combine_final_kernel, nbc=geo.nc // tb),
        grid=(geo.n // tb,),
        in_specs=in_specs + [pl.BlockSpec(final_g.shape, lambda i: (0, 0))],
        out_specs=[ctx, lat],
        out_shape=[jax.ShapeDtypeStruct((geo.nc, d), F32), jax.ShapeDtypeStruct((geo.nl, d), F32)],
        compiler_params=_cparams(1), name="moe_combine_final",
    )(x, mods, yg, top_p, final_g)


@functools.lru_cache(maxsize=None)
def _rope_tables(length):
    rows = length // GRID_W
    row = np.repeat(np.arange(rows), GRID_W).astype(np.float32)
    col = np.tile(np.arange(GRID_W), rows).astype(np.float32)
    ax = DA_HEAD_DIM // 2
    inv = (ROPE_BASE ** (-(np.arange(ax // 2, dtype=np.float32) * 2.0 / ax))).astype(np.float32)
    ang_r = (row[:, None] * inv).astype(np.float32)
    ang_c = (col[:, None] * inv).astype(np.float32)
    ang = np.concatenate([ang_r, ang_r, ang_c, ang_c], axis=-1).astype(np.float64)
    sign = np.where((np.arange(DA_HEAD_DIM) % 32) < 16, -1.0, 1.0)
    reps = DA_WIDTH // DA_HEAD_DIM
    cos = np.tile(np.cos(ang), (1, reps)).astype(np.float32)
    sin_signed = np.tile(np.sin(ang) * sign[None, :], (1, reps)).astype(np.float32)
    return jnp.asarray(cos), jnp.asarray(sin_signed)


def kernel(x_prompt, x_sample, cache_k, cache_v, c, c_ctx, w_ada, b_ada, norm1, norm2, w_in, w_pool, pool_scale, da_lambda, da_subln, hy_conv_w, hy_conv_b, hy_f_w1, hy_f_b1, hy_f_w2, hy_f_b2, hy_f_w3, hy_sin_freq, hy_skip, w_branch, w_gate, b_gate, w_o, w_router, b_router, w_gu, b_gu, w_down, b_down, final_norm):
    bc, lc, d = x_prompt.shape
    bl, ll, _ = x_sample.shape
    depth = w_in.shape[0]
    n_experts = w_router.shape[-1]
    past = cache_k.shape[2]
    geo = _Geom(bc, lc, bl, ll)
    assert 1 + bl <= MOD_ROWS

    x = jnp.concatenate([x_prompt.reshape(bc * lc, d), x_sample.reshape(bl * ll, d)], axis=0)
    cond = jnp.concatenate([c_ctx[None], c, jnp.zeros((MOD_ROWS - 1 - bl, d), F32)], axis=0)
    mods = _ada(cond, w_ada, b_ada).reshape(depth * MOD_ROWS, 1, N_MOD * d)
    cos_t, sin_t = _rope_tables(ll)
    ck = cache_k.reshape(bl, depth, past, DA_WIDTH)
    cv = cache_v.reshape(bl, depth, past, DA_WIDTH)

    row3 = lambda a: a.reshape(depth, 1, a.shape[-1])
    g1, g2 = row3(norm1), row3(norm2)
    w_in_b = w_in.astype(BF16)
    w_pool_b = w_pool.astype(BF16)
    w_gate_b = w_gate.astype(BF16)
    w_branch_b = w_branch.astype(BF16)
    w_o_b = w_o.astype(BF16)
    w1p = jnp.pad(hy_f_w1, ((0, 0), (0, 64 - HY_EMB), (0, 0)))
    fargs = (w1p, row3(hy_f_b1), hy_f_w2, row3(hy_f_b2), hy_f_w3, hy_sin_freq)
    w_gu_s = w_gu.reshape(depth * n_experts, d, 2 * D_FF)
    b_gu_s = b_gu.reshape(depth * n_experts, 1, 2 * D_FF)
    w_down_s = w_down.reshape(depth * n_experts, D_FF, d)
    b_down_s = b_down.reshape(depth * n_experts, 1, d)

    new_k = jnp.zeros((bc, depth, lc * DA_HEADS, DA_VDIM), F32)
    new_v = jnp.zeros((bc, depth, lc * DA_HEADS, DA_VDIM), F32)
    moe = None
    for l in range(depth):
        x, u_pool, q, kb, vb, new_k, new_v, u_hy = _inproj(geo, l, x, mods, g1, w_in_b, cos_t, sin_t,
                                                           new_k, new_v, moe)
        a_out = _pool(geo, l, u_pool, w_pool_b, row3(pool_scale))

        lam_init = 0.8 - 0.6 * math.exp(-0.3 * l)
        subln = row3(da_subln)
        b_ctx = _attn_ctx(geo, l, q, kb, vb, da_lambda, subln, lam_init)
        b_lat = _attn_lat(geo, l, q, kb, vb, ck, cv, da_lambda, subln, lam_init)

        zb, x0 = _hyconv(geo, l, u_hy, hy_conv_w, row3(hy_conv_b))
        skip = row3(hy_skip)
        c_ctx_out = _hyena_long_conv(lc, l, bc, 0, zb, x0, _hyfilter(lc, l, *fargs), skip)
        c_lat_out = _hyena_long_conv(ll, l, bl, geo.nc // ll, zb, x0, _hyfilter(ll, l, *fargs), skip)

        x, h2, top_i, top_p = _merge(geo, l, x, mods, g1, a_out, b_ctx, b_lat, c_ctx_out, c_lat_out,
                                     w_gate_b, row3(b_gate), w_branch_b, w_o_b, g2, w_router, row3(b_router))

        runs, cols, (step_expert, step_tile, step_sub, n_valid) = _route(top_i, n_experts)
        xs, pair_slot = _dispatch(h2, top_i.T, runs, cols, n_experts)
        ys = _ffn(xs, step_expert + l * n_experts, step_tile, step_sub, n_valid,
                  w_gu_s, b_gu_s, w_down_s, b_down_s)
        yg = jnp.take(ys, pair_slot, axis=0, mode="clip")
        moe = (yg, top_p)

    y_ctx, y_lat = _combine(geo, depth - 1, x, mods, yg, top_p, final_norm[None])

    return (y_ctx.reshape(bc, lc, d), y_lat.reshape(bl, ll, d),
            new_k.reshape(bc, depth, lc, DA_HEADS, 2 * DA_HEAD_DIM),
            new_v.reshape(bc, depth, lc, DA_HEADS, DA_VDIM))
```

```python
import functools
import math

import numpy as np
import jax
import jax.numpy as jnp
from jax import lax
from jax.experimental import pallas as pl
from jax.experimental.pallas import tpu as pltpu

F32 = jnp.float32
BF16 = jnp.bfloat16

D_MODEL = 1024
GRID_W = 64
NORM_EPS = 1e-6
POOL_WIDTH = 512
POOL_WINDOWS = (2, 4, 8, 16)
POOL_GC = POOL_WIDTH // len(POOL_WINDOWS)
DA_HEADS = 4
DA_HEAD_DIM = 64
DA_VDIM = 2 * DA_HEAD_DIM
DA_WIDTH = DA_HEADS * DA_VDIM
ROPE_BASE = 10000.0
HY_WIDTH = 512
HY_EMB = 33
HY_BANDS = (HY_EMB - 1) // 2
HY_HIDDEN = 64
HY_FAST = 0.3
HY_SLOW = 1.5
HY_TARGET = 1e-2
N_BRANCH = 3
D_IN = POOL_WIDTH + 3 * DA_WIDTH + 3 * HY_WIDTH
TOP_K = 4
D_FF = 1024
SWIGLU_ALPHA = 1.702
SWIGLU_LIMIT = 7.0
N_MOD = 6
MOD_ROWS = 8

TOKEN_BLOCK = 256
WIDE_BLOCK = 512
HALO = 8
MOE_TILE = 128
MOE_MAX_SUB = 8
DISPATCH_BLOCK = 512
RUN_ALIGN = 16
V7X_VMEM_LIMIT = 56 * 1024 * 1024


def _cparams(n_axes):
    return pltpu.CompilerParams(
        dimension_semantics=("arbitrary",) * n_axes,
        vmem_limit_bytes=V7X_VMEM_LIMIT)


def _dot(a, b):
    return jnp.dot(a, b, preferred_element_type=F32)


def _dot_nt(a, b):
    return lax.dot_general(a, b, (((1,), (1,)), ((), ())), preferred_element_type=F32)


def _split_bf16(a):
    hi = a.astype(BF16)
    lo = (a - hi.astype(F32)).astype(BF16)
    return hi, lo


def _dot3(a, b):
    ah, al = _split_bf16(a)
    bh, bl = _split_bf16(b)
    return _dot(ah, bh) + _dot(al, bh) + _dot(ah, bl)


def _sigmoid(x):
    return 1.0 / (1.0 + jnp.exp(-x))


def _rms(x):
    return x * lax.rsqrt(jnp.mean(x * x, axis=-1, keepdims=True) + NORM_EPS)


class _Geom:
    def __init__(self, n_ctx_seq, ctx_len, n_lat_seq, lat_len):
        assert ctx_len == TOKEN_BLOCK, "one context sequence per token block"
        assert lat_len % WIDE_BLOCK == 0 and (n_ctx_seq * ctx_len) % lat_len == 0
        self.bc, self.lc, self.bl, self.ll = n_ctx_seq, ctx_len, n_lat_seq, lat_len
        self.nc = n_ctx_seq * ctx_len
        self.nl = n_lat_seq * lat_len
        self.n = self.nc + self.nl

    def group(self, i, tb):
        nbc = self.nc // tb
        return jnp.where(i < nbc, 0, 1 + (i - nbc) // (self.ll // tb))

    def pos_block(self, i, tb):
        nbc = self.nc // tb
        return jnp.where(i < nbc, 0, (i - nbc) % (self.ll // tb))

    def is_start(self, i):
        nbc, bpl = self.nc // TOKEN_BLOCK, self.ll // TOKEN_BLOCK
        return jnp.logical_or(i < nbc, (i - nbc) % bpl == 0)

    def is_end(self, i):
        nbc, bpl = self.nc // TOKEN_BLOCK, self.ll // TOKEN_BLOCK
        return jnp.logical_or(i < nbc, (i - nbc) % bpl == bpl - 1)

    def mod_spec(self, layer, tb):
        return pl.BlockSpec((1, 1, N_MOD * D_MODEL),
                            lambda i: (layer * MOD_ROWS + self.group(i, tb), 0, 0))

    def halo_specs(self, width):
        per = TOKEN_BLOCK // HALO
        last = self.n // HALO - 1
        before = pl.BlockSpec((HALO, width), lambda i: (jnp.maximum(i * per - 1, 0), 0))
        after = pl.BlockSpec((HALO, width), lambda i: (jnp.minimum((i + 1) * per, last), 0))
        return before, after

    def split_specs(self, tb, width):
        nbc = self.nc // tb
        last_lat = self.nl // tb - 1
        ctx = pl.BlockSpec((tb, width), lambda i: (jnp.minimum(i, nbc - 1), 0))
        lat = pl.BlockSpec((tb, width), lambda i: (jnp.clip(i - nbc, 0, last_lat), 0))
        return ctx, lat


def _layer_spec(a, layer, single_buffer=False):
    kw = dict(pipeline_mode=pl.Buffered(1)) if single_buffer else {}
    return pl.BlockSpec((None,) + a.shape[1:], lambda *_: (layer,) + (0,) * (a.ndim - 1), **kw)


def _ada_kernel(c_ref, w_ref, b_ref, o_ref):
    c = c_ref[...]
    s = c * _sigmoid(c)
    o_ref[0] = _dot(s.astype(BF16), w_ref[0].astype(BF16)) + b_ref[0]


def _ada(cond, w_ada, b_ada):
    depth, d, n6 = w_ada.shape
    rows = cond.shape[0]
    tn = 1024
    return pl.pallas_call(
        _ada_kernel,
        grid=(depth, n6 // tn),
        in_specs=[pl.BlockSpec((rows, d), lambda l, j: (0, 0)),
                  pl.BlockSpec((1, d, tn), lambda l, j: (l, 0, j)),
                  pl.BlockSpec((1, 1, tn), lambda l, j: (l, 0, j))],
        out_specs=pl.BlockSpec((1, rows, tn), lambda l, j: (l, 0, j)),
        out_shape=jax.ShapeDtypeStruct((depth, rows, n6), F32),
        compiler_params=_cparams(2),
        name="ada_mod",
    )(cond, w_ada, b_ada.reshape(depth, 1, n6))


def _rope(x, cos, sin_signed, first_half):
    d = x.shape[-1]
    partner = jnp.where(first_half, pltpu.roll(x, d - 16, 1), pltpu.roll(x, 16, 1))
    return x * cos + partner * sin_signed


def _inproj_kernel(*refs, nbc, lc, add_moe):
    if add_moe:
        x_ref, pmod_ref, yg_ref, p_ref = refs[:4]
        refs = refs[4:]
        (mod_ref, g_ref, w_ref, cos_ref, sin_ref, _, _,
         xo_ref, up_ref, q_ref, kb_ref, vb_ref, kf_ref, vf_ref, uh_ref) = refs
        x = _moe_sum(x_ref, pmod_ref, yg_ref, p_ref)
        xo_ref[...] = x
    else:
        (x_ref, mod_ref, g_ref, w_ref, cos_ref, sin_ref, _, _,
         up_ref, q_ref, kb_ref, vb_ref, kf_ref, vf_ref, uh_ref) = refs
        x = x_ref[...]
    i = pl.program_id(0)
    d = D_MODEL
    mod = mod_ref[0]
    shift, scale = mod[:, 0:d], mod[:, d:2 * d]
    h = (_rms(x) * g_ref[...] * (1.0 + scale) + shift).astype(BF16)

    c1 = POOL_WIDTH
    c2 = c1 + DA_WIDTH
    c3 = c2 + DA_WIDTH
    c4 = c3 + DA_WIDTH
    up_ref[...] = _dot(h, w_ref[:, 0:c1])
    uh_ref[...] = _dot(h, w_ref[:, c4:D_IN])
    q = _dot(h, w_ref[:, c1:c2]) * (DA_HEAD_DIM ** -0.5)
    k = _dot(h, w_ref[:, c2:c3])
    v = _dot(h, w_ref[:, c3:c4])
    vb_ref[...] = v.astype(BF16)

    @pl.when(i < nbc)
    def _():
        q_ref[...] = q.astype(BF16)
        kb_ref[...] = k.astype(BF16)
        for s in range(k.shape[0] // lc):
            for hh in range(DA_HEADS):
                rows = pl.ds(hh, lc, stride=DA_HEADS)
                cols = slice(hh * DA_VDIM, (hh + 1) * DA_VDIM)
                kf_ref[s, 0, rows, :] = k[s * lc:(s + 1) * lc, cols]
                vf_ref[s, 0, rows, :] = v[s * lc:(s + 1) * lc, cols]

    @pl.when(i >= nbc)
    def _():
        cos, sin_signed = cos_ref[...], sin_ref[...]
        lane = lax.broadcasted_iota(jnp.int32, q.shape, 1)
        first_half = (lane % 32) < 16
        q_ref[...] = _rope(q, cos, sin_signed, first_half).astype(BF16)
        kb_ref[...] = _rope(k, cos, sin_signed, first_half).astype(BF16)


def _inproj(geo, layer, x, mods, g1, w_in_b, cos_t, sin_t, kacc, vacc, moe=None):
    d = D_MODEL
    tb = WIDE_BLOCK
    nbc = geo.nc // tb
    spb = tb // geo.lc
    row = lambda w: pl.BlockSpec((tb, w), lambda i: (i, 0))
    cache = pl.BlockSpec((spb, 1, geo.lc * DA_HEADS, DA_VDIM), lambda i: (jnp.minimum(i, nbc - 1), layer, 0, 0))
    tab = pl.BlockSpec((tb, DA_WIDTH), lambda i: (geo.pos_block(i, tb), 0))
    hbm = pl.BlockSpec(memory_space=pl.ANY)
    in_specs = [geo.mod_spec(layer, tb), _layer_spec(g1, layer),
                _layer_spec(w_in_b, layer, single_buffer=True), tab, tab, hbm, hbm]
    args = [mods, g1, w_in_b, cos_t, sin_t, kacc, vacc]
    out_specs = [row(POOL_WIDTH), row(DA_WIDTH), row(DA_WIDTH), row(DA_WIDTH), cache, cache, row(3 * HY_WIDTH)]
    out_shape = [jax.ShapeDtypeStruct((geo.n, POOL_WIDTH), F32),
                 jax.ShapeDtypeStruct((geo.n, DA_WIDTH), BF16),
                 jax.ShapeDtypeStruct((geo.n, DA_WIDTH), BF16),
                 jax.ShapeDtypeStruct((geo.n, DA_WIDTH), BF16),
                 jax.ShapeDtypeStruct(kacc.shape, F32),
                 jax.ShapeDtypeStruct(vacc.shape, F32),
                 jax.ShapeDtypeStruct((geo.n, 3 * HY_WIDTH), F32)]
    if moe is None:
        in_specs = [row(d)] + in_specs
        args = [x] + args
    else:
        yg, top_p = moe
        in_specs = [row(d), geo.mod_spec(layer - 1, tb),
                    pl.BlockSpec((TOP_K, tb, d), lambda i: (0, i, 0)),
                    pl.BlockSpec((tb, TOP_K), lambda i: (i, 0))] + in_specs
        args = [x, mods, yg, top_p] + args
        out_specs = [row(d)] + out_specs
        out_shape = [jax.ShapeDtypeStruct((geo.n, d), F32)] + out_shape
    n_in, n_out = len(args), len(out_shape)
    outs = pl.pallas_call(
        functools.partial(_inproj_kernel, nbc=nbc, lc=geo.lc, add_moe=moe is not None),
        grid=(geo.n // tb,),
        in_specs=in_specs, out_specs=out_specs, out_shape=out_shape,
        input_output_aliases={n_in - 2: n_out - 3, n_in - 1: n_out - 2},
        compiler_params=_cparams(1),
        name="in_proj",
    )(*args)
    return outs if moe is not None else [x] + list(outs)


def _fill_padded(pad_ref, before_ref, main_ref, after_ref, start, end):
    tb = TOKEN_BLOCK
    zero = jnp.zeros(before_ref.shape, F32)
    pad_ref[0:HALO, :] = jnp.where(start, zero, before_ref[...])
    pad_ref[HALO:HALO + tb, :] = main_ref[...]
    pad_ref[HALO + tb:2 * HALO + tb, :] = jnp.where(end, zero, after_ref[...])


def _pool_kernel(main_ref, before_ref, after_ref, w_ref, s_ref, o_ref, pad_ref, *, geo):
    i = pl.program_id(0)
    tb = TOKEN_BLOCK
    start, end = geo.is_start(i), geo.is_end(i)
    _fill_padded(pad_ref, before_ref, main_ref, after_ref, start, end)
    r = lax.broadcasted_iota(jnp.int32, (tb, 1), 0)
    for g, w in enumerate(POOL_WINDOWS):
        cols = slice(g * POOL_GC, (g + 1) * POOL_GC)
        acc = pad_ref[HALO - w // 2:HALO - w // 2 + tb, cols]
        for j in range(-w // 2 + 1, w // 2):
            acc = acc + pad_ref[HALO + j:HALO + j + tb, cols]
        lo = jnp.where(start, jnp.maximum(r - w // 2, 0), r - w // 2)
        hi = jnp.where(end, jnp.minimum(r + w // 2, tb), r + w // 2)
        mean = acc / (hi - lo).astype(F32)
        dlt = mean - main_ref[:, cols]
        y = _dot(dlt.astype(BF16), w_ref[g])
        o_ref[:, cols] = (y * s_ref[:, cols]).astype(BF16)


def _pool(geo, layer, u_pool, w_pool_b, pool_scale):
    tb = TOKEN_BLOCK
    before, after = geo.halo_specs(POOL_WIDTH)
    return pl.pallas_call(
        functools.partial(_pool_kernel, geo=geo),
        grid=(geo.n // tb,),
        in_specs=[pl.BlockSpec((tb, POOL_WIDTH), lambda i: (i, 0)), before, after,
                  _layer_spec(w_pool_b, layer), _layer_spec(pool_scale, layer)],
        out_specs=pl.BlockSpec((tb, POOL_WIDTH), lambda i: (i, 0)),
        out_shape=jax.ShapeDtypeStruct((geo.n, POOL_WIDTH), BF16),
        scratch_shapes=[pltpu.VMEM((tb + 2 * HALO, POOL_WIDTH), F32)],
        compiler_params=_cparams(1),
        name="pool_branch",
    )(u_pool, u_pool, u_pool, w_pool_b, pool_scale)


def _attn_kernel(*refs, lam_init, has_ctx):
    if has_ctx:
        q_ref, k_ref, v_ref, kc_ref, vc_ref, lam_ref, g_ref, o_ref = refs
    else:
        q_ref, k_ref, v_ref, lam_ref, g_ref, o_ref = refs
    lp = lam_ref[...]
    lam = (jnp.exp(jnp.sum(lp[0:1] * lp[1:2], axis=-1, keepdims=True))
           - jnp.exp(jnp.sum(lp[2:3] * lp[3:4], axis=-1, keepdims=True)) + lam_init)
    for h in range(DA_HEADS):
        vcols = slice(h * DA_VDIM, (h + 1) * DA_VDIM)
        v = v_ref[0, :, vcols]
        if has_ctx:
            vc = vc_ref[0, 0, :, vcols].astype(BF16)
        outs = []
        for sub in range(2):
            c0 = h * DA_VDIM + sub * DA_HEAD_DIM
            cols = slice(c0, c0 + DA_HEAD_DIM)
            qs = q_ref[0, :, cols]
            s1 = _dot_nt(qs, k_ref[0, :, cols])
            m = jnp.max(s1, axis=-1, keepdims=True)
            if has_ctx:
                s2 = _dot_nt(qs, kc_ref[0, 0, :, cols].astype(BF16))
                m = jnp.maximum(m, jnp.max(s2, axis=-1, keepdims=True))
            e1 = jnp.exp(s1 - m)
            den = jnp.sum(e1, axis=-1, keepdims=True)
            pv = _dot(e1.astype(BF16), v)
            if has_ctx:
                e2 = jnp.exp(s2 - m)
                den = den + jnp.sum(e2, axis=-1, keepdims=True)
                pv = pv + _dot(e2.astype(BF16), vc)
            outs.append(pv / den)
        o = outs[0] - lam * outs[1]
        o = _rms(o) * g_ref[...] * (1.0 - lam_init)
        o_ref[0, :, vcols] = o.astype(BF16)


def _attn_ctx(geo, layer, q, kb, vb, lam_p, subln, lam_init):
    bc, lc = geo.bc, geo.lc
    seq = pl.BlockSpec((1, lc, DA_WIDTH), lambda b: (b, 0, 0))
    view = lambda a: a.reshape(geo.n // lc, lc, DA_WIDTH)
    out = pl.pallas_call(
        functools.partial(_attn_kernel, lam_init=lam_init, has_ctx=False),
        grid=(bc,),
        in_specs=[seq, seq, seq, _layer_spec(lam_p, layer), _layer_spec(subln, layer)],
        out_specs=seq,
        out_shape=jax.ShapeDtypeStruct((bc, lc, DA_WIDTH), BF16),
        compiler_params=_cparams(1),
        name="attn_ctx",
    )(view(q), view(kb), view(vb), lam_p, subln)
    return out.reshape(bc * lc, DA_WIDTH)


def _attn_lat(geo, layer, q, kb, vb, cache_k, cache_v, lam_p, subln, lam_init):
    bl, ll = geo.bl, geo.ll
    tq = TOKEN_BLOCK
    past = cache_k.shape[2]
    off = geo.nc // ll
    qblk = pl.BlockSpec((1, tq, DA_WIDTH), lambda b, j: (b + off, j, 0))
    seq = pl.BlockSpec((1, ll, DA_WIDTH), lambda b, j: (b + off, 0, 0))
    cache = pl.BlockSpec((1, 1, past, DA_WIDTH), lambda b, j: (b, layer, 0, 0))
    view = lambda a: a.reshape(geo.n // ll, ll, DA_WIDTH)
    out = pl.pallas_call(
        functools.partial(_attn_kernel, lam_init=lam_init, has_ctx=True),
        grid=(bl, ll // tq),
        in_specs=[qblk, seq, seq, cache, cache, _layer_spec(lam_p, layer), _layer_spec(subln, layer)],
        out_specs=pl.BlockSpec((1, tq, DA_WIDTH), lambda b, j: (b, j, 0)),
        out_shape=jax.ShapeDtypeStruct((bl, ll, DA_WIDTH), BF16),
        compiler_params=_cparams(2),
        name="attn_lat",
    )(view(q), view(kb), view(vb), cache_k, cache_v, lam_p, subln)
    return out.reshape(bl * ll, DA_WIDTH)


@functools.lru_cache(maxsize=None)
def _dft_tables(length):
    k = np.arange(length, dtype=np.int64)
    ks = (k[:, None] * k[None, :]) % (2 * length)
    ang = ks.astype(np.float64) * (np.pi / length)
    cmat = np.cos(ang)
    smat = -np.sin(ang)
    smat[0, :] = 1.0 - 2.0 * (k % 2)
    to_bf16 = lambda a: jnp.asarray(a.astype(np.float32)).astype(BF16)
    return to_bf16(cmat), to_bf16(smat), to_bf16(smat.T)


@functools.lru_cache(maxsize=None)
def _filter_features(length):
    t = np.linspace(0.0, 1.0, length, dtype=np.float32)
    w_ang = (2.0 * math.pi * np.arange(length, dtype=np.float32) / length).astype(np.float32)
    f = np.linspace(1e-4, HY_BANDS - 1, HY_BANDS, dtype=np.float32)
    arg = (w_ang[:, None] * f[None, :]).astype(np.float32).astype(np.float64)
    z = np.concatenate([t[:, None].astype(np.float64), np.cos(arg), -np.sin(arg)], axis=-1)
    z = np.pad(z, ((0, 0), (0, 64 - HY_EMB))).astype(np.float32)
    rev = np.concatenate([z[:1], z[:0:-1]], axis=0)
    deltas = np.linspace(math.log(HY_TARGET) / HY_FAST, math.log(HY_TARGET) / HY_SLOW,
                         HY_WIDTH, dtype=np.float32)
    return jnp.asarray(z), jnp.asarray(rev), jnp.asarray(np.abs(deltas)[None, :])


def _hyfilter_kernel(z_ref, zr_ref, dl_ref, w1_ref, b1_ref, w2_ref, b2_ref, w3_ref, fr_ref, o_ref):
    c = pl.program_id(0)
    fr = fr_ref[...]

    def mlp(z, w3):
        hid = jnp.sin(fr[0:1] * (_dot3(z, w1_ref[...]) + b1_ref[...]))
        hid = jnp.sin(fr[1:2] * (_dot3(hid, w2_ref[...]) + b2_ref[...]))
        return _dot3(hid, w3) * jnp.exp(-z[:, 0:1] * dl_ref[...])

    z, zr = z_ref[...], zr_ref[...]
    o_ref[0, :, 0:HY_WIDTH] = mlp(z, w3_ref[:, 0:HY_WIDTH]).astype(BF16)
    bwd = mlp(zr, w3_ref[:, HY_WIDTH:2 * HY_WIDTH])
    row = lax.broadcasted_iota(jnp.int32, bwd.shape, 0) + c * z.shape[0]
    o_ref[0, :, HY_WIDTH:2 * HY_WIDTH] = jnp.where(row == 0, 0.0, bwd).astype(BF16)


def _hyfilter(length, layer, w1p, b1, w2, b2, w3, fr):
    z, zr, dl = _filter_features(length)
    tb = TOKEN_BLOCK
    blk = pl.BlockSpec((tb, 64), lambda c: (c, 0))
    lay = lambda a: _layer_spec(a, layer)
    return pl.pallas_call(
        _hyfilter_kernel,
        grid=(length // tb,),
        in_specs=[blk, blk, pl.BlockSpec(dl.shape, lambda c: (0, 0)),
                  lay(w1p), lay(b1), lay(w2), lay(b2), lay(w3), lay(fr)],
        out_specs=pl.BlockSpec((1, tb, 2 * HY_WIDTH), lambda c: (0, c, 0)),
        out_shape=jax.ShapeDtypeStruct((1, length, 2 * HY_WIDTH), BF16),
        compiler_params=_cparams(1),
        name="hyena_filter",
    )(z, zr, dl, w1p, b1, w2, b2, w3, fr)


def _hyconv_kernel(main_ref, before_ref, after_ref, cw_ref, cb_ref, zb_ref, x0_ref, pad_ref, *, geo):
    i = pl.program_id(0)
    tb = TOKEN_BLOCK
    _fill_padded(pad_ref, before_ref, main_ref, after_ref, geo.is_start(i), geo.is_end(i))
    w = HY_WIDTH
    parts = []
    for p in range(3):
        cols = slice(p * w, (p + 1) * w)
        uc = (pad_ref[HALO - 1:HALO - 1 + tb, cols] * cw_ref[0:1, cols]
              + main_ref[:, cols] * cw_ref[1:2, cols]
              + pad_ref[HALO + 1:HALO + 1 + tb, cols] * cw_ref[2:3, cols]
              + cb_ref[:, cols])
        parts.append(uc)
    x0, x1, v = parts
    zb_ref[...] = (v * x1).astype(BF16)
    x0_ref[...] = x0.astype(BF16)


def _hyconv(geo, layer, u_hy, conv_w, conv_b):
    tb = TOKEN_BLOCK
    w3 = 3 * HY_WIDTH
    before, after = geo.halo_specs(w3)
    row = pl.BlockSpec((tb, HY_WIDTH), lambda i: (i, 0))
    return pl.pallas_call(
        functools.partial(_hyconv_kernel, geo=geo),
        grid=(geo.n // tb,),
        in_specs=[pl.BlockSpec((tb, w3), lambda i: (i, 0)), before, after,
                  _layer_spec(conv_w, layer), _layer_spec(conv_b, layer)],
        out_specs=[row, row],
        out_shape=[jax.ShapeDtypeStruct((geo.n, HY_WIDTH), BF16),
                   jax.ShapeDtypeStruct((geo.n, HY_WIDTH), BF16)],
        scratch_shapes=[pltpu.VMEM((tb + 2 * HALO, w3), F32)],
        compiler_params=_cparams(1),
        name="hyena_conv_gate",
    )(u_hy, u_hy, u_hy, conv_w, conv_b)


def _seqs_per_step(length, nseq):
    g = max(1, min(nseq, 2048 // length))
    assert nseq % g == 0
    return g


def _dft_fwd_kernel(c_ref, s_ref, x_ref, re_ref, im_ref):
    for s in range(x_ref.shape[0]):
        x = x_ref[s]
        re_ref[s] = _dot(c_ref[...], x)
        im_ref[s] = _dot(s_ref[...], x)


def _dft_fwd(length, x, nseq, seq_off):
    cmat, smat, _ = _dft_tables(length)
    n = x.shape[-1]
    tm = min(length, 512)
    tn = 512
    g = _seqs_per_step(length, nseq)
    assert seq_off % g == 0
    a_spec = pl.BlockSpec((tm, length), lambda b, j, m: (m, 0))
    o_spec = pl.BlockSpec((g, tm, tn), lambda b, j, m: (b, m, j))
    return pl.pallas_call(
        _dft_fwd_kernel,
        grid=(nseq // g, n // tn, length // tm),
        in_specs=[a_spec, a_spec, pl.BlockSpec((g, length, tn), lambda b, j, m: (b + seq_off // g, 0, j))],
        out_specs=[o_spec, o_spec],
        out_shape=[jax.ShapeDtypeStruct((nseq, length, n), F32)] * 2,
        compiler_params=_cparams(3),
        name="hyena_dft",
    )(cmat, smat, x)


def _hyprod_kernel(zr_ref, zi_ref, hfr_ref, hgr_ref, hfi_ref, hgi_ref, yr_ref, yi_ref, *, length):
    c = pl.program_id(1)
    shape = zr_ref.shape[1:]
    k = lax.broadcasted_iota(jnp.int32, shape, 0) + c * shape[0]
    sgn = (1 - 2 * (k % 2)).astype(F32)
    hr = hfr_ref[0] + sgn * hgr_ref[0]
    hi = hfi_ref[0] + sgn * hgi_ref[0]
    inv = 1.0 / length
    first = k == 0
    for s in range(zr_ref.shape[0]):
        zr, zi = zr_ref[s], zi_ref[s]
        yr = (zr * hr - zi * hi) * inv
        yi = (zr * hi + zi * hr) * inv
        yr_ref[s] = jnp.where(first, zr * hr * (0.5 * inv), yr).astype(BF16)
        yi_ref[s] = jnp.where(first, zi * hi * (0.5 * inv), yi).astype(BF16)


def _hyprod(length, zre, zim, hre, him):
    nb = zre.shape[0]
    tb = TOKEN_BLOCK
    g = _seqs_per_step(length, nb)
    zs = pl.BlockSpec((g, tb, HY_WIDTH), lambda b, c: (b, c, 0))
    hf = pl.BlockSpec((1, tb, HY_WIDTH), lambda b, c: (0, c, 0))
    hg = pl.BlockSpec((1, tb, HY_WIDTH), lambda b, c: (0, c, 1))
    return pl.pallas_call(
        functools.partial(_hyprod_kernel, length=length),
        grid=(nb // g, length // tb),
        in_specs=[zs, zs, hf, hg, hf, hg],
        out_specs=[zs, zs],
        out_shape=[jax.ShapeDtypeStruct((nb, length, HY_WIDTH), BF16)] * 2,
        compiler_params=_cparams(2),
        name="hyena_spectral_product",
    )(zre, zim, hre, hre, him, him)


def _hyinv_kernel(c_ref, st_ref, yr_ref, yi_ref, z_ref, x0_ref, skip_ref, o_ref):
    for s in range(yr_ref.shape[0]):
        y = _dot(c_ref[...], yr_ref[s]) + _dot(st_ref[...], yi_ref[s])
        o_ref[s] = ((y + z_ref[s].astype(F32) * skip_ref[...]) * x0_ref[s].astype(F32)).astype(BF16)


def _hyinv(length, layer, yr, yi, z, x0, skip, seq_off):
    cmat, _, smat_t = _dft_tables(length)
    nseq = yr.shape[0]
    tm = min(length, 512)
    g = _seqs_per_step(length, nseq)
    assert seq_off % g == 0
    a_spec = pl.BlockSpec((tm, length), lambda b, m: (m, 0))
    y_spec = pl.BlockSpec((g, length, HY_WIDTH), lambda b, m: (b, 0, 0))
    t_spec = pl.BlockSpec((g, tm, HY_WIDTH), lambda b, m: (b + seq_off // g, m, 0))
    return pl.pallas_call(
        _hyinv_kernel,
        grid=(nseq // g, length // tm),
        in_specs=[a_spec, a_spec, y_spec, y_spec, t_spec, t_spec, _layer_spec(skip, layer)],
        out_specs=pl.BlockSpec((g, tm, HY_WIDTH), lambda b, m: (b, m, 0)),
        out_shape=jax.ShapeDtypeStruct((nseq, length, HY_WIDTH), BF16),
        compiler_params=_cparams(2),
        name="hyena_idft",
    )(cmat, smat_t, yr, yi, z, x0, skip)


def _hyena_long_conv(length, layer, nseq, seq_off, zb, x0, filt, skip):
    view = lambda a: a.reshape(a.shape[0] // length, length, HY_WIDTH)
    zre, zim = _dft_fwd(length, view(zb), nseq, seq_off)
    hre, him = _dft_fwd(length, filt, 1, 0)
    yr, yi = _hyprod(length, zre, zim, hre, him)
    out = _hyinv(length, layer, yr, yi, view(zb), view(x0), skip, seq_off)
    return out.reshape(nseq * length, HY_WIDTH)


def _merge_kernel(x_ref, mod_ref, g1_ref, a_ref, bc_ref, bl_ref, cc_ref, cl_ref, wg_ref, bg_ref, wbr_ref,
                  wo_ref, g2_ref, wr_ref, br_ref, xo_ref, h2_ref, ti_ref, tp_ref, *, nbc):
    d = D_MODEL
    is_ctx = pl.program_id(0) < nbc
    mod = mod_ref[0]
    shift1, scale1, gate1 = mod[:, 0:d], mod[:, d:2 * d], mod[:, 2 * d:3 * d]
    shift2, scale2 = mod[:, 3 * d:4 * d], mod[:, 4 * d:5 * d]
    x = x_ref[...]
    h = (_rms(x) * g1_ref[...] * (1.0 + scale1) + shift1).astype(BF16)
    branches = (a_ref[...],
                jnp.where(is_ctx, bc_ref[...], bl_ref[...]),
                jnp.where(is_ctx, cc_ref[...], cl_ref[...]))
    merged = None
    for n, br_n in enumerate(branches):
        cols = slice(n * d, (n + 1) * d)
        gate = _sigmoid(_dot(h, wg_ref[:, cols]) + bg_ref[:, cols])
        term = gate * _dot(br_n, wbr_ref[n])
        merged = term if merged is None else merged + term
    x = x + gate1 * _dot(merged.astype(BF16), wo_ref[...])
    xo_ref[...] = x
    h2 = _rms(x) * g2_ref[...] * (1.0 + scale2) + shift2
    h2_ref[...] = h2.astype(BF16)

    logits = _dot3(h2, wr_ref[...]) + br_ref[...]
    ne = logits.shape[-1]
    lane = lax.broadcasted_iota(jnp.int32, logits.shape, 1).astype(F32)
    vals = logits
    top_v, top_i = [], []
    for _ in range(TOP_K):
        m = jnp.max(vals, axis=-1, keepdims=True)
        idx = jnp.min(jnp.where(vals == m, lane, float(ne)), axis=-1, keepdims=True)
        top_v.append(m)
        top_i.append(idx)
        vals = jnp.where(lane == idx, -jnp.inf, vals)
    es = [jnp.exp(v - top_v[0]) for v in top_v]
    den = es[0] + es[1] + es[2] + es[3]
    for kk in range(TOP_K):
        ti_ref[:, kk:kk + 1] = top_i[kk].astype(jnp.int32)
        tp_ref[:, kk:kk + 1] = es[kk] / den


def _merge(geo, layer, x, mods, g1, a_out, b_ctx, b_lat, c_ctx, c_lat, wg_b, bg, wbr_b, wo_b, g2, wr, br):
    d = D_MODEL
    tb = WIDE_BLOCK
    row = lambda w: pl.BlockSpec((tb, w), lambda i: (i, 0))
    ctx, lat = geo.split_specs(tb, DA_WIDTH)
    lay = lambda a: _layer_spec(a, layer, single_buffer=True)
    return pl.pallas_call(
        functools.partial(_merge_kernel, nbc=geo.nc // tb),
        grid=(geo.n // tb,),
        in_specs=[row(d), geo.mod_spec(layer, tb), lay(g1), row(POOL_WIDTH), ctx, lat, ctx, lat,
                  lay(wg_b), lay(bg), lay(wbr_b), lay(wo_b), lay(g2), lay(wr), lay(br)],
        out_specs=[row(d), row(d), row(TOP_K), row(TOP_K)],
        out_shape=[jax.ShapeDtypeStruct((geo.n, d), F32),
                   jax.ShapeDtypeStruct((geo.n, d), BF16),
                   jax.ShapeDtypeStruct((geo.n, TOP_K), jnp.int32),
                   jax.ShapeDtypeStruct((geo.n, TOP_K), F32)],
        compiler_params=_cparams(1),
        name="merge_route",
    )(x, mods, g1, a_out, b_ctx, b_lat, c_ctx, c_lat, wg_b, bg, wbr_b, wo_b, g2, wr, br)


def _ffn_kernel(sw_ref, st_ref, sn_ref, nv_ref, xs_hbm, wgu_ref, bgu_ref, wd_ref, bd_ref, ys_hbm,
                xbuf, ybuf, semx, semy):
    i = pl.program_id(0)
    nv = nv_ref[0]
    slot = i % 2
    t = MOE_TILE
    half = D_FF // 2

    def rows_of(step, k):
        return pl.ds(pl.multiple_of(st_ref[step] * t, t), k * t)

    def x_copy(step, buf, k):
        return pltpu.make_async_copy(xs_hbm.at[rows_of(step, k)], xbuf.at[buf, pl.ds(0, k * t)], semx.at[buf])

    def y_copy(step, buf, k):
        return pltpu.make_async_copy(ybuf.at[buf, pl.ds(0, k * t)], ys_hbm.at[rows_of(step, k)], semy.at[buf])

    def for_size(step, fn):
        n = sn_ref[step]
        for k in range(1, MOE_MAX_SUB + 1):
            @pl.when(n == k)
            def _(k=k):
                fn(k)

    def compute(k):
        m = k * t
        x = xbuf[slot, 0:m, :]
        y = None
        for c in range(2):
            gcols = slice(c * half, (c + 1) * half)
            ucols = slice(D_FF + c * half, D_FF + (c + 1) * half)
            gate = _dot(x, wgu_ref[0, :, gcols].astype(BF16)) + bgu_ref[0, :, gcols]
            up = _dot(x, wgu_ref[0, :, ucols].astype(BF16)) + bgu_ref[0, :, ucols]
            gate = jnp.minimum(gate, SWIGLU_LIMIT)
            up = jnp.clip(up, -SWIGLU_LIMIT, SWIGLU_LIMIT)
            act = ((up + 1.0) * gate * _sigmoid(SWIGLU_ALPHA * gate)).astype(BF16)
            part = _dot(act, wd_ref[0, gcols, :].astype(BF16))
            y = part if y is None else y + part
        ybuf[slot, 0:m, :] = (y + bd_ref[0]).astype(BF16)
        y_copy(i, slot, k).start()

    @pl.when(i < nv)
    def _():
        @pl.when(i == 0)
        def _():
            for_size(0, lambda k: x_copy(0, 0, k).start())

        for_size(i, lambda k: x_copy(i, slot, k).wait())

        @pl.when(i + 1 < nv)
        def _():
            for_size(i + 1, lambda k: x_copy(i + 1, 1 - slot, k).start())

        @pl.when(i >= 2)
        def _():
            for_size(i - 2, lambda k: y_copy(i - 2, slot, k).wait())

        for_size(i, compute)

        @pl.when(i == nv - 1)
        def _():
            for_size(i, lambda k: y_copy(i, slot, k).wait())

            @pl.when(i >= 1)
            def _():
                for_size(i - 1, lambda k: y_copy(i - 1, 1 - slot, k).wait())


def _ffn(xs, step_weight, step_tile, step_sub, n_valid, w_gu, b_gu, w_down, b_down):
    p = xs.shape[0]
    d = D_MODEL
    t = MOE_TILE
    hbm = pl.BlockSpec(memory_space=pl.ANY)
    wmap = lambda i, sw, st, sn, nv: (sw[i], 0, 0)
    grid_spec = pltpu.PrefetchScalarGridSpec(
        num_scalar_prefetch=4,
        grid=(step_weight.shape[0],),
        in_specs=[hbm,
                  pl.BlockSpec((1, d, 2 * D_FF), wmap), pl.BlockSpec((1, 1, 2 * D_FF), wmap),
                  pl.BlockSpec((1, D_FF, d), wmap), pl.BlockSpec((1, 1, d), wmap)],
        out_specs=hbm,
        scratch_shapes=[pltpu.VMEM((2, MOE_MAX_SUB * t, d), BF16), pltpu.VMEM((2, MOE_MAX_SUB * t, d), BF16),
                        pltpu.SemaphoreType.DMA((2,)), pltpu.SemaphoreType.DMA((2,))])
    return pl.pallas_call(
        _ffn_kernel,
        grid_spec=grid_spec,
        out_shape=jax.ShapeDtypeStruct((p, d), BF16),
        compiler_params=_cparams(1),
        name="moe_experts",
    )(step_weight, step_tile, step_sub, n_valid, xs, w_gu, b_gu, w_down, b_down)


def _moe_rows(n, n_experts):
    worst = n * TOP_K + n_experts * ((n // DISPATCH_BLOCK) * (RUN_ALIGN - 1) + MOE_TILE - 1)
    return -(-worst // MOE_TILE) * MOE_TILE


def _route(top_i, n_experts):
    n = top_i.shape[0]
    t = MOE_TILE
    nb = n // DISPATCH_BLOCK
    experts = jnp.arange(n_experts, dtype=jnp.int32)
    onehot = top_i.reshape(nb, DISPATCH_BLOCK * TOP_K)[:, :, None] == experts[None, None, :]
    n_be = jnp.sum(onehot, axis=1, dtype=jnp.int32)
    cap = (n_be + RUN_ALIGN - 1) // RUN_ALIGN * RUN_ALIGN
    rows_e = jnp.sum(cap, axis=0)
    tiles_e = (rows_e + t - 1) // t
    tile_end = jnp.cumsum(tiles_e)
    tile_start = tile_end - tiles_e
    run_dst = tile_start[None, :] * t + jnp.cumsum(cap, axis=0) - cap
    run_loc = jnp.cumsum(cap, axis=1) - cap
    tail_dst = tile_start * t + rows_e
    tail = tiles_e * t - rows_e

    def owner(ends, ids):
        return jnp.sum(ends[None, :] <= ids[:, None], axis=1, dtype=jnp.int32)[:, None] == experts[None, :]

    pick = lambda oh, table: jnp.sum(jnp.where(oh, table[None, :], 0), axis=1)

    steps_e = (tiles_e + MOE_MAX_SUB - 1) // MOE_MAX_SUB
    step_end = jnp.cumsum(steps_e)
    n_steps = _moe_rows(n, n_experts) // (t * MOE_MAX_SUB) + n_experts
    n_valid = step_end[-1]
    step_ids = jnp.minimum(jnp.arange(n_steps, dtype=jnp.int32), n_valid - 1)
    step_oh = owner(step_end, step_ids)
    j = step_ids - pick(step_oh, step_end - steps_e)
    step_expert = jnp.argmax(step_oh, axis=1).astype(jnp.int32)
    n_st = jnp.maximum(pick(step_oh, steps_e), 1)
    n_ti = pick(step_oh, tiles_e)
    base, rem = n_ti // n_st, n_ti % n_st
    extra = jnp.maximum(j - (n_st - rem), 0)
    step_tile = pick(step_oh, tile_start) + j * base + extra
    step_sub = jnp.clip(base + (j >= n_st - rem).astype(jnp.int32), 1, MOE_MAX_SUB)
    units = lambda a: (a // RUN_ALIGN).reshape(-1).astype(jnp.int32)
    runs = (units(run_dst), units(run_loc), units(cap), units(jnp.sum(cap, axis=1)), units(tail_dst), units(tail))
    cols = (run_dst.reshape(nb, n_experts, 1).astype(F32), run_loc.reshape(nb, n_experts, 1).astype(F32))
    steps = (step_expert, step_tile.astype(jnp.int32), step_sub.astype(jnp.int32),
             n_valid.reshape(1).astype(jnp.int32))
    return runs, cols, steps


def _sized_copies(units, src_at, dst_at, sem, max_bits, fn):
    for bit in range(max_bits):
        size = RUN_ALIGN << bit

        @pl.when((units >> bit) & 1 == 1)
        def _(bit=bit, size=size):
            off = (units & ((1 << bit) - 1)) * RUN_ALIGN
            fn(pltpu.make_async_copy(src_at(off, size), dst_at(off, size), sem))


def _dispatch_kernel(rdst_ref, rloc_ref, rcap_ref, rtot_ref, tdst_ref, tlen_ref,
                     h_ref, tt_ref, dcol_ref, lcol_ref, xs_hbm, slot_ref, stage, zeros, sem, zsem,
                     *, n_experts):
    b = pl.program_id(0)
    nb = pl.num_programs(0)
    tb = DISPATCH_BLOCK
    buf = b % 2
    run_bits = (tb // RUN_ALIGN).bit_length()
    tail_bits = (MOE_TILE // RUN_ALIGN).bit_length()
    stage_bits = (stage.shape[1] // RUN_ALIGN).bit_length()

    def drain_runs(blk, slot):
        _sized_copies(rtot_ref[blk],
                      lambda off, size: stage.at[slot, pl.ds(0, size)],
                      lambda off, size: xs_hbm.at[pl.ds(0, size)],
                      sem.at[slot], stage_bits, lambda cp: cp.wait())

    def run_copies(blk, slot, fn):
        def body(e, carry):
            idx = blk * n_experts + e
            src0 = pl.multiple_of(rloc_ref[idx] * RUN_ALIGN, RUN_ALIGN)
            dst0 = pl.multiple_of(rdst_ref[idx] * RUN_ALIGN, RUN_ALIGN)
            _sized_copies(rcap_ref[idx],
                          lambda off, size: stage.at[slot, pl.ds(pl.multiple_of(src0 + off, RUN_ALIGN), size)],
                          lambda off, size: xs_hbm.at[pl.ds(pl.multiple_of(dst0 + off, RUN_ALIGN), size)],
                          sem.at[slot], run_bits, fn)
            return carry
        lax.fori_loop(0, n_experts, body, 0)

    def tail_copies(fn):
        def body(e, carry):
            dst0 = pl.multiple_of(tdst_ref[e] * RUN_ALIGN, RUN_ALIGN)
            _sized_copies(tlen_ref[e],
                          lambda off, size: zeros.at[pl.ds(0, size)],
                          lambda off, size: xs_hbm.at[pl.ds(pl.multiple_of(dst0 + off, RUN_ALIGN), size)],
                          zsem, tail_bits, fn)
            return carry
        lax.fori_loop(0, n_experts, body, 0)

    @pl.when(b == 0)
    def _():
        zeros[...] = jnp.zeros(zeros.shape, zeros.dtype)
        tail_copies(lambda cp: cp.start())

    @pl.when(b >= 2)
    def _():
        drain_runs(b - 2, buf)

    tt = tt_ref[...]
    e_iota = lax.broadcasted_iota(jnp.int32, (n_experts, tb), 0)
    hit = [e_iota == tt[k:k + 1, :] for k in range(TOP_K)]
    member = jnp.where(hit[0] | hit[1] | hit[2] | hit[3], 1.0, 0.0)
    before = (lax.broadcasted_iota(jnp.int32, (tb, tb), 0)
              < lax.broadcasted_iota(jnp.int32, (tb, tb), 1))
    rank = _dot(member.astype(BF16), jnp.where(before, 1.0, 0.0).astype(BF16))
    loc = [jnp.sum(jnp.where(hit[k], lcol_ref[0] + rank, 0.0), axis=0, keepdims=True) for k in range(TOP_K)]
    for k in range(TOP_K):
        slot_ref[k:k + 1, :] = jnp.sum(jnp.where(hit[k], dcol_ref[0] + rank, 0.0),
                                       axis=0, keepdims=True).astype(jnp.int32)

    rows = stage.shape[1]
    r_iota = lax.broadcasted_iota(jnp.int32, (rows, tb), 0).astype(F32)
    place = jnp.zeros((rows, tb), F32)
    for k in range(TOP_K):
        place = jnp.where(r_iota == loc[k], 1.0, place)
    stage[buf] = _dot(place.astype(BF16), h_ref[...]).astype(BF16)
    run_copies(b, buf, lambda cp: cp.start())

    @pl.when(b == nb - 1)
    def _():
        drain_runs(b, buf)

        @pl.when(b >= 1)
        def _():
            drain_runs(b - 1, 1 - buf)
        tail_copies(lambda cp: cp.wait())


def _dispatch(h2, top_t, runs, cols, n_experts):
    n, d = h2.shape
    tb = DISPATCH_BLOCK
    stage_rows = tb * TOP_K + n_experts * RUN_ALIGN
    col = pl.BlockSpec((1, n_experts, 1), lambda b, *_: (b, 0, 0))
    grid_spec = pltpu.PrefetchScalarGridSpec(
        num_scalar_prefetch=6,
        grid=(n // tb,),
        in_specs=[pl.BlockSpec((tb, d), lambda b, *_: (b, 0)),
                  pl.BlockSpec((TOP_K, tb), lambda b, *_: (0, b)), col, col],
        out_specs=[pl.BlockSpec(memory_space=pl.ANY), pl.BlockSpec((TOP_K, tb), lambda b, *_: (0, b))],
        scratch_shapes=[pltpu.VMEM((2, stage_rows, d), BF16), pltpu.VMEM((MOE_TILE, d), BF16),
                        pltpu.SemaphoreType.DMA((2,)), pltpu.SemaphoreType.DMA(())])
    return pl.pallas_call(
        functools.partial(_dispatch_kernel, n_experts=n_experts),
        grid_spec=grid_spec,
        out_shape=[jax.ShapeDtypeStruct((_moe_rows(n, n_experts), d), BF16),
                   jax.ShapeDtypeStruct((TOP_K, n), jnp.int32)],
        compiler_params=_cparams(1),
        name="moe_dispatch",
    )(*runs, h2, top_t, *cols)


def _moe_sum(x_ref, mod_ref, yg_ref, p_ref):
    d = D_MODEL
    gate2 = mod_ref[0][:, 5 * d:6 * d]
    p = p_ref[...]
    moe = p[:, 0:1] * yg_ref[0].astype(F32)
    for kk in range(1, TOP_K):
        moe = moe + p[:, kk:kk + 1] * yg_ref[kk].astype(F32)
    return x_ref[...] + gate2 * moe


def _combine_final_kernel(x_ref, mod_ref, yg_ref, p_ref, gf_ref, yc_ref, yl_ref, *, nbc):
    i = pl.program_id(0)
    y = _rms(_moe_sum(x_ref, mod_ref, yg_ref, p_ref)) * gf_ref[...]

    @pl.when(i < nbc)
    def _():
        yc_ref[...] = y

    @pl.when(i >= nbc)
    def _():
        yl_ref[...] = y


def _combine(geo, layer, x, mods, yg, top_p, final_g):
    d = D_MODEL
    tb = TOKEN_BLOCK
    row = pl.BlockSpec((tb, d), lambda i: (i, 0))
    in_specs = [row, geo.mod_spec(layer, tb),
                pl.BlockSpec((TOP_K, tb, d), lambda i: (0, i, 0)),
                pl.BlockSpec((tb, TOP_K), lambda i: (i, 0))]
    ctx, lat = geo.split_specs(tb, d)
    return pl.pallas_call(
        functools.partial(_combine_final_kernel, nbc=geo.nc // tb),
        grid=(geo.n // tb,),
        in_specs=in_specs + [pl.BlockSpec(final_g.shape, lambda i: (0, 0))],
        out_specs=[ctx, lat],
        out_shape=[jax.ShapeDtypeStruct((geo.nc, d), F32), jax.ShapeDtypeStruct((geo.nl, d), F32)],
        compiler_params=_cparams(1), name="moe_combine_final",
    )(x, mods, yg, top_p, final_g)


@functools.lru_cache(maxsize=None)
def _rope_tables(length):
    rows = length // GRID_W
    row = np.repeat(np.arange(rows), GRID_W).astype(np.float32)
    col = np.tile(np.arange(GRID_W), rows).astype(np.float32)
    ax = DA_HEAD_DIM // 2
    inv = (ROPE_BASE ** (-(np.arange(ax // 2, dtype=np.float32) * 2.0 / ax))).astype(np.float32)
    ang_r = (row[:, None] * inv).astype(np.float32)
    ang_c = (col[:, None] * inv).astype(np.float32)
    ang = np.concatenate([ang_r, ang_r, ang_c, ang_c], axis=-1).astype(np.float64)
    sign = np.where((np.arange(DA_HEAD_DIM) % 32) < 16, -1.0, 1.0)
    reps = DA_WIDTH // DA_HEAD_DIM
    cos = np.tile(np.cos(ang), (1, reps)).astype(np.float32)
    sin_signed = np.tile(np.sin(ang) * sign[None, :], (1, reps)).astype(np.float32)
    return jnp.asarray(cos), jnp.asarray(sin_signed)


def kernel(x_prompt, x_sample, cache_k, cache_v, c, c_ctx, w_ada, b_ada, norm1, norm2, w_in, w_pool, pool_scale, da_lambda, da_subln, hy_conv_w, hy_conv_b, hy_f_w1, hy_f_b1, hy_f_w2, hy_f_b2, hy_f_w3, hy_sin_freq, hy_skip, w_branch, w_gate, b_gate, w_o, w_router, b_router, w_gu, b_gu, w_down, b_down, final_norm):
    bc, lc, d = x_prompt.shape
    bl, ll, _ = x_sample.shape
    depth = w_in.shape[0]
    n_experts = w_router.shape[-1]
    past = cache_k.shape[2]
    geo = _Geom(bc, lc, bl, ll)
    assert 1 + bl <= MOD_ROWS

    x = jnp.concatenate([x_prompt.reshape(bc * lc, d), x_sample.reshape(bl * ll, d)], axis=0)
    cond = jnp.concatenate([c_ctx[None], c, jnp.zeros((MOD_ROWS - 1 - bl, d), F32)], axis=0)
    mods = _ada(cond, w_ada, b_ada).reshape(depth * MOD_ROWS, 1, N_MOD * d)
    cos_t, sin_t = _rope_tables(ll)
    ck = cache_k.reshape(bl, depth, past, DA_WIDTH)
    cv = cache_v.reshape(bl, depth, past, DA_WIDTH)

    row3 = lambda a: a.reshape(depth, 1, a.shape[-1])
    g1, g2 = row3(norm1), row3(norm2)
    w_in_b = w_in.astype(BF16)
    w_pool_b = w_pool.astype(BF16)
    w_gate_b = w_gate.astype(BF16)
    w_branch_b = w_branch.astype(BF16)
    w_o_b = w_o.astype(BF16)
    w1p = jnp.pad(hy_f_w1, ((0, 0), (0, 64 - HY_EMB), (0, 0)))
    fargs = (w1p, row3(hy_f_b1), hy_f_w2, row3(hy_f_b2), hy_f_w3, hy_sin_freq)
    w_gu_s = w_gu.reshape(depth * n_experts, d, 2 * D_FF)
    b_gu_s = b_gu.reshape(depth * n_experts, 1, 2 * D_FF)
    w_down_s = w_down.reshape(depth * n_experts, D_FF, d)
    b_down_s = b_down.reshape(depth * n_experts, 1, d)

    new_k = jnp.zeros((bc, depth, lc * DA_HEADS, DA_VDIM), F32)
    new_v = jnp.zeros((bc, depth, lc * DA_HEADS, DA_VDIM), F32)
    moe = None
    for l in range(depth):
        x, u_pool, q, kb, vb, new_k, new_v, u_hy = _inproj(geo, l, x, mods, g1, w_in_b, cos_t, sin_t,
                                                           new_k, new_v, moe)
        a_out = _pool(geo, l, u_pool, w_pool_b, row3(pool_scale))

        lam_init = 0.8 - 0.6 * math.exp(-0.3 * l)
        subln = row3(da_subln)
        b_ctx = _attn_ctx(geo, l, q, kb, vb, da_lambda, subln, lam_init)
        b_lat = _attn_lat(geo, l, q, kb, vb, ck, cv, da_lambda, subln, lam_init)

        zb, x0 = _hyconv(geo, l, u_hy, hy_conv_w, row3(hy_conv_b))
        skip = row3(hy_skip)
        c_ctx_out = _hyena_long_conv(lc, l, bc, 0, zb, x0, _hyfilter(lc, l, *fargs), skip)
        c_lat_out = _hyena_long_conv(ll, l, bl, geo.nc // ll, zb, x0, _hyfilter(ll, l, *fargs), skip)

        x, h2, top_i, top_p = _merge(geo, l, x, mods, g1, a_out, b_ctx, b_lat, c_ctx_out, c_lat_out,
                                     w_gate_b, row3(b_gate), w_branch_b, w_o_b, g2, w_router, row3(b_router))

        runs, cols, (step_expert, step_tile, step_sub, n_valid) = _route(top_i, n_experts)
        xs, pair_slot = _dispatch(h2, top_i.T, runs, cols, n_experts)
        ys = _ffn(xs, step_expert + l * n_experts, step_tile, step_sub, n_valid,
                  w_gu_s, b_gu_s, w_down_s, b_down_s)
        yg = jnp.take(ys, pair_slot, axis=0, mode="clip")
        moe = (yg, top_p)

    y_ctx, y_lat = _combine(geo, depth - 1, x, mods, yg, top_p, final_norm[None])

    return (y_ctx.reshape(bc, lc, d), y_lat.reshape(bl, ll, d),
            new_k.reshape(bc, depth, lc, DA_HEADS, 2 * DA_HEAD_DIM),
            new_v.reshape(bc, depth, lc, DA_HEADS, DA_VDIM))
```

```python
import functools
import math

import numpy as np
import jax
import jax.numpy as jnp
from jax import lax
from jax.experimental import pallas as pl
from jax.experimental.pallas import tpu as pltpu

F32 = jnp.float32
BF16 = jnp.bfloat16

D_MODEL = 1024
GRID_W = 64
NORM_EPS = 1e-6
POOL_WIDTH = 512
POOL_WINDOWS = (2, 4, 8, 16)
POOL_GC = POOL_WIDTH // len(POOL_WINDOWS)
DA_HEADS = 4
DA_HEAD_DIM = 64
DA_VDIM = 2 * DA_HEAD_DIM
DA_WIDTH = DA_HEADS * DA_VDIM
ROPE_BASE = 10000.0
HY_WIDTH = 512
HY_EMB = 33
HY_BANDS = (HY_EMB - 1) // 2
HY_HIDDEN = 64
HY_FAST = 0.3
HY_SLOW = 1.5
HY_TARGET = 1e-2
N_BRANCH = 3
D_IN = POOL_WIDTH + 3 * DA_WIDTH + 3 * HY_WIDTH
TOP_K = 4
D_FF = 1024
SWIGLU_ALPHA = 1.702
SWIGLU_LIMIT = 7.0
N_MOD = 6
MOD_ROWS = 8

TOKEN_BLOCK = 256
WIDE_BLOCK = 512
HALO = 16
MOE_TILE = 128
MOE_MAX_SUB = 8
DISPATCH_BLOCK = 512
RUN_ALIGN = 16
V7X_VMEM_LIMIT = 56 * 1024 * 1024


def _cparams(n_axes):
    return pltpu.CompilerParams(
        dimension_semantics=("arbitrary",) * n_axes,
        vmem_limit_bytes=V7X_VMEM_LIMIT)


def _dot(a, b):
    return jnp.dot(a, b, preferred_element_type=F32)


def _dot_nt(a, b):
    return lax.dot_general(a, b, (((1,), (1,)), ((), ())), preferred_element_type=F32)


def _split_bf16(a):
    hi = a.astype(BF16)
    lo = (a - hi.astype(F32)).astype(BF16)
    return hi, lo


def _dot3(a, b):
    ah, al = _split_bf16(a)
    bh, bl = _split_bf16(b)
    return _dot(ah, bh) + _dot(al, bh) + _dot(ah, bl)


def _sigmoid(x):
    return 0.5 * jnp.tanh(0.5 * x) + 0.5


def _rms(x):
    return x * lax.rsqrt(jnp.mean(x * x, axis=-1, keepdims=True) + NORM_EPS)


class _Geom:
    def __init__(self, n_ctx_seq, ctx_len, n_lat_seq, lat_len):
        assert ctx_len == TOKEN_BLOCK, "one context sequence per token block"
        assert lat_len % WIDE_BLOCK == 0 and (n_ctx_seq * ctx_len) % lat_len == 0
        self.bc, self.lc, self.bl, self.ll = n_ctx_seq, ctx_len, n_lat_seq, lat_len
        self.nc = n_ctx_seq * ctx_len
        self.nl = n_lat_seq * lat_len
        self.n = self.nc + self.nl

    def group(self, i, tb):
        nbc = self.nc // tb
        return jnp.where(i < nbc, 0, 1 + (i - nbc) // (self.ll // tb))

    def pos_block(self, i, tb):
        nbc = self.nc // tb
        return jnp.where(i < nbc, 0, (i - nbc) % (self.ll // tb))

    def is_start(self, i):
        nbc, bpl = self.nc // TOKEN_BLOCK, self.ll // TOKEN_BLOCK
        return jnp.logical_or(i < nbc, (i - nbc) % bpl == 0)

    def is_end(self, i):
        nbc, bpl = self.nc // TOKEN_BLOCK, self.ll // TOKEN_BLOCK
        return jnp.logical_or(i < nbc, (i - nbc) % bpl == bpl - 1)

    def mod_spec(self, layer, tb):
        return pl.BlockSpec((1, 1, N_MOD * D_MODEL),
                            lambda i: (layer * MOD_ROWS + self.group(i, tb), 0, 0))

    def halo_specs(self, width):
        per = TOKEN_BLOCK // HALO
        last = self.n // HALO - 1
        before = pl.BlockSpec((HALO, width), lambda i: (jnp.maximum(i * per - 1, 0), 0))
        after = pl.BlockSpec((HALO, width), lambda i: (jnp.minimum((i + 1) * per, last), 0))
        return before, after

    def split_specs(self, tb, width):
        nbc = self.nc // tb
        last_lat = self.nl // tb - 1
        ctx = pl.BlockSpec((tb, width), lambda i: (jnp.minimum(i, nbc - 1), 0))
        lat = pl.BlockSpec((tb, width), lambda i: (jnp.clip(i - nbc, 0, last_lat), 0))
        return ctx, lat


def _layer_spec(a, layer, single_buffer=False):
    kw = dict(pipeline_mode=pl.Buffered(1)) if single_buffer else {}
    return pl.BlockSpec((None,) + a.shape[1:], lambda *_: (layer,) + (0,) * (a.ndim - 1), **kw)


def _ada_kernel(c_ref, w_ref, b_ref, o_ref):
    c = c_ref[...]
    s = c * _sigmoid(c)
    o_ref[0] = _dot(s.astype(BF16), w_ref[0].astype(BF16)) + b_ref[0]


def _ada(cond, w_ada, b_ada):
    depth, d, n6 = w_ada.shape
    rows = cond.shape[0]
    tn = 1024
    return pl.pallas_call(
        _ada_kernel,
        grid=(depth, n6 // tn),
        in_specs=[pl.BlockSpec((rows, d), lambda l, j: (0, 0)),
                  pl.BlockSpec((1, d, tn), lambda l, j: (l, 0, j)),
                  pl.BlockSpec((1, 1, tn), lambda l, j: (l, 0, j))],
        out_specs=pl.BlockSpec((1, rows, tn), lambda l, j: (l, 0, j)),
        out_shape=jax.ShapeDtypeStruct((depth, rows, n6), F32),
        compiler_params=_cparams(2),
        name="ada_mod",
    )(cond, w_ada, b_ada.reshape(depth, 1, n6))


def _rope(x, cos, sin_signed, first_half):
    d = x.shape[-1]
    partner = jnp.where(first_half, pltpu.roll(x, d - 16, 1), pltpu.roll(x, 16, 1))
    return x * cos + partner * sin_signed


def _inproj_kernel(*refs, nbc, lc, add_moe):
    if add_moe:
        x_ref, pmod_ref, yg_ref, p_ref = refs[:4]
        refs = refs[4:]
        (mod_ref, g_ref, w_ref, cos_ref, sin_ref, _, _,
         xo_ref, up_ref, q_ref, kb_ref, vb_ref, kf_ref, vf_ref, uh_ref) = refs
        x = _moe_sum(x_ref, pmod_ref, yg_ref, p_ref)
        xo_ref[...] = x
    else:
        (x_ref, mod_ref, g_ref, w_ref, cos_ref, sin_ref, _, _,
         up_ref, q_ref, kb_ref, vb_ref, kf_ref, vf_ref, uh_ref) = refs
        x = x_ref[...]
    i = pl.program_id(0)
    d = D_MODEL
    mod = mod_ref[0]
    shift, scale = mod[:, 0:d], mod[:, d:2 * d]
    h = (_rms(x) * g_ref[...] * (1.0 + scale) + shift).astype(BF16)

    c1 = POOL_WIDTH
    c2 = c1 + DA_WIDTH
    c3 = c2 + DA_WIDTH
    c4 = c3 + DA_WIDTH
    up_ref[...] = _dot(h, w_ref[:, 0:c1])
    uh_ref[...] = _dot(h, w_ref[:, c4:D_IN]).astype(BF16)
    q = _dot(h, w_ref[:, c1:c2]) * (DA_HEAD_DIM ** -0.5 * math.log2(math.e))
    k = _dot(h, w_ref[:, c2:c3])
    v = _dot(h, w_ref[:, c3:c4])
    vb_ref[...] = v.astype(BF16)

    @pl.when(i < nbc)
    def _():
        q_ref[...] = q.astype(BF16)
        kb_ref[...] = k.astype(BF16)
        for s in range(k.shape[0] // lc):
            for hh in range(DA_HEADS):
                rows = pl.ds(hh, lc, stride=DA_HEADS)
                cols = slice(hh * DA_VDIM, (hh + 1) * DA_VDIM)
                kf_ref[s, 0, rows, :] = k[s * lc:(s + 1) * lc, cols]
                vf_ref[s, 0, rows, :] = v[s * lc:(s + 1) * lc, cols]

    @pl.when(i >= nbc)
    def _():
        cos, sin_signed = cos_ref[...], sin_ref[...]
        lane = lax.broadcasted_iota(jnp.int32, q.shape, 1)
        first_half = (lane % 32) < 16
        q_ref[...] = _rope(q, cos, sin_signed, first_half).astype(BF16)
        kb_ref[...] = _rope(k, cos, sin_signed, first_half).astype(BF16)


def _inproj(geo, layer, x, mods, g1, w_in_b, cos_t, sin_t, kacc, vacc, moe=None):
    d = D_MODEL
    tb = WIDE_BLOCK
    nbc = geo.nc // tb
    spb = tb // geo.lc
    row = lambda w: pl.BlockSpec((tb, w), lambda i: (i, 0))
    cache = pl.BlockSpec((spb, 1, geo.lc * DA_HEADS, DA_VDIM), lambda i: (jnp.minimum(i, nbc - 1), layer, 0, 0))
    tab = pl.BlockSpec((tb, DA_WIDTH), lambda i: (geo.pos_block(i, tb), 0))
    hbm = pl.BlockSpec(memory_space=pl.ANY)
    in_specs = [geo.mod_spec(layer, tb), _layer_spec(g1, layer),
                _layer_spec(w_in_b, layer, single_buffer=True), tab, tab, hbm, hbm]
    args = [mods, g1, w_in_b, cos_t, sin_t, kacc, vacc]
    out_specs = [row(POOL_WIDTH), row(DA_WIDTH), row(DA_WIDTH), row(DA_WIDTH), cache, cache, row(3 * HY_WIDTH)]
    out_shape = [jax.ShapeDtypeStruct((geo.n, POOL_WIDTH), F32),
                 jax.ShapeDtypeStruct((geo.n, DA_WIDTH), BF16),
                 jax.ShapeDtypeStruct((geo.n, DA_WIDTH), BF16),
                 jax.ShapeDtypeStruct((geo.n, DA_WIDTH), BF16),
                 jax.ShapeDtypeStruct(kacc.shape, F32),
                 jax.ShapeDtypeStruct(vacc.shape, F32),
                 jax.ShapeDtypeStruct((geo.n, 3 * HY_WIDTH), BF16)]
    if moe is None:
        in_specs = [row(d)] + in_specs
        args = [x] + args
    else:
        yg, top_p = moe
        in_specs = [row(d), geo.mod_spec(layer - 1, tb),
                    pl.BlockSpec((TOP_K, tb, d), lambda i: (0, i, 0)),
                    pl.BlockSpec((tb, TOP_K), lambda i: (i, 0))] + in_specs
        args = [x, mods, yg, top_p] + args
        out_specs = [row(d)] + out_specs
        out_shape = [jax.ShapeDtypeStruct((geo.n, d), F32)] + out_shape
    n_in, n_out = len(args), len(out_shape)
    outs = pl.pallas_call(
        functools.partial(_inproj_kernel, nbc=nbc, lc=geo.lc, add_moe=moe is not None),
        grid=(geo.n // tb,),
        in_specs=in_specs, out_specs=out_specs, out_shape=out_shape,
        input_output_aliases={n_in - 2: n_out - 3, n_in - 1: n_out - 2},
        compiler_params=_cparams(1),
        name="in_proj",
    )(*args)
    return outs if moe is not None else [x] + list(outs)


def _fill_padded(pad_ref, before_ref, main_ref, after_ref, start, end):
    tb = TOKEN_BLOCK
    zero = jnp.zeros(before_ref.shape, F32)
    pad_ref[0:HALO, :] = jnp.where(start, zero, before_ref[...].astype(F32))
    pad_ref[HALO:HALO + tb, :] = main_ref[...].astype(F32)
    pad_ref[HALO + tb:2 * HALO + tb, :] = jnp.where(end, zero, after_ref[...].astype(F32))


def _pool_kernel(main_ref, before_ref, after_ref, w_ref, s_ref, o_ref, pad_ref, *, geo):
    i = pl.program_id(0)
    tb = TOKEN_BLOCK
    start, end = geo.is_start(i), geo.is_end(i)
    _fill_padded(pad_ref, before_ref, main_ref, after_ref, start, end)
    r = lax.broadcasted_iota(jnp.int32, (tb, 1), 0)
    for g, w in enumerate(POOL_WINDOWS):
        cols = slice(g * POOL_GC, (g + 1) * POOL_GC)
        acc = pad_ref[HALO - w // 2:HALO - w // 2 + tb, cols]
        for j in range(-w // 2 + 1, w // 2):
            acc = acc + pad_ref[HALO + j:HALO + j + tb, cols]
        lo = jnp.where(start, jnp.maximum(r - w // 2, 0), r - w // 2)
        hi = jnp.where(end, jnp.minimum(r + w // 2, tb), r + w // 2)
        mean = acc / (hi - lo).astype(F32)
        dlt = mean - main_ref[:, cols]
        y = _dot(dlt.astype(BF16), w_ref[g])
        o_ref[:, cols] = (y * s_ref[:, cols]).astype(BF16)


def _pool(geo, layer, u_pool, w_pool_b, pool_scale):
    tb = TOKEN_BLOCK
    before, after = geo.halo_specs(POOL_WIDTH)
    return pl.pallas_call(
        functools.partial(_pool_kernel, geo=geo),
        grid=(geo.n // tb,),
        in_specs=[pl.BlockSpec((tb, POOL_WIDTH), lambda i: (i, 0)), before, after,
                  _layer_spec(w_pool_b, layer), _layer_spec(pool_scale, layer)],
        out_specs=pl.BlockSpec((tb, POOL_WIDTH), lambda i: (i, 0)),
        out_shape=jax.ShapeDtypeStruct((geo.n, POOL_WIDTH), BF16),
        scratch_shapes=[pltpu.VMEM((tb + 2 * HALO, POOL_WIDTH), F32)],
        compiler_params=_cparams(1),
        name="pool_branch",
    )(u_pool, u_pool, u_pool, w_pool_b, pool_scale)


def _attn_kernel(*refs, lam_init, has_ctx):
    if has_ctx:
        q_ref, k_ref, v_ref, kc_ref, vc_ref, lam_ref, g_ref, o_ref = refs
    else:
        q_ref, k_ref, v_ref, lam_ref, g_ref, o_ref = refs
    lp = lam_ref[...]
    lam = (jnp.exp(jnp.sum(lp[0:1] * lp[1:2], axis=-1, keepdims=True))
           - jnp.exp(jnp.sum(lp[2:3] * lp[3:4], axis=-1, keepdims=True)) + lam_init)
    for h in range(DA_HEADS):
        vcols = slice(h * DA_VDIM, (h + 1) * DA_VDIM)
        v = v_ref[0, :, vcols]
        if has_ctx:
            ones = lambda a: jnp.concatenate([a, jnp.ones_like(a)], axis=1)
            v = ones(v)
            vc = ones(vc_ref[0, 0, :, vcols].astype(BF16))
        outs = []
        for sub in range(2):
            c0 = h * DA_VDIM + sub * DA_HEAD_DIM
            cols = slice(c0, c0 + DA_HEAD_DIM)
            qs = q_ref[0, :, cols]
            s1 = _dot_nt(qs, k_ref[0, :, cols])
            m = jnp.max(s1, axis=-1, keepdims=True)
            if has_ctx:
                s2 = _dot_nt(qs, kc_ref[0, 0, :, cols].astype(BF16))
                m = jnp.maximum(m, jnp.max(s2, axis=-1, keepdims=True))
            e1 = jnp.exp2(s1 - m)
            pv = _dot(e1.astype(BF16), v)
            if has_ctx:
                pv = pv + _dot(jnp.exp2(s2 - m).astype(BF16), vc)
                outs.append(pv[:, 0:DA_VDIM] / pv[:, DA_VDIM:DA_VDIM + 1])
            else:
                outs.append(pv / jnp.sum(e1, axis=-1, keepdims=True))
        o = outs[0] - lam * outs[1]
        o = _rms(o) * g_ref[...] * (1.0 - lam_init)
        o_ref[0, :, vcols] = o.astype(BF16)


def _attn_ctx(geo, layer, q, kb, vb, lam_p, subln, lam_init):
    bc, lc = geo.bc, geo.lc
    seq = pl.BlockSpec((1, lc, DA_WIDTH), lambda b: (b, 0, 0))
    view = lambda a: a.reshape(geo.n // lc, lc, DA_WIDTH)
    out = pl.pallas_call(
        functools.partial(_attn_kernel, lam_init=lam_init, has_ctx=False),
        grid=(bc,),
        in_specs=[seq, seq, seq, _layer_spec(lam_p, layer), _layer_spec(subln, layer)],
        out_specs=seq,
        out_shape=jax.ShapeDtypeStruct((bc, lc, DA_WIDTH), BF16),
        compiler_params=_cparams(1),
        name="attn_ctx",
    )(view(q), view(kb), view(vb), lam_p, subln)
    return out.reshape(bc * lc, DA_WIDTH)


def _attn_lat(geo, layer, q, kb, vb, cache_k, cache_v, lam_p, subln, lam_init):
    bl, ll = geo.bl, geo.ll
    tq = TOKEN_BLOCK
    past = cache_k.shape[2]
    off = geo.nc // ll
    qblk = pl.BlockSpec((1, tq, DA_WIDTH), lambda b, j: (b + off, j, 0))
    seq = pl.BlockSpec((1, ll, DA_WIDTH), lambda b, j: (b + off, 0, 0))
    cache = pl.BlockSpec((1, 1, past, DA_WIDTH), lambda b, j: (b, layer, 0, 0))
    view = lambda a: a.reshape(geo.n // ll, ll, DA_WIDTH)
    out = pl.pallas_call(
        functools.partial(_attn_kernel, lam_init=lam_init, has_ctx=True),
        grid=(bl, ll // tq),
        in_specs=[qblk, seq, seq, cache, cache, _layer_spec(lam_p, layer), _layer_spec(subln, layer)],
        out_specs=pl.BlockSpec((1, tq, DA_WIDTH), lambda b, j: (b, j, 0)),
        out_shape=jax.ShapeDtypeStruct((bl, ll, DA_WIDTH), BF16),
        compiler_params=_cparams(2),
        name="attn_lat",
    )(view(q), view(kb), view(vb), cache_k, cache_v, lam_p, subln)
    return out.reshape(bl * ll, DA_WIDTH)


@functools.lru_cache(maxsize=None)
def _dft_tables(length):
    k = np.arange(length, dtype=np.int64)
    ks = (k[:, None] * k[None, :]) % (2 * length)
    ang = ks.astype(np.float64) * (np.pi / length)
    cmat = np.cos(ang)
    smat = -np.sin(ang)
    smat[0, :] = 1.0 - 2.0 * (k % 2)
    to_bf16 = lambda a: jnp.asarray(a.astype(np.float32)).astype(BF16)
    return to_bf16(cmat), to_bf16(smat), to_bf16(smat.T)


@functools.lru_cache(maxsize=None)
def _filter_features(length):
    t = np.linspace(0.0, 1.0, length, dtype=np.float32)
    w_ang = (2.0 * math.pi * np.arange(length, dtype=np.float32) / length).astype(np.float32)
    f = np.linspace(1e-4, HY_BANDS - 1, HY_BANDS, dtype=np.float32)
    arg = (w_ang[:, None] * f[None, :]).astype(np.float32).astype(np.float64)
    z = np.concatenate([t[:, None].astype(np.float64), np.cos(arg), -np.sin(arg)], axis=-1)
    z = np.pad(z, ((0, 0), (0, 64 - HY_EMB))).astype(np.float32)
    rev = np.concatenate([z[:1], z[:0:-1]], axis=0)
    deltas = np.linspace(math.log(HY_TARGET) / HY_FAST, math.log(HY_TARGET) / HY_SLOW,
                         HY_WIDTH, dtype=np.float32)
    return jnp.asarray(z), jnp.asarray(rev), jnp.asarray(np.abs(deltas)[None, :])


def _hyfilter_kernel(z_ref, zr_ref, dl_ref, w1_ref, b1_ref, w2_ref, b2_ref, w3_ref, fr_ref, o_ref):
    c = pl.program_id(0)
    fr = fr_ref[...]

    def mlp(z, w3):
        hid = jnp.sin(fr[0:1] * (_dot3(z, w1_ref[...]) + b1_ref[...]))
        hid = jnp.sin(fr[1:2] * (_dot3(hid, w2_ref[...]) + b2_ref[...]))
        return _dot3(hid, w3) * jnp.exp(-z[:, 0:1] * dl_ref[...])

    z, zr = z_ref[...], zr_ref[...]
    o_ref[0, :, 0:HY_WIDTH] = mlp(z, w3_ref[:, 0:HY_WIDTH]).astype(BF16)
    bwd = mlp(zr, w3_ref[:, HY_WIDTH:2 * HY_WIDTH])
    row = lax.broadcasted_iota(jnp.int32, bwd.shape, 0) + c * z.shape[0]
    o_ref[0, :, HY_WIDTH:2 * HY_WIDTH] = jnp.where(row == 0, 0.0, bwd).astype(BF16)


def _hyfilter(length, layer, w1p, b1, w2, b2, w3, fr):
    z, zr, dl = _filter_features(length)
    tb = TOKEN_BLOCK
    blk = pl.BlockSpec((tb, 64), lambda c: (c, 0))
    lay = lambda a: _layer_spec(a, layer)
    return pl.pallas_call(
        _hyfilter_kernel,
        grid=(length // tb,),
        in_specs=[blk, blk, pl.BlockSpec(dl.shape, lambda c: (0, 0)),
                  lay(w1p), lay(b1), lay(w2), lay(b2), lay(w3), lay(fr)],
        out_specs=pl.BlockSpec((1, tb, 2 * HY_WIDTH), lambda c: (0, c, 0)),
        out_shape=jax.ShapeDtypeStruct((1, length, 2 * HY_WIDTH), BF16),
        compiler_params=_cparams(1),
        name="hyena_filter",
    )(z, zr, dl, w1p, b1, w2, b2, w3, fr)


def _hyconv_kernel(main_ref, before_ref, after_ref, cw_ref, cb_ref, zb_ref, x0_ref, pad_ref, *, geo):
    i = pl.program_id(0)
    tb = TOKEN_BLOCK
    _fill_padded(pad_ref, before_ref, main_ref, after_ref, geo.is_start(i), geo.is_end(i))
    w = HY_WIDTH
    parts = []
    for p in range(3):
        cols = slice(p * w, (p + 1) * w)
        uc = (pad_ref[HALO - 1:HALO - 1 + tb, cols] * cw_ref[0:1, cols]
              + pad_ref[HALO:HALO + tb, cols] * cw_ref[1:2, cols]
              + pad_ref[HALO + 1:HALO + 1 + tb, cols] * cw_ref[2:3, cols]
              + cb_ref[:, cols])
        parts.append(uc)
    x0, x1, v = parts
    zb_ref[...] = (v * x1).astype(BF16)
    x0_ref[...] = x0.astype(BF16)


def _hyconv(geo, layer, u_hy, conv_w, conv_b):
    tb = TOKEN_BLOCK
    w3 = 3 * HY_WIDTH
    before, after = geo.halo_specs(w3)
    row = pl.BlockSpec((tb, HY_WIDTH), lambda i: (i, 0))
    return pl.pallas_call(
        functools.partial(_hyconv_kernel, geo=geo),
        grid=(geo.n // tb,),
        in_specs=[pl.BlockSpec((tb, w3), lambda i: (i, 0)), before, after,
                  _layer_spec(conv_w, layer), _layer_spec(conv_b, layer)],
        out_specs=[row, row],
        out_shape=[jax.ShapeDtypeStruct((geo.n, HY_WIDTH), BF16),
                   jax.ShapeDtypeStruct((geo.n, HY_WIDTH), BF16)],
        scratch_shapes=[pltpu.VMEM((tb + 2 * HALO, w3), F32)],
        compiler_params=_cparams(1),
        name="hyena_conv_gate",
    )(u_hy, u_hy, u_hy, conv_w, conv_b)


def _seqs_per_step(length, nseq):
    g = max(1, min(nseq, 2048 // length))
    assert nseq % g == 0
    return g


def _dft_fwd_kernel(c_ref, s_ref, x_ref, re_ref, im_ref):
    for s in range(x_ref.shape[0]):
        x = x_ref[s]
        re_ref[s] = _dot(c_ref[...], x)
        im_ref[s] = _dot(s_ref[...], x)


def _dft_fwd(length, x, nseq, seq_off):
    cmat, smat, _ = _dft_tables(length)
    n = x.shape[-1]
    tm = min(length, 512)
    tn = 512
    g = _seqs_per_step(length, nseq)
    assert seq_off % g == 0
    a_spec = pl.BlockSpec((tm, length), lambda b, j, m: (m, 0))
    o_spec = pl.BlockSpec((g, tm, tn), lambda b, j, m: (b, m, j))
    return pl.pallas_call(
        _dft_fwd_kernel,
        grid=(nseq // g, n // tn, length // tm),
        in_specs=[a_spec, a_spec, pl.BlockSpec((g, length, tn), lambda b, j, m: (b + seq_off // g, 0, j))],
        out_specs=[o_spec, o_spec],
        out_shape=[jax.ShapeDtypeStruct((nseq, length, n), F32)] * 2,
        compiler_params=_cparams(3),
        name="hyena_dft",
    )(cmat, smat, x)


def _hyprod_kernel(zr_ref, zi_ref, hfr_ref, hgr_ref, hfi_ref, hgi_ref, yr_ref, yi_ref, *, length):
    c = pl.program_id(1)
    shape = zr_ref.shape[1:]
    k = lax.broadcasted_iota(jnp.int32, shape, 0) + c * shape[0]
    sgn = (1 - 2 * (k % 2)).astype(F32)
    hr = hfr_ref[0] + sgn * hgr_ref[0]
    hi = hfi_ref[0] + sgn * hgi_ref[0]
    inv = 1.0 / length
    first = k == 0
    for s in range(zr_ref.shape[0]):
        zr, zi = zr_ref[s], zi_ref[s]
        yr = (zr * hr - zi * hi) * inv
        yi = (zr * hi + zi * hr) * inv
        yr_ref[s] = jnp.where(first, zr * hr * (0.5 * inv), yr).astype(BF16)
        yi_ref[s] = jnp.where(first, zi * hi * (0.5 * inv), yi).astype(BF16)


def _hyprod(length, zre, zim, hre, him):
    nb = zre.shape[0]
    tb = TOKEN_BLOCK
    g = _seqs_per_step(length, nb)
    zs = pl.BlockSpec((g, tb, HY_WIDTH), lambda b, c: (b, c, 0))
    hf = pl.BlockSpec((1, tb, HY_WIDTH), lambda b, c: (0, c, 0))
    hg = pl.BlockSpec((1, tb, HY_WIDTH), lambda b, c: (0, c, 1))
    return pl.pallas_call(
        functools.partial(_hyprod_kernel, length=length),
        grid=(nb // g, length // tb),
        in_specs=[zs, zs, hf, hg, hf, hg],
        out_specs=[zs, zs],
        out_shape=[jax.ShapeDtypeStruct((nb, length, HY_WIDTH), BF16)] * 2,
        compiler_params=_cparams(2),
        name="hyena_spectral_product",
    )(zre, zim, hre, hre, him, him)


def _hyinv_kernel(c_ref, st_ref, yr_ref, yi_ref, z_ref, x0_ref, skip_ref, o_ref):
    for s in range(yr_ref.shape[0]):
        y = _dot(c_ref[...], yr_ref[s]) + _dot(st_ref[...], yi_ref[s])
        o_ref[s] = ((y + z_ref[s].astype(F32) * skip_ref[...]) * x0_ref[s].astype(F32)).astype(BF16)


def _hyinv(length, layer, yr, yi, z, x0, skip, seq_off):
    cmat, _, smat_t = _dft_tables(length)
    nseq = yr.shape[0]
    tm = min(length, 512)
    g = _seqs_per_step(length, nseq)
    assert seq_off % g == 0
    a_spec = pl.BlockSpec((tm, length), lambda b, m: (m, 0))
    y_spec = pl.BlockSpec((g, length, HY_WIDTH), lambda b, m: (b, 0, 0))
    t_spec = pl.BlockSpec((g, tm, HY_WIDTH), lambda b, m: (b + seq_off // g, m, 0))
    return pl.pallas_call(
        _hyinv_kernel,
        grid=(nseq // g, length // tm),
        in_specs=[a_spec, a_spec, y_spec, y_spec, t_spec, t_spec, _layer_spec(skip, layer)],
        out_specs=pl.BlockSpec((g, tm, HY_WIDTH), lambda b, m: (b, m, 0)),
        out_shape=jax.ShapeDtypeStruct((nseq, length, HY_WIDTH), BF16),
        compiler_params=_cparams(2),
        name="hyena_idft",
    )(cmat, smat_t, yr, yi, z, x0, skip)


def _hyena_long_conv(length, layer, nseq, seq_off, zb, x0, filt, skip):
    view = lambda a: a.reshape(a.shape[0] // length, length, HY_WIDTH)
    zre, zim = _dft_fwd(length, view(zb), nseq, seq_off)
    hre, him = _dft_fwd(length, filt, 1, 0)
    yr, yi = _hyprod(length, zre, zim, hre, him)
    out = _hyinv(length, layer, yr, yi, view(zb), view(x0), skip, seq_off)
    return out.reshape(nseq * length, HY_WIDTH)


def _merge_kernel(x_ref, mod_ref, g1_ref, a_ref, bc_ref, bl_ref, cc_ref, cl_ref, wg_ref, bg_ref, wbr_ref,
                  wo_ref, g2_ref, wr_ref, br_ref, xo_ref, h2_ref, ti_ref, tp_ref, *, nbc):
    d = D_MODEL
    is_ctx = pl.program_id(0) < nbc
    mod = mod_ref[0]
    shift1, scale1, gate1 = mod[:, 0:d], mod[:, d:2 * d], mod[:, 2 * d:3 * d]
    shift2, scale2 = mod[:, 3 * d:4 * d], mod[:, 4 * d:5 * d]
    x = x_ref[...]
    h = (_rms(x) * g1_ref[...] * (1.0 + scale1) + shift1).astype(BF16)
    branches = (a_ref[...],
                jnp.where(is_ctx, bc_ref[...], bl_ref[...]),
                jnp.where(is_ctx, cc_ref[...], cl_ref[...]))
    merged = None
    for n, br_n in enumerate(branches):
        cols = slice(n * d, (n + 1) * d)
        gate = _sigmoid(_dot(h, wg_ref[:, cols]) + bg_ref[:, cols])
        term = gate * _dot(br_n, wbr_ref[n])
        merged = term if merged is None else merged + term
    x = x + gate1 * _dot(merged.astype(BF16), wo_ref[...])
    xo_ref[...] = x
    h2 = _rms(x) * g2_ref[...] * (1.0 + scale2) + shift2
    h2_ref[...] = h2.astype(BF16)

    logits = _dot3(h2, wr_ref[...]) + br_ref[...]
    ne = logits.shape[-1]
    lane = lax.broadcasted_iota(jnp.int32, logits.shape, 1).astype(F32)
    vals = logits
    top_v, top_i = [], []
    for _ in range(TOP_K):
        m = jnp.max(vals, axis=-1, keepdims=True)
        idx = jnp.min(jnp.where(vals == m, lane, float(ne)), axis=-1, keepdims=True)
        top_v.append(m)
        top_i.append(idx)
        vals = jnp.where(lane == idx, -jnp.inf, vals)
    es = [jnp.exp(v - top_v[0]) for v in top_v]
    den = es[0] + es[1] + es[2] + es[3]
    for kk in range(TOP_K):
        ti_ref[:, kk:kk + 1] = top_i[kk].astype(jnp.int32)
        tp_ref[:, kk:kk + 1] = es[kk] / den


def _merge(geo, layer, x, mods, g1, a_out, b_ctx, b_lat, c_ctx, c_lat, wg_b, bg, wbr_b, wo_b, g2, wr, br):
    d = D_MODEL
    tb = WIDE_BLOCK
    row = lambda w: pl.BlockSpec((tb, w), lambda i: (i, 0))
    ctx, lat = geo.split_specs(tb, DA_WIDTH)
    lay = lambda a: _layer_spec(a, layer, single_buffer=True)
    return pl.pallas_call(
        functools.partial(_merge_kernel, nbc=geo.nc // tb),
        grid=(geo.n // tb,),
        in_specs=[row(d), geo.mod_spec(layer, tb), lay(g1), row(POOL_WIDTH), ctx, lat, ctx, lat,
                  lay(wg_b), lay(bg), lay(wbr_b), lay(wo_b), lay(g2), lay(wr), lay(br)],
        out_specs=[row(d), row(d), row(TOP_K), row(TOP_K)],
        out_shape=[jax.ShapeDtypeStruct((geo.n, d), F32),
                   jax.ShapeDtypeStruct((geo.n, d), BF16),
                   jax.ShapeDtypeStruct((geo.n, TOP_K), jnp.int32),
                   jax.ShapeDtypeStruct((geo.n, TOP_K), F32)],
        compiler_params=_cparams(1),
        name="merge_route",
    )(x, mods, g1, a_out, b_ctx, b_lat, c_ctx, c_lat, wg_b, bg, wbr_b, wo_b, g2, wr, br)


def _ffn_kernel(sw_ref, st_ref, sn_ref, nv_ref, xs_hbm, wgu_ref, bgu_ref, wd_ref, bd_ref, ys_hbm,
                xbuf, ybuf, semx, semy):
    i = pl.program_id(0)
    nv = nv_ref[0]
    slot = i % 2
    t = MOE_TILE
    half = D_FF // 2

    def rows_of(step, k):
        return pl.ds(pl.multiple_of(st_ref[step] * t, t), k * t)

    def x_copy(step, buf, k):
        return pltpu.make_async_copy(xs_hbm.at[rows_of(step, k)], xbuf.at[buf, pl.ds(0, k * t)], semx.at[buf])

    def y_copy(step, buf, k):
        return pltpu.make_async_copy(ybuf.at[buf, pl.ds(0, k * t)], ys_hbm.at[rows_of(step, k)], semy.at[buf])

    def for_size(step, fn):
        n = sn_ref[step]
        for k in range(1, MOE_MAX_SUB + 1):
            @pl.when(n == k)
            def _(k=k):
                fn(k)

    def compute(k):
        m = k * t
        x = xbuf[slot, 0:m, :]
        y = None
        for c in range(2):
            gcols = slice(c * half, (c + 1) * half)
            ucols = slice(D_FF + c * half, D_FF + (c + 1) * half)
            gate = _dot(x, wgu_ref[0, :, gcols].astype(BF16)) + bgu_ref[0, :, gcols]
            up = _dot(x, wgu_ref[0, :, ucols].astype(BF16)) + bgu_ref[0, :, ucols]
            gate = jnp.minimum(gate, SWIGLU_LIMIT)
            up = jnp.clip(up, -SWIGLU_LIMIT, SWIGLU_LIMIT)
            act = ((up + 1.0) * gate * _sigmoid(SWIGLU_ALPHA * gate)).astype(BF16)
            part = _dot(act, wd_ref[0, gcols, :].astype(BF16))
            y = part if y is None else y + part
        ybuf[slot, 0:m, :] = (y + bd_ref[0]).astype(BF16)
        y_copy(i, slot, k).start()

    @pl.when(i < nv)
    def _():
        @pl.when(i == 0)
        def _():
            for_size(0, lambda k: x_copy(0, 0, k).start())

        for_size(i, lambda k: x_copy(i, slot, k).wait())

        @pl.when(i + 1 < nv)
        def _():
            for_size(i + 1, lambda k: x_copy(i + 1, 1 - slot, k).start())

        @pl.when(i >= 2)
        def _():
            for_size(i - 2, lambda k: y_copy(i - 2, slot, k).wait())

        for_size(i, compute)

        @pl.when(i == nv - 1)
        def _():
            for_size(i, lambda k: y_copy(i, slot, k).wait())

            @pl.when(i >= 1)
            def _():
                for_size(i - 1, lambda k: y_copy(i - 1, 1 - slot, k).wait())


def _ffn(xs, step_weight, step_tile, step_sub, n_valid, w_gu, b_gu, w_down, b_down):
    p = xs.shape[0]
    d = D_MODEL
    t = MOE_TILE
    hbm = pl.BlockSpec(memory_space=pl.ANY)
    wmap = lambda i, sw, st, sn, nv: (sw[i], 0, 0)
    grid_spec = pltpu.PrefetchScalarGridSpec(
        num_scalar_prefetch=4,
        grid=(step_weight.shape[0],),
        in_specs=[hbm,
                  pl.BlockSpec((1, d, 2 * D_FF), wmap), pl.BlockSpec((1, 1, 2 * D_FF), wmap),
                  pl.BlockSpec((1, D_FF, d), wmap), pl.BlockSpec((1, 1, d), wmap)],
        out_specs=hbm,
        scratch_shapes=[pltpu.VMEM((2, MOE_MAX_SUB * t, d), BF16), pltpu.VMEM((2, MOE_MAX_SUB * t, d), BF16),
                        pltpu.SemaphoreType.DMA((2,)), pltpu.SemaphoreType.DMA((2,))])
    return pl.pallas_call(
        _ffn_kernel,
        grid_spec=grid_spec,
        out_shape=jax.ShapeDtypeStruct((p, d), BF16),
        compiler_params=_cparams(1),
        name="moe_experts",
    )(step_weight, step_tile, step_sub, n_valid, xs, w_gu, b_gu, w_down, b_down)


def _moe_rows(n, n_experts):
    worst = n * TOP_K + n_experts * ((n // DISPATCH_BLOCK) * (RUN_ALIGN - 1) + MOE_TILE - 1)
    return -(-worst // MOE_TILE) * MOE_TILE


def _route(top_i, n_experts):
    n = top_i.shape[0]
    t = MOE_TILE
    nb = n // DISPATCH_BLOCK
    experts = jnp.arange(n_experts, dtype=jnp.int32)
    onehot = top_i.reshape(nb, DISPATCH_BLOCK * TOP_K)[:, :, None] == experts[None, None, :]
    n_be = jnp.sum(onehot, axis=1, dtype=jnp.int32)
    cap = (n_be + RUN_ALIGN - 1) // RUN_ALIGN * RUN_ALIGN
    rows_e = jnp.sum(cap, axis=0)
    tiles_e = (rows_e + t - 1) // t
    tile_end = jnp.cumsum(tiles_e)
    tile_start = tile_end - tiles_e
    run_dst = tile_start[None, :] * t + jnp.cumsum(cap, axis=0) - cap
    run_loc = jnp.cumsum(cap, axis=1) - cap
    tail_dst = tile_start * t + rows_e
    tail = tiles_e * t - rows_e

    def owner(ends, ids):
        return jnp.sum(ends[None, :] <= ids[:, None], axis=1, dtype=jnp.int32)[:, None] == experts[None, :]

    pick = lambda oh, table: jnp.sum(jnp.where(oh, table[None, :], 0), axis=1)

    steps_e = (tiles_e + MOE_MAX_SUB - 1) // MOE_MAX_SUB
    step_end = jnp.cumsum(steps_e)
    n_steps = _moe_rows(n, n_experts) // (t * MOE_MAX_SUB) + n_experts
    n_valid = step_end[-1]
    step_ids = jnp.minimum(jnp.arange(n_steps, dtype=jnp.int32), n_valid - 1)
    step_oh = owner(step_end, step_ids)
    j = step_ids - pick(step_oh, step_end - steps_e)
    step_expert = jnp.argmax(step_oh, axis=1).astype(jnp.int32)
    n_st = jnp.maximum(pick(step_oh, steps_e), 1)
    n_ti = pick(step_oh, tiles_e)
    base, rem = n_ti // n_st, n_ti % n_st
    extra = jnp.maximum(j - (n_st - rem), 0)
    step_tile = pick(step_oh, tile_start) + j * base + extra
    step_sub = jnp.clip(base + (j >= n_st - rem).astype(jnp.int32), 1, MOE_MAX_SUB)
    units = lambda a: (a // RUN_ALIGN).reshape(-1).astype(jnp.int32)
    runs = (units(run_dst), units(run_loc), units(cap), units(jnp.sum(cap, axis=1)), units(tail_dst), units(tail))
    cols = (run_dst.reshape(nb, n_experts, 1).astype(F32), run_loc.reshape(nb, n_experts, 1).astype(F32))
    steps = (step_expert, step_tile.astype(jnp.int32), step_sub.astype(jnp.int32),
             n_valid.reshape(1).astype(jnp.int32))
    return runs, cols, steps


def _sized_copies(units, src_at, dst_at, sem, max_bits, fn):
    for bit in range(max_bits):
        size = RUN_ALIGN << bit

        @pl.when((units >> bit) & 1 == 1)
        def _(bit=bit, size=size):
            off = (units & ((1 << bit) - 1)) * RUN_ALIGN
            fn(pltpu.make_async_copy(src_at(off, size), dst_at(off, size), sem))


def _dispatch_kernel(rdst_ref, rloc_ref, rcap_ref, rtot_ref, tdst_ref, tlen_ref,
                     h_ref, tt_ref, dcol_ref, lcol_ref, xs_hbm, slot_ref, stage, zeros, sem, zsem,
                     *, n_experts):
    b = pl.program_id(0)
    nb = pl.num_programs(0)
    tb = DISPATCH_BLOCK
    buf = b % 2
    run_bits = (tb // RUN_ALIGN).bit_length()
    tail_bits = (MOE_TILE // RUN_ALIGN).bit_length()
    stage_bits = (stage.shape[1] // RUN_ALIGN).bit_length()

    def drain_runs(blk, slot):
        _sized_copies(rtot_ref[blk],
                      lambda off, size: stage.at[slot, pl.ds(0, size)],
                      lambda off, size: xs_hbm.at[pl.ds(0, size)],
                      sem.at[slot], stage_bits, lambda cp: cp.wait())

    def run_copies(blk, slot, fn):
        def body(e, carry):
            idx = blk * n_experts + e
            src0 = pl.multiple_of(rloc_ref[idx] * RUN_ALIGN, RUN_ALIGN)
            dst0 = pl.multiple_of(rdst_ref[idx] * RUN_ALIGN, RUN_ALIGN)
            _sized_copies(rcap_ref[idx],
                          lambda off, size: stage.at[slot, pl.ds(pl.multiple_of(src0 + off, RUN_ALIGN), size)],
                          lambda off, size: xs_hbm.at[pl.ds(pl.multiple_of(dst0 + off, RUN_ALIGN), size)],
                          sem.at[slot], run_bits, fn)
            return carry
        lax.fori_loop(0, n_experts, body, 0)

    def tail_copies(fn):
        def body(e, carry):
            dst0 = pl.multiple_of(tdst_ref[e] * RUN_ALIGN, RUN_ALIGN)
            _sized_copies(tlen_ref[e],
                          lambda off, size: zeros.at[pl.ds(0, size)],
                          lambda off, size: xs_hbm.at[pl.ds(pl.multiple_of(dst0 + off, RUN_ALIGN), size)],
                          zsem, tail_bits, fn)
            return carry
        lax.fori_loop(0, n_experts, body, 0)

    @pl.when(b == 0)
    def _():
        zeros[...] = jnp.zeros(zeros.shape, zeros.dtype)
        tail_copies(lambda cp: cp.start())

    @pl.when(b >= 2)
    def _():
        drain_runs(b - 2, buf)

    tt = tt_ref[...]
    e_iota = lax.broadcasted_iota(jnp.int32, (n_experts, tb), 0)
    hit = [e_iota == tt[k:k + 1, :] for k in range(TOP_K)]
    member = jnp.where(hit[0] | hit[1] | hit[2] | hit[3], 1.0, 0.0)
    before = (lax.broadcasted_iota(jnp.int32, (tb, tb), 0)
              < lax.broadcasted_iota(jnp.int32, (tb, tb), 1))
    rank = _dot(member.astype(BF16), jnp.where(before, 1.0, 0.0).astype(BF16))
    loc = [jnp.sum(jnp.where(hit[k], lcol_ref[0] + rank, 0.0), axis=0, keepdims=True) for k in range(TOP_K)]
    for k in range(TOP_K):
        slot_ref[k:k + 1, :] = jnp.sum(jnp.where(hit[k], dcol_ref[0] + rank, 0.0),
                                       axis=0, keepdims=True).astype(jnp.int32)

    rows = stage.shape[1]
    r_iota = lax.broadcasted_iota(jnp.int32, (rows, tb), 0).astype(F32)
    place = jnp.zeros((rows, tb), F32)
    for k in range(TOP_K):
        place = jnp.where(r_iota == loc[k], 1.0, place)
    stage[buf] = _dot(place.astype(BF16), h_ref[...]).astype(BF16)
    run_copies(b, buf, lambda cp: cp.start())

    @pl.when(b == nb - 1)
    def _():
        drain_runs(b, buf)

        @pl.when(b >= 1)
        def _():
            drain_runs(b - 1, 1 - buf)
        tail_copies(lambda cp: cp.wait())


def _dispatch(h2, top_t, runs, cols, n_experts):
    n, d = h2.shape
    tb = DISPATCH_BLOCK
    stage_rows = tb * TOP_K + n_experts * RUN_ALIGN
    col = pl.BlockSpec((1, n_experts, 1), lambda b, *_: (b, 0, 0))
    grid_spec = pltpu.PrefetchScalarGridSpec(
        num_scalar_prefetch=6,
        grid=(n // tb,),
        in_specs=[pl.BlockSpec((tb, d), lambda b, *_: (b, 0)),
                  pl.BlockSpec((TOP_K, tb), lambda b, *_: (0, b)), col, col],
        out_specs=[pl.BlockSpec(memory_space=pl.ANY), pl.BlockSpec((TOP_K, tb), lambda b, *_: (0, b))],
        scratch_shapes=[pltpu.VMEM((2, stage_rows, d), BF16), pltpu.VMEM((MOE_TILE, d), BF16),
                        pltpu.SemaphoreType.DMA((2,)), pltpu.SemaphoreType.DMA(())])
    return pl.pallas_call(
        functools.partial(_dispatch_kernel, n_experts=n_experts),
        grid_spec=grid_spec,
        out_shape=[jax.ShapeDtypeStruct((_moe_rows(n, n_experts), d), BF16),
                   jax.ShapeDtypeStruct((TOP_K, n), jnp.int32)],
        compiler_params=_cparams(1),
        name="moe_dispatch",
    )(*runs, h2, top_t, *cols)


def _moe_sum(x_ref, mod_ref, yg_ref, p_ref):
    d = D_MODEL
    gate2 = mod_ref[0][:, 5 * d:6 * d]
    p = p_ref[...]
    moe = p[:, 0:1] * yg_ref[0].astype(F32)
    for kk in range(1, TOP_K):
        moe = moe + p[:, kk:kk + 1] * yg_ref[kk].astype(F32)
    return x_ref[...] + gate2 * moe


def _combine_final_kernel(x_ref, mod_ref, yg_ref, p_ref, gf_ref, yc_ref, yl_ref, *, nbc):
    i = pl.program_id(0)
    y = _rms(_moe_sum(x_ref, mod_ref, yg_ref, p_ref)) * gf_ref[...]

    @pl.when(i < nbc)
    def _():
        yc_ref[...] = y

    @pl.when(i >= nbc)
    def _():
        yl_ref[...] = y


def _combine(geo, layer, x, mods, yg, top_p, final_g):
    d = D_MODEL
    tb = TOKEN_BLOCK
    row = pl.BlockSpec((tb, d), lambda i: (i, 0))
    in_specs = [row, geo.mod_spec(layer, tb),
                pl.BlockSpec((TOP_K, tb, d), lambda i: (0, i, 0)),
                pl.BlockSpec((tb, TOP_K), lambda i: (i, 0))]
    ctx, lat = geo.split_specs(tb, d)
    return pl.pallas_call(
        functools.partial(_combine_final_kernel, nbc=geo.nc // tb),
        grid=(geo.n // tb,),
        in_specs=in_specs + [pl.BlockSpec(final_g.shape, lambda i: (0, 0))],
        out_specs=[ctx, lat],
        out_shape=[jax.ShapeDtypeStruct((geo.nc, d), F32), jax.ShapeDtypeStruct((geo.nl, d), F32)],
        compiler_params=_cparams(1), name="moe_combine_final",
    )(x, mods, yg, top_p, final_g)


@functools.lru_cache(maxsize=None)
def _rope_tables(length):
    rows = length // GRID_W
    row = np.repeat(np.arange(rows), GRID_W).astype(np.float32)
    col = np.tile(np.arange(GRID_W), rows).astype(np.float32)
    ax = DA_HEAD_DIM // 2
    inv = (ROPE_BASE ** (-(np.arange(ax // 2, dtype=np.float32) * 2.0 / ax))).astype(np.float32)
    ang_r = (row[:, None] * inv).astype(np.float32)
    ang_c = (col[:, None] * inv).astype(np.float32)
    ang = np.concatenate([ang_r, ang_r, ang_c, ang_c], axis=-1).astype(np.float64)
    sign = np.where((np.arange(DA_HEAD_DIM) % 32) < 16, -1.0, 1.0)
    reps = DA_WIDTH // DA_HEAD_DIM
    cos = np.tile(np.cos(ang), (1, reps)).astype(np.float32)
    sin_signed = np.tile(np.sin(ang) * sign[None, :], (1, reps)).astype(np.float32)
    return jnp.asarray(cos), jnp.asarray(sin_signed)


def kernel(x_prompt, x_sample, cache_k, cache_v, c, c_ctx, w_ada, b_ada, norm1, norm2, w_in, w_pool, pool_scale, da_lambda, da_subln, hy_conv_w, hy_conv_b, hy_f_w1, hy_f_b1, hy_f_w2, hy_f_b2, hy_f_w3, hy_sin_freq, hy_skip, w_branch, w_gate, b_gate, w_o, w_router, b_router, w_gu, b_gu, w_down, b_down, final_norm):
    bc, lc, d = x_prompt.shape
    bl, ll, _ = x_sample.shape
    depth = w_in.shape[0]
    n_experts = w_router.shape[-1]
    past = cache_k.shape[2]
    geo = _Geom(bc, lc, bl, ll)
    assert 1 + bl <= MOD_ROWS

    x = jnp.concatenate([x_prompt.reshape(bc * lc, d), x_sample.reshape(bl * ll, d)], axis=0)
    cond = jnp.concatenate([c_ctx[None], c, jnp.zeros((MOD_ROWS - 1 - bl, d), F32)], axis=0)
    mods = _ada(cond, w_ada, b_ada).reshape(depth * MOD_ROWS, 1, N_MOD * d)
    cos_t, sin_t = _rope_tables(ll)
    ck = cache_k.reshape(bl, depth, past, DA_WIDTH)
    cv = cache_v.reshape(bl, depth, past, DA_WIDTH)

    row3 = lambda a: a.reshape(depth, 1, a.shape[-1])
    g1, g2 = row3(norm1), row3(norm2)
    w_in_b = w_in.astype(BF16)
    w_pool_b = w_pool.astype(BF16)
    w_gate_b = w_gate.astype(BF16)
    w_branch_b = w_branch.astype(BF16)
    w_o_b = w_o.astype(BF16)
    w1p = jnp.pad(hy_f_w1, ((0, 0), (0, 64 - HY_EMB), (0, 0)))
    fargs = (w1p, row3(hy_f_b1), hy_f_w2, row3(hy_f_b2), hy_f_w3, hy_sin_freq)
    w_gu_s = w_gu.reshape(depth * n_experts, d, 2 * D_FF)
    b_gu_s = b_gu.reshape(depth * n_experts, 1, 2 * D_FF)
    w_down_s = w_down.reshape(depth * n_experts, D_FF, d)
    b_down_s = b_down.reshape(depth * n_experts, 1, d)

    new_k = jnp.zeros((bc, depth, lc * DA_HEADS, DA_VDIM), F32)
    new_v = jnp.zeros((bc, depth, lc * DA_HEADS, DA_VDIM), F32)
    moe = None
    for l in range(depth):
        x, u_pool, q, kb, vb, new_k, new_v, u_hy = _inproj(geo, l, x, mods, g1, w_in_b, cos_t, sin_t,
                                                           new_k, new_v, moe)
        a_out = _pool(geo, l, u_pool, w_pool_b, row3(pool_scale))

        lam_init = 0.8 - 0.6 * math.exp(-0.3 * l)
        subln = row3(da_subln)
        b_ctx = _attn_ctx(geo, l, q, kb, vb, da_lambda, subln, lam_init)
        b_lat = _attn_lat(geo, l, q, kb, vb, ck, cv, da_lambda, subln, lam_init)

        zb, x0 = _hyconv(geo, l, u_hy, hy_conv_w, row3(hy_conv_b))
        skip = row3(hy_skip)
        c_ctx_out = _hyena_long_conv(lc, l, bc, 0, zb, x0, _hyfilter(lc, l, *fargs), skip)
        c_lat_out = _hyena_long_conv(ll, l, bl, geo.nc // ll, zb, x0, _hyfilter(ll, l, *fargs), skip)

        x, h2, top_i, top_p = _merge(geo, l, x, mods, g1, a_out, b_ctx, b_lat, c_ctx_out, c_lat_out,
                                     w_gate_b, row3(b_gate), w_branch_b, w_o_b, g2, w_router, row3(b_router))

        runs, cols, (step_expert, step_tile, step_sub, n_valid) = _route(top_i, n_experts)
        xs, pair_slot = _dispatch(h2, top_i.T, runs, cols, n_experts)
        ys = _ffn(xs, step_expert + l * n_experts, step_tile, step_sub, n_valid,
                  w_gu_s, b_gu_s, w_down_s, b_down_s)
        yg = jnp.take(ys, pair_slot, axis=0, mode="clip")
        moe = (yg, top_p)

    y_ctx, y_lat = _combine(geo, depth - 1, x, mods, yg, top_p, final_norm[None])

    return (y_ctx.reshape(bc, lc, d), y_lat.reshape(bl, ll, d),
            new_k.reshape(bc, depth, lc, DA_HEADS, 2 * DA_HEAD_DIM),
            new_v.reshape(bc, depth, lc, DA_HEADS, DA_VDIM))
```

```python
import functools
import math

import numpy as np
import jax
import jax.numpy as jnp
from jax import lax
from jax.experimental import pallas as pl
from jax.experimental.pallas import tpu as pltpu

F32 = jnp.float32
BF16 = jnp.bfloat16

D_MODEL = 1024
GRID_W = 64
NORM_EPS = 1e-6
POOL_WIDTH = 512
POOL_WINDOWS = (2, 4, 8, 16)
POOL_GC = POOL_WIDTH // len(POOL_WINDOWS)
DA_HEADS = 4
DA_HEAD_DIM = 64
DA_VDIM = 2 * DA_HEAD_DIM
DA_WIDTH = DA_HEADS * DA_VDIM
ROPE_BASE = 10000.0
HY_WIDTH = 512
HY_EMB = 33
HY_BANDS = (HY_EMB - 1) // 2
HY_HIDDEN = 64
HY_FAST = 0.3
HY_SLOW = 1.5
HY_TARGET = 1e-2
N_BRANCH = 3
D_IN = POOL_WIDTH + 3 * DA_WIDTH + 3 * HY_WIDTH
TOP_K = 4
D_FF = 1024
SWIGLU_ALPHA = 1.702
SWIGLU_LIMIT = 7.0
N_MOD = 6
MOD_ROWS = 8

TOKEN_BLOCK = 256
WIDE_BLOCK = 512
HALO = 16
MOE_TILE = 128
MOE_MAX_SUB = 8
DISPATCH_BLOCK = 512
RUN_ALIGN = 16
V7X_VMEM_LIMIT = 56 * 1024 * 1024


def _cparams(n_axes):
    return pltpu.CompilerParams(
        dimension_semantics=("arbitrary",) * n_axes,
        vmem_limit_bytes=V7X_VMEM_LIMIT)


def _dot(a, b):
    return jnp.dot(a, b, preferred_element_type=F32)


def _dot_nt(a, b):
    return lax.dot_general(a, b, (((1,), (1,)), ((), ())), preferred_element_type=F32)


def _split_bf16(a):
    hi = a.astype(BF16)
    lo = (a - hi.astype(F32)).astype(BF16)
    return hi, lo


def _dot3(a, b):
    ah, al = _split_bf16(a)
    bh, bl = _split_bf16(b)
    return _dot(ah, bh) + _dot(al, bh) + _dot(ah, bl)


def _sigmoid(x):
    return 0.5 * jnp.tanh(0.5 * x) + 0.5


def _rms(x):
    return x * lax.rsqrt(jnp.mean(x * x, axis=-1, keepdims=True) + NORM_EPS)


class _Geom:
    def __init__(self, n_ctx_seq, ctx_len, n_lat_seq, lat_len):
        assert ctx_len == TOKEN_BLOCK, "one context sequence per token block"
        assert lat_len % WIDE_BLOCK == 0 and (n_ctx_seq * ctx_len) % lat_len == 0
        self.bc, self.lc, self.bl, self.ll = n_ctx_seq, ctx_len, n_lat_seq, lat_len
        self.nc = n_ctx_seq * ctx_len
        self.nl = n_lat_seq * lat_len
        self.n = self.nc + self.nl

    def group(self, i, tb):
        nbc = self.nc // tb
        return jnp.where(i < nbc, 0, 1 + (i - nbc) // (self.ll // tb))

    def pos_block(self, i, tb):
        nbc = self.nc // tb
        return jnp.where(i < nbc, 0, (i - nbc) % (self.ll // tb))

    def is_start(self, i):
        nbc, bpl = self.nc // TOKEN_BLOCK, self.ll // TOKEN_BLOCK
        return jnp.logical_or(i < nbc, (i - nbc) % bpl == 0)

    def is_end(self, i):
        nbc, bpl = self.nc // TOKEN_BLOCK, self.ll // TOKEN_BLOCK
        return jnp.logical_or(i < nbc, (i - nbc) % bpl == bpl - 1)

    def mod_spec(self, layer, tb):
        return pl.BlockSpec((1, 1, N_MOD * D_MODEL),
                            lambda i: (layer * MOD_ROWS + self.group(i, tb), 0, 0))

    def halo_specs(self, width):
        per = TOKEN_BLOCK // HALO
        last = self.n // HALO - 1
        before = pl.BlockSpec((HALO, width), lambda i: (jnp.maximum(i * per - 1, 0), 0))
        after = pl.BlockSpec((HALO, width), lambda i: (jnp.minimum((i + 1) * per, last), 0))
        return before, after

    def split_specs(self, tb, width):
        nbc = self.nc // tb
        last_lat = self.nl // tb - 1
        ctx = pl.BlockSpec((tb, width), lambda i: (jnp.minimum(i, nbc - 1), 0))
        lat = pl.BlockSpec((tb, width), lambda i: (jnp.clip(i - nbc, 0, last_lat), 0))
        return ctx, lat


def _layer_spec(a, layer, single_buffer=False):
    kw = dict(pipeline_mode=pl.Buffered(1)) if single_buffer else {}
    return pl.BlockSpec((None,) + a.shape[1:], lambda *_: (layer,) + (0,) * (a.ndim - 1), **kw)


def _ada_kernel(c_ref, w_ref, b_ref, o_ref):
    c = c_ref[...]
    s = c * _sigmoid(c)
    o_ref[0] = _dot(s.astype(BF16), w_ref[0].astype(BF16)) + b_ref[0]


def _ada(cond, w_ada, b_ada):
    depth, d, n6 = w_ada.shape
    rows = cond.shape[0]
    tn = 1024
    return pl.pallas_call(
        _ada_kernel,
        grid=(depth, n6 // tn),
        in_specs=[pl.BlockSpec((rows, d), lambda l, j: (0, 0)),
                  pl.BlockSpec((1, d, tn), lambda l, j: (l, 0, j)),
                  pl.BlockSpec((1, 1, tn), lambda l, j: (l, 0, j))],
        out_specs=pl.BlockSpec((1, rows, tn), lambda l, j: (l, 0, j)),
        out_shape=jax.ShapeDtypeStruct((depth, rows, n6), F32),
        compiler_params=_cparams(2),
        name="ada_mod",
    )(cond, w_ada, b_ada.reshape(depth, 1, n6))


def _rope(x, cos, sin_signed, first_half):
    d = x.shape[-1]
    partner = jnp.where(first_half, pltpu.roll(x, d - 16, 1), pltpu.roll(x, 16, 1))
    return x * cos + partner * sin_signed


def _inproj_kernel(*refs, nbc, lc, add_moe):
    if add_moe:
        x_ref, pmod_ref, yg_ref, p_ref = refs[:4]
        refs = refs[4:]
        (mod_ref, g_ref, w_ref, cos_ref, sin_ref, _, _,
         xo_ref, up_ref, q_ref, kb_ref, vb_ref, kf_ref, vf_ref, uh_ref) = refs
        x = _moe_sum(x_ref, pmod_ref, yg_ref, p_ref)
        xo_ref[...] = x
    else:
        (x_ref, mod_ref, g_ref, w_ref, cos_ref, sin_ref, _, _,
         up_ref, q_ref, kb_ref, vb_ref, kf_ref, vf_ref, uh_ref) = refs
        x = x_ref[...]
    i = pl.program_id(0)
    d = D_MODEL
    mod = mod_ref[0]
    shift, scale = mod[:, 0:d], mod[:, d:2 * d]
    h = (_rms(x) * g_ref[...] * (1.0 + scale) + shift).astype(BF16)

    c1 = POOL_WIDTH
    c2 = c1 + DA_WIDTH
    c3 = c2 + DA_WIDTH
    c4 = c3 + DA_WIDTH
    up_ref[...] = _dot(h, w_ref[:, 0:c1])
    uh_ref[...] = _dot(h, w_ref[:, c4:D_IN]).astype(BF16)
    q = _dot(h, w_ref[:, c1:c2]) * (DA_HEAD_DIM ** -0.5 * math.log2(math.e))
    k = _dot(h, w_ref[:, c2:c3])
    v = _dot(h, w_ref[:, c3:c4])
    vb_ref[...] = v.astype(BF16)

    @pl.when(i < nbc)
    def _():
        q_ref[...] = q.astype(BF16)
        kb_ref[...] = k.astype(BF16)
        for s in range(k.shape[0] // lc):
            for hh in range(DA_HEADS):
                rows = pl.ds(hh, lc, stride=DA_HEADS)
                cols = slice(hh * DA_VDIM, (hh + 1) * DA_VDIM)
                kf_ref[s, 0, rows, :] = k[s * lc:(s + 1) * lc, cols]
                vf_ref[s, 0, rows, :] = v[s * lc:(s + 1) * lc, cols]

    @pl.when(i >= nbc)
    def _():
        cos, sin_signed = cos_ref[...], sin_ref[...]
        lane = lax.broadcasted_iota(jnp.int32, q.shape, 1)
        first_half = (lane % 32) < 16
        q_ref[...] = _rope(q, cos, sin_signed, first_half).astype(BF16)
        kb_ref[...] = _rope(k, cos, sin_signed, first_half).astype(BF16)


def _inproj(geo, layer, x, mods, g1, w_in_b, cos_t, sin_t, kacc, vacc, moe=None):
    d = D_MODEL
    tb = WIDE_BLOCK
    nbc = geo.nc // tb
    spb = tb // geo.lc
    row = lambda w: pl.BlockSpec((tb, w), lambda i: (i, 0))
    cache = pl.BlockSpec((spb, 1, geo.lc * DA_HEADS, DA_VDIM), lambda i: (jnp.minimum(i, nbc - 1), layer, 0, 0))
    tab = pl.BlockSpec((tb, DA_WIDTH), lambda i: (geo.pos_block(i, tb), 0))
    hbm = pl.BlockSpec(memory_space=pl.ANY)
    in_specs = [geo.mod_spec(layer, tb), _layer_spec(g1, layer),
                _layer_spec(w_in_b, layer, single_buffer=True), tab, tab, hbm, hbm]
    args = [mods, g1, w_in_b, cos_t, sin_t, kacc, vacc]
    out_specs = [row(POOL_WIDTH), row(DA_WIDTH), row(DA_WIDTH), row(DA_WIDTH), cache, cache, row(3 * HY_WIDTH)]
    out_shape = [jax.ShapeDtypeStruct((geo.n, POOL_WIDTH), F32),
                 jax.ShapeDtypeStruct((geo.n, DA_WIDTH), BF16),
                 jax.ShapeDtypeStruct((geo.n, DA_WIDTH), BF16),
                 jax.ShapeDtypeStruct((geo.n, DA_WIDTH), BF16),
                 jax.ShapeDtypeStruct(kacc.shape, F32),
                 jax.ShapeDtypeStruct(vacc.shape, F32),
                 jax.ShapeDtypeStruct((geo.n, 3 * HY_WIDTH), BF16)]
    if moe is None:
        in_specs = [row(d)] + in_specs
        args = [x] + args
    else:
        yg, top_p = moe
        in_specs = [row(d), geo.mod_spec(layer - 1, tb),
                    pl.BlockSpec((TOP_K, tb, d), lambda i: (0, i, 0)),
                    pl.BlockSpec((tb, TOP_K), lambda i: (i, 0))] + in_specs
        args = [x, mods, yg, top_p] + args
        out_specs = [row(d)] + out_specs
        out_shape = [jax.ShapeDtypeStruct((geo.n, d), F32)] + out_shape
    n_in, n_out = len(args), len(out_shape)
    outs = pl.pallas_call(
        functools.partial(_inproj_kernel, nbc=nbc, lc=geo.lc, add_moe=moe is not None),
        grid=(geo.n // tb,),
        in_specs=in_specs, out_specs=out_specs, out_shape=out_shape,
        input_output_aliases={n_in - 2: n_out - 3, n_in - 1: n_out - 2},
        compiler_params=_cparams(1),
        name="in_proj",
    )(*args)
    return outs if moe is not None else [x] + list(outs)


def _fill_padded(pad_ref, before_ref, main_ref, after_ref, start, end):
    tb = TOKEN_BLOCK
    zero = jnp.zeros(before_ref.shape, F32)
    pad_ref[0:HALO, :] = jnp.where(start, zero, before_ref[...].astype(F32))
    pad_ref[HALO:HALO + tb, :] = main_ref[...].astype(F32)
    pad_ref[HALO + tb:2 * HALO + tb, :] = jnp.where(end, zero, after_ref[...].astype(F32))


def _pool_kernel(main_ref, before_ref, after_ref, w_ref, s_ref, o_ref, pad_ref, *, geo):
    i = pl.program_id(0)
    tb = TOKEN_BLOCK
    start, end = geo.is_start(i), geo.is_end(i)
    _fill_padded(pad_ref, before_ref, main_ref, after_ref, start, end)
    r = lax.broadcasted_iota(jnp.int32, (tb, 1), 0)
    for g, w in enumerate(POOL_WINDOWS):
        cols = slice(g * POOL_GC, (g + 1) * POOL_GC)
        acc = pad_ref[HALO - w // 2:HALO - w // 2 + tb, cols]
        for j in range(-w // 2 + 1, w // 2):
            acc = acc + pad_ref[HALO + j:HALO + j + tb, cols]
        lo = jnp.where(start, jnp.maximum(r - w // 2, 0), r - w // 2)
        hi = jnp.where(end, jnp.minimum(r + w // 2, tb), r + w // 2)
        mean = acc / (hi - lo).astype(F32)
        dlt = mean - main_ref[:, cols]
        y = _dot(dlt.astype(BF16), w_ref[g])
        o_ref[:, cols] = (y * s_ref[:, cols]).astype(BF16)


def _pool(geo, layer, u_pool, w_pool_b, pool_scale):
    tb = TOKEN_BLOCK
    before, after = geo.halo_specs(POOL_WIDTH)
    return pl.pallas_call(
        functools.partial(_pool_kernel, geo=geo),
        grid=(geo.n // tb,),
        in_specs=[pl.BlockSpec((tb, POOL_WIDTH), lambda i: (i, 0)), before, after,
                  _layer_spec(w_pool_b, layer), _layer_spec(pool_scale, layer)],
        out_specs=pl.BlockSpec((tb, POOL_WIDTH), lambda i: (i, 0)),
        out_shape=jax.ShapeDtypeStruct((geo.n, POOL_WIDTH), BF16),
        scratch_shapes=[pltpu.VMEM((tb + 2 * HALO, POOL_WIDTH), F32)],
        compiler_params=_cparams(1),
        name="pool_branch",
    )(u_pool, u_pool, u_pool, w_pool_b, pool_scale)


def _attn_kernel(*refs, lam_init, has_ctx):
    if has_ctx:
        q_ref, k_ref, v_ref, kc_ref, vc_ref, lam_ref, g_ref, o_ref = refs
    else:
        q_ref, k_ref, v_ref, lam_ref, g_ref, o_ref = refs
    lp = lam_ref[...]
    lam = (jnp.exp(jnp.sum(lp[0:1] * lp[1:2], axis=-1, keepdims=True))
           - jnp.exp(jnp.sum(lp[2:3] * lp[3:4], axis=-1, keepdims=True)) + lam_init)
    for h in range(DA_HEADS):
        vcols = slice(h * DA_VDIM, (h + 1) * DA_VDIM)
        v = v_ref[0, :, vcols]
        if has_ctx:
            ones = lambda a: jnp.concatenate([a, jnp.ones_like(a)], axis=1)
            v = ones(v)
            vc = ones(vc_ref[0, 0, :, vcols].astype(BF16))
        outs = []
        for sub in range(2):
            c0 = h * DA_VDIM + sub * DA_HEAD_DIM
            cols = slice(c0, c0 + DA_HEAD_DIM)
            qs = q_ref[0, :, cols]
            s1 = _dot_nt(qs, k_ref[0, :, cols])
            m = jnp.max(s1, axis=-1, keepdims=True)
            if has_ctx:
                s2 = _dot_nt(qs, kc_ref[0, 0, :, cols].astype(BF16))
                m = jnp.maximum(m, jnp.max(s2, axis=-1, keepdims=True))
            e1 = jnp.exp2(s1 - m)
            pv = _dot(e1.astype(BF16), v)
            if has_ctx:
                pv = pv + _dot(jnp.exp2(s2 - m).astype(BF16), vc)
                outs.append(pv[:, 0:DA_VDIM] / pv[:, DA_VDIM:DA_VDIM + 1])
            else:
                outs.append(pv / jnp.sum(e1, axis=-1, keepdims=True))
        o = outs[0] - lam * outs[1]
        o = _rms(o) * g_ref[...] * (1.0 - lam_init)
        o_ref[0, :, vcols] = o.astype(BF16)


def _attn_ctx(geo, layer, q, kb, vb, lam_p, subln, lam_init):
    bc, lc = geo.bc, geo.lc
    seq = pl.BlockSpec((1, lc, DA_WIDTH), lambda b: (b, 0, 0))
    view = lambda a: a.reshape(geo.n // lc, lc, DA_WIDTH)
    out = pl.pallas_call(
        functools.partial(_attn_kernel, lam_init=lam_init, has_ctx=False),
        grid=(bc,),
        in_specs=[seq, seq, seq, _layer_spec(lam_p, layer), _layer_spec(subln, layer)],
        out_specs=seq,
        out_shape=jax.ShapeDtypeStruct((bc, lc, DA_WIDTH), BF16),
        compiler_params=_cparams(1),
        name="attn_ctx",
    )(view(q), view(kb), view(vb), lam_p, subln)
    return out.reshape(bc * lc, DA_WIDTH)


def _attn_lat(geo, layer, q, kb, vb, cache_k, cache_v, lam_p, subln, lam_init):
    bl, ll = geo.bl, geo.ll
    tq = TOKEN_BLOCK
    past = cache_k.shape[2]
    off = geo.nc // ll
    qblk = pl.BlockSpec((1, tq, DA_WIDTH), lambda b, j: (b + off, j, 0))
    seq = pl.BlockSpec((1, ll, DA_WIDTH), lambda b, j: (b + off, 0, 0))
    cache = pl.BlockSpec((1, 1, past, DA_WIDTH), lambda b, j: (b, layer, 0, 0))
    view = lambda a: a.reshape(geo.n // ll, ll, DA_WIDTH)
    out = pl.pallas_call(
        functools.partial(_attn_kernel, lam_init=lam_init, has_ctx=True),
        grid=(bl, ll // tq),
        in_specs=[qblk, seq, seq, cache, cache, _layer_spec(lam_p, layer), _layer_spec(subln, layer)],
        out_specs=pl.BlockSpec((1, tq, DA_WIDTH), lambda b, j: (b, j, 0)),
        out_shape=jax.ShapeDtypeStruct((bl, ll, DA_WIDTH), BF16),
        compiler_params=_cparams(2),
        name="attn_lat",
    )(view(q), view(kb), view(vb), cache_k, cache_v, lam_p, subln)
    return out.reshape(bl * ll, DA_WIDTH)


@functools.lru_cache(maxsize=None)
def _dft_tables(length):
    k = np.arange(length, dtype=np.int64)
    ks = (k[:, None] * k[None, :]) % (2 * length)
    ang = ks.astype(np.float64) * (np.pi / length)
    cmat = np.cos(ang)
    smat = -np.sin(ang)
    smat[0, :] = 1.0 - 2.0 * (k % 2)
    to_bf16 = lambda a: jnp.asarray(a.astype(np.float32)).astype(BF16)
    return to_bf16(cmat), to_bf16(smat), to_bf16(smat.T)


@functools.lru_cache(maxsize=None)
def _filter_features(length):
    t = np.linspace(0.0, 1.0, length, dtype=np.float32)
    w_ang = (2.0 * math.pi * np.arange(length, dtype=np.float32) / length).astype(np.float32)
    f = np.linspace(1e-4, HY_BANDS - 1, HY_BANDS, dtype=np.float32)
    arg = (w_ang[:, None] * f[None, :]).astype(np.float32).astype(np.float64)
    z = np.concatenate([t[:, None].astype(np.float64), np.cos(arg), -np.sin(arg)], axis=-1)
    z = np.pad(z, ((0, 0), (0, 64 - HY_EMB))).astype(np.float32)
    deltas = np.linspace(math.log(HY_TARGET) / HY_FAST, math.log(HY_TARGET) / HY_SLOW,
                         HY_WIDTH, dtype=np.float32)
    rates = np.abs(np.concatenate([deltas, deltas]))[None, :]
    return jnp.asarray(z), jnp.asarray(rates)


def _hyfilter_kernel(z_ref, dl_ref, w1_ref, b1_ref, w2_ref, b2_ref, w3_ref, fr_ref, o_ref):
    c = pl.program_id(0)
    fr = fr_ref[...]
    z = z_ref[...]
    hid = jnp.sin(fr[0:1] * (_dot3(z, w1_ref[...]) + b1_ref[...]))
    hid = jnp.sin(fr[1:2] * (_dot3(hid, w2_ref[...]) + b2_ref[...]))
    h = _dot3(hid, w3_ref[...]) * jnp.exp(-z[:, 0:1] * dl_ref[...])
    row = lax.broadcasted_iota(jnp.int32, h.shape, 0) + c * z.shape[0]
    col = lax.broadcasted_iota(jnp.int32, h.shape, 1)
    o_ref[0] = jnp.where(jnp.logical_and(row == 0, col >= HY_WIDTH), 0.0, h).astype(BF16)


def _hyfilter(length, layer, w1p, b1, w2, b2, w3, fr):
    z, dl = _filter_features(length)
    tb = TOKEN_BLOCK
    lay = lambda a: _layer_spec(a, layer)
    return pl.pallas_call(
        _hyfilter_kernel,
        grid=(length // tb,),
        in_specs=[pl.BlockSpec((tb, 64), lambda c: (c, 0)), pl.BlockSpec(dl.shape, lambda c: (0, 0)),
                  lay(w1p), lay(b1), lay(w2), lay(b2), lay(w3), lay(fr)],
        out_specs=pl.BlockSpec((1, tb, 2 * HY_WIDTH), lambda c: (0, c, 0)),
        out_shape=jax.ShapeDtypeStruct((1, length, 2 * HY_WIDTH), BF16),
        compiler_params=_cparams(1),
        name="hyena_filter",
    )(z, dl, w1p, b1, w2, b2, w3, fr)


def _hyconv_kernel(main_ref, before_ref, after_ref, cw_ref, cb_ref, zb_ref, x0_ref, pad_ref, *, geo):
    i = pl.program_id(0)
    tb = TOKEN_BLOCK
    _fill_padded(pad_ref, before_ref, main_ref, after_ref, geo.is_start(i), geo.is_end(i))
    w = HY_WIDTH
    parts = []
    for p in range(3):
        cols = slice(p * w, (p + 1) * w)
        uc = (pad_ref[HALO - 1:HALO - 1 + tb, cols] * cw_ref[0:1, cols]
              + pad_ref[HALO:HALO + tb, cols] * cw_ref[1:2, cols]
              + pad_ref[HALO + 1:HALO + 1 + tb, cols] * cw_ref[2:3, cols]
              + cb_ref[:, cols])
        parts.append(uc)
    x0, x1, v = parts
    zb_ref[...] = (v * x1).astype(BF16)
    x0_ref[...] = x0.astype(BF16)


def _hyconv(geo, layer, u_hy, conv_w, conv_b):
    tb = TOKEN_BLOCK
    w3 = 3 * HY_WIDTH
    before, after = geo.halo_specs(w3)
    row = pl.BlockSpec((tb, HY_WIDTH), lambda i: (i, 0))
    return pl.pallas_call(
        functools.partial(_hyconv_kernel, geo=geo),
        grid=(geo.n // tb,),
        in_specs=[pl.BlockSpec((tb, w3), lambda i: (i, 0)), before, after,
                  _layer_spec(conv_w, layer), _layer_spec(conv_b, layer)],
        out_specs=[row, row],
        out_shape=[jax.ShapeDtypeStruct((geo.n, HY_WIDTH), BF16),
                   jax.ShapeDtypeStruct((geo.n, HY_WIDTH), BF16)],
        scratch_shapes=[pltpu.VMEM((tb + 2 * HALO, w3), F32)],
        compiler_params=_cparams(1),
        name="hyena_conv_gate",
    )(u_hy, u_hy, u_hy, conv_w, conv_b)


def _seqs_per_step(length, nseq):
    g = max(1, min(nseq, 2048 // length))
    assert nseq % g == 0
    return g


def _dft_fwd_kernel(c_ref, s_ref, x_ref, re_ref, im_ref):
    for s in range(x_ref.shape[0]):
        x = x_ref[s]
        re_ref[s] = _dot(c_ref[...], x)
        im_ref[s] = _dot(s_ref[...], x)


def _dft_fwd(length, x, nseq, seq_off):
    cmat, smat, _ = _dft_tables(length)
    n = x.shape[-1]
    tm = min(length, 512)
    tn = 512
    g = _seqs_per_step(length, nseq)
    assert seq_off % g == 0
    a_spec = pl.BlockSpec((tm, length), lambda b, j, m: (m, 0))
    o_spec = pl.BlockSpec((g, tm, tn), lambda b, j, m: (b, m, j))
    return pl.pallas_call(
        _dft_fwd_kernel,
        grid=(nseq // g, n // tn, length // tm),
        in_specs=[a_spec, a_spec, pl.BlockSpec((g, length, tn), lambda b, j, m: (b + seq_off // g, 0, j))],
        out_specs=[o_spec, o_spec],
        out_shape=[jax.ShapeDtypeStruct((nseq, length, n), F32)] * 2,
        compiler_params=_cparams(3),
        name="hyena_dft",
    )(cmat, smat, x)


def _hyprod_kernel(zr_ref, zi_ref, hfr_ref, hgr_ref, hfi_ref, hgi_ref, yr_ref, yi_ref, *, length):
    c = pl.program_id(1)
    shape = zr_ref.shape[1:]
    k = lax.broadcasted_iota(jnp.int32, shape, 0) + c * shape[0]
    first = k == 0
    hr = hfr_ref[0] + hgr_ref[0]
    hi = jnp.where(first, hfi_ref[0] + hgi_ref[0], hfi_ref[0] - hgi_ref[0])
    inv = 1.0 / length
    for s in range(zr_ref.shape[0]):
        zr, zi = zr_ref[s], zi_ref[s]
        yr = (zr * hr - zi * hi) * inv
        yi = (zr * hi + zi * hr) * inv
        yr_ref[s] = jnp.where(first, zr * hr * (0.5 * inv), yr).astype(BF16)
        yi_ref[s] = jnp.where(first, zi * hi * (0.5 * inv), yi).astype(BF16)


def _hyprod(length, zre, zim, hre, him):
    nb = zre.shape[0]
    tb = TOKEN_BLOCK
    g = _seqs_per_step(length, nb)
    zs = pl.BlockSpec((g, tb, HY_WIDTH), lambda b, c: (b, c, 0))
    hf = pl.BlockSpec((1, tb, HY_WIDTH), lambda b, c: (0, c, 0))
    hg = pl.BlockSpec((1, tb, HY_WIDTH), lambda b, c: (0, c, 1))
    return pl.pallas_call(
        functools.partial(_hyprod_kernel, length=length),
        grid=(nb // g, length // tb),
        in_specs=[zs, zs, hf, hg, hf, hg],
        out_specs=[zs, zs],
        out_shape=[jax.ShapeDtypeStruct((nb, length, HY_WIDTH), BF16)] * 2,
        compiler_params=_cparams(2),
        name="hyena_spectral_product",
    )(zre, zim, hre, hre, him, him)


def _hyinv_kernel(c_ref, st_ref, yr_ref, yi_ref, z_ref, x0_ref, skip_ref, o_ref):
    for s in range(yr_ref.shape[0]):
        y = _dot(c_ref[...], yr_ref[s]) + _dot(st_ref[...], yi_ref[s])
        o_ref[s] = ((y + z_ref[s].astype(F32) * skip_ref[...]) * x0_ref[s].astype(F32)).astype(BF16)


def _hyinv(length, layer, yr, yi, z, x0, skip, seq_off):
    cmat, _, smat_t = _dft_tables(length)
    nseq = yr.shape[0]
    tm = min(length, 512)
    g = _seqs_per_step(length, nseq)
    assert seq_off % g == 0
    a_spec = pl.BlockSpec((tm, length), lambda b, m: (m, 0))
    y_spec = pl.BlockSpec((g, length, HY_WIDTH), lambda b, m: (b, 0, 0))
    t_spec = pl.BlockSpec((g, tm, HY_WIDTH), lambda b, m: (b + seq_off // g, m, 0))
    return pl.pallas_call(
        _hyinv_kernel,
        grid=(nseq // g, length // tm),
        in_specs=[a_spec, a_spec, y_spec, y_spec, t_spec, t_spec, _layer_spec(skip, layer)],
        out_specs=pl.BlockSpec((g, tm, HY_WIDTH), lambda b, m: (b, m, 0)),
        out_shape=jax.ShapeDtypeStruct((nseq, length, HY_WIDTH), BF16),
        compiler_params=_cparams(2),
        name="hyena_idft",
    )(cmat, smat_t, yr, yi, z, x0, skip)


def _hyena_long_conv(length, layer, nseq, seq_off, zb, x0, filt, skip):
    view = lambda a: a.reshape(a.shape[0] // length, length, HY_WIDTH)
    zre, zim = _dft_fwd(length, view(zb), nseq, seq_off)
    hre, him = _dft_fwd(length, filt, 1, 0)
    yr, yi = _hyprod(length, zre, zim, hre, him)
    out = _hyinv(length, layer, yr, yi, view(zb), view(x0), skip, seq_off)
    return out.reshape(nseq * length, HY_WIDTH)


def _merge_kernel(x_ref, mod_ref, g1_ref, a_ref, bc_ref, bl_ref, cc_ref, cl_ref, wg_ref, bg_ref, wbr_ref,
                  wo_ref, g2_ref, wr_ref, br_ref, xo_ref, h2_ref, ti_ref, tp_ref, *, nbc):
    d = D_MODEL
    is_ctx = pl.program_id(0) < nbc
    mod = mod_ref[0]
    shift1, scale1, gate1 = mod[:, 0:d], mod[:, d:2 * d], mod[:, 2 * d:3 * d]
    shift2, scale2 = mod[:, 3 * d:4 * d], mod[:, 4 * d:5 * d]
    x = x_ref[...]
    h = (_rms(x) * g1_ref[...] * (1.0 + scale1) + shift1).astype(BF16)
    branches = (a_ref[...],
                jnp.where(is_ctx, bc_ref[...], bl_ref[...]),
                jnp.where(is_ctx, cc_ref[...], cl_ref[...]))
    merged = None
    for n, br_n in enumerate(branches):
        cols = slice(n * d, (n + 1) * d)
        gate = _sigmoid(_dot(h, wg_ref[:, cols]) + bg_ref[:, cols])
        term = gate * _dot(br_n, wbr_ref[n])
        merged = term if merged is None else merged + term
    x = x + gate1 * _dot(merged.astype(BF16), wo_ref[...])
    xo_ref[...] = x
    h2 = _rms(x) * g2_ref[...] * (1.0 + scale2) + shift2
    h2_ref[...] = h2.astype(BF16)

    logits = _dot3(h2, wr_ref[...]) + br_ref[...]
    ne = logits.shape[-1]
    lane = lax.broadcasted_iota(jnp.int32, logits.shape, 1).astype(F32)
    vals = logits
    top_v, top_i = [], []
    for _ in range(TOP_K):
        m = jnp.max(vals, axis=-1, keepdims=True)
        idx = jnp.min(jnp.where(vals == m, lane, float(ne)), axis=-1, keepdims=True)
        top_v.append(m)
        top_i.append(idx)
        vals = jnp.where(lane == idx, -jnp.inf, vals)
    es = [jnp.exp(v - top_v[0]) for v in top_v]
    den = es[0] + es[1] + es[2] + es[3]
    for kk in range(TOP_K):
        ti_ref[:, kk:kk + 1] = top_i[kk].astype(jnp.int32)
        tp_ref[:, kk:kk + 1] = es[kk] / den


def _merge(geo, layer, x, mods, g1, a_out, b_ctx, b_lat, c_ctx, c_lat, wg_b, bg, wbr_b, wo_b, g2, wr, br):
    d = D_MODEL
    tb = WIDE_BLOCK
    row = lambda w: pl.BlockSpec((tb, w), lambda i: (i, 0))
    ctx, lat = geo.split_specs(tb, DA_WIDTH)
    lay = lambda a: _layer_spec(a, layer, single_buffer=True)
    return pl.pallas_call(
        functools.partial(_merge_kernel, nbc=geo.nc // tb),
        grid=(geo.n // tb,),
        in_specs=[row(d), geo.mod_spec(layer, tb), lay(g1), row(POOL_WIDTH), ctx, lat, ctx, lat,
                  lay(wg_b), lay(bg), lay(wbr_b), lay(wo_b), lay(g2), lay(wr), lay(br)],
        out_specs=[row(d), row(d), row(TOP_K), row(TOP_K)],
        out_shape=[jax.ShapeDtypeStruct((geo.n, d), F32),
                   jax.ShapeDtypeStruct((geo.n, d), BF16),
                   jax.ShapeDtypeStruct((geo.n, TOP_K), jnp.int32),
                   jax.ShapeDtypeStruct((geo.n, TOP_K), F32)],
        compiler_params=_cparams(1),
        name="merge_route",
    )(x, mods, g1, a_out, b_ctx, b_lat, c_ctx, c_lat, wg_b, bg, wbr_b, wo_b, g2, wr, br)


def _ffn_kernel(sw_ref, st_ref, sn_ref, nv_ref, xs_hbm, wgu_ref, bgu_ref, wd_ref, bd_ref, ys_hbm,
                xbuf, ybuf, semx, semy):
    i = pl.program_id(0)
    nv = nv_ref[0]
    slot = i % 2
    t = MOE_TILE
    half = D_FF // 2

    def rows_of(step, k):
        return pl.ds(pl.multiple_of(st_ref[step] * t, t), k * t)

    def x_copy(step, buf, k):
        return pltpu.make_async_copy(xs_hbm.at[rows_of(step, k)], xbuf.at[buf, pl.ds(0, k * t)], semx.at[buf])

    def y_copy(step, buf, k):
        return pltpu.make_async_copy(ybuf.at[buf, pl.ds(0, k * t)], ys_hbm.at[rows_of(step, k)], semy.at[buf])

    def for_size(step, fn):
        n = sn_ref[step]
        for k in range(1, MOE_MAX_SUB + 1):
            @pl.when(n == k)
            def _(k=k):
                fn(k)

    def compute(k):
        m = k * t
        x = xbuf[slot, 0:m, :]
        y = None
        for c in range(2):
            gcols = slice(c * half, (c + 1) * half)
            ucols = slice(D_FF + c * half, D_FF + (c + 1) * half)
            gate = _dot(x, wgu_ref[0, :, gcols].astype(BF16)) + bgu_ref[0, :, gcols]
            up = _dot(x, wgu_ref[0, :, ucols].astype(BF16)) + bgu_ref[0, :, ucols]
            gate = jnp.minimum(gate, SWIGLU_LIMIT)
            up = jnp.clip(up, -SWIGLU_LIMIT, SWIGLU_LIMIT)
            act = ((up + 1.0) * gate * _sigmoid(SWIGLU_ALPHA * gate)).astype(BF16)
            part = _dot(act, wd_ref[0, gcols, :].astype(BF16))
            y = part if y is None else y + part
        ybuf[slot, 0:m, :] = (y + bd_ref[0]).astype(BF16)
        y_copy(i, slot, k).start()

    @pl.when(i < nv)
    def _():
        @pl.when(i == 0)
        def _():
            for_size(0, lambda k: x_copy(0, 0, k).start())

        for_size(i, lambda k: x_copy(i, slot, k).wait())

        @pl.when(i + 1 < nv)
        def _():
            for_size(i + 1, lambda k: x_copy(i + 1, 1 - slot, k).start())

        @pl.when(i >= 2)
        def _():
            for_size(i - 2, lambda k: y_copy(i - 2, slot, k).wait())

        for_size(i, compute)

        @pl.when(i == nv - 1)
        def _():
            for_size(i, lambda k: y_copy(i, slot, k).wait())

            @pl.when(i >= 1)
            def _():
                for_size(i - 1, lambda k: y_copy(i - 1, 1 - slot, k).wait())


def _ffn(xs, step_weight, step_tile, step_sub, n_valid, w_gu, b_gu, w_down, b_down):
    p = xs.shape[0]
    d = D_MODEL
    t = MOE_TILE
    hbm = pl.BlockSpec(memory_space=pl.ANY)
    wmap = lambda i, sw, st, sn, nv: (sw[i], 0, 0)
    grid_spec = pltpu.PrefetchScalarGridSpec(
        num_scalar_prefetch=4,
        grid=(step_weight.shape[0],),
        in_specs=[hbm,
                  pl.BlockSpec((1, d, 2 * D_FF), wmap), pl.BlockSpec((1, 1, 2 * D_FF), wmap),
                  pl.BlockSpec((1, D_FF, d), wmap), pl.BlockSpec((1, 1, d), wmap)],
        out_specs=hbm,
        scratch_shapes=[pltpu.VMEM((2, MOE_MAX_SUB * t, d), BF16), pltpu.VMEM((2, MOE_MAX_SUB * t, d), BF16),
                        pltpu.SemaphoreType.DMA((2,)), pltpu.SemaphoreType.DMA((2,))])
    return pl.pallas_call(
        _ffn_kernel,
        grid_spec=grid_spec,
        out_shape=jax.ShapeDtypeStruct((p, d), BF16),
        compiler_params=_cparams(1),
        name="moe_experts",
    )(step_weight, step_tile, step_sub, n_valid, xs, w_gu, b_gu, w_down, b_down)


def _moe_rows(n, n_experts):
    worst = n * TOP_K + n_experts * ((n // DISPATCH_BLOCK) * (RUN_ALIGN - 1) + MOE_TILE - 1)
    return -(-worst // MOE_TILE) * MOE_TILE


def _route(top_i, n_experts):
    n = top_i.shape[0]
    t = MOE_TILE
    nb = n // DISPATCH_BLOCK
    experts = jnp.arange(n_experts, dtype=jnp.int32)
    onehot = top_i.reshape(nb, DISPATCH_BLOCK * TOP_K)[:, :, None] == experts[None, None, :]
    n_be = jnp.sum(onehot, axis=1, dtype=jnp.int32)
    cap = (n_be + RUN_ALIGN - 1) // RUN_ALIGN * RUN_ALIGN
    rows_e = jnp.sum(cap, axis=0)
    tiles_e = (rows_e + t - 1) // t
    tile_end = jnp.cumsum(tiles_e)
    tile_start = tile_end - tiles_e
    run_dst = tile_start[None, :] * t + jnp.cumsum(cap, axis=0) - cap
    run_loc = jnp.cumsum(cap, axis=1) - cap
    tail_dst = tile_start * t + rows_e
    tail = tiles_e * t - rows_e

    def owner(ends, ids):
        return jnp.sum(ends[None, :] <= ids[:, None], axis=1, dtype=jnp.int32)[:, None] == experts[None, :]

    pick = lambda oh, table: jnp.sum(jnp.where(oh, table[None, :], 0), axis=1)

    steps_e = (tiles_e + MOE_MAX_SUB - 1) // MOE_MAX_SUB
    step_end = jnp.cumsum(steps_e)
    n_steps = _moe_rows(n, n_experts) // (t * MOE_MAX_SUB) + n_experts
    n_valid = step_end[-1]
    step_ids = jnp.minimum(jnp.arange(n_steps, dtype=jnp.int32), n_valid - 1)
    step_oh = owner(step_end, step_ids)
    j = step_ids - pick(step_oh, step_end - steps_e)
    step_expert = jnp.argmax(step_oh, axis=1).astype(jnp.int32)
    n_st = jnp.maximum(pick(step_oh, steps_e), 1)
    n_ti = pick(step_oh, tiles_e)
    base, rem = n_ti // n_st, n_ti % n_st
    extra = jnp.maximum(j - (n_st - rem), 0)
    step_tile = pick(step_oh, tile_start) + j * base + extra
    step_sub = jnp.clip(base + (j >= n_st - rem).astype(jnp.int32), 1, MOE_MAX_SUB)
    units = lambda a: (a // RUN_ALIGN).reshape(-1).astype(jnp.int32)
    runs = (units(run_dst), units(run_loc), units(cap), units(jnp.sum(cap, axis=1)), units(tail_dst), units(tail))
    cols = (run_dst.reshape(nb, n_experts, 1).astype(F32), run_loc.reshape(nb, n_experts, 1).astype(F32))
    steps = (step_expert, step_tile.astype(jnp.int32), step_sub.astype(jnp.int32),
             n_valid.reshape(1).astype(jnp.int32))
    return runs, cols, steps


def _sized_copies(units, src_at, dst_at, sem, max_bits, fn):
    for bit in range(max_bits):
        size = RUN_ALIGN << bit

        @pl.when((units >> bit) & 1 == 1)
        def _(bit=bit, size=size):
            off = (units & ((1 << bit) - 1)) * RUN_ALIGN
            fn(pltpu.make_async_copy(src_at(off, size), dst_at(off, size), sem))


def _dispatch_kernel(rdst_ref, rloc_ref, rcap_ref, rtot_ref, tdst_ref, tlen_ref,
                     h_ref, tt_ref, dcol_ref, lcol_ref, xs_hbm, slot_ref, stage, zeros, sem, zsem,
                     *, n_experts):
    b = pl.program_id(0)
    nb = pl.num_programs(0)
    tb = DISPATCH_BLOCK
    buf = b % 2
    run_bits = (tb // RUN_ALIGN).bit_length()
    tail_bits = (MOE_TILE // RUN_ALIGN).bit_length()
    stage_bits = (stage.shape[1] // RUN_ALIGN).bit_length()

    def drain_runs(blk, slot):
        _sized_copies(rtot_ref[blk],
                      lambda off, size: stage.at[slot, pl.ds(0, size)],
                      lambda off, size: xs_hbm.at[pl.ds(0, size)],
                      sem.at[slot], stage_bits, lambda cp: cp.wait())

    def run_copies(blk, slot, fn):
        def body(e, carry):
            idx = blk * n_experts + e
            src0 = pl.multiple_of(rloc_ref[idx] * RUN_ALIGN, RUN_ALIGN)
            dst0 = pl.multiple_of(rdst_ref[idx] * RUN_ALIGN, RUN_ALIGN)
            _sized_copies(rcap_ref[idx],
                          lambda off, size: stage.at[slot, pl.ds(pl.multiple_of(src0 + off, RUN_ALIGN), size)],
                          lambda off, size: xs_hbm.at[pl.ds(pl.multiple_of(dst0 + off, RUN_ALIGN), size)],
                          sem.at[slot], run_bits, fn)
            return carry
        lax.fori_loop(0, n_experts, body, 0)

    def tail_copies(fn):
        def body(e, carry):
            dst0 = pl.multiple_of(tdst_ref[e] * RUN_ALIGN, RUN_ALIGN)
            _sized_copies(tlen_ref[e],
                          lambda off, size: zeros.at[pl.ds(0, size)],
                          lambda off, size: xs_hbm.at[pl.ds(pl.multiple_of(dst0 + off, RUN_ALIGN), size)],
                          zsem, tail_bits, fn)
            return carry
        lax.fori_loop(0, n_experts, body, 0)

    @pl.when(b == 0)
    def _():
        zeros[...] = jnp.zeros(zeros.shape, zeros.dtype)
        tail_copies(lambda cp: cp.start())

    @pl.when(b >= 2)
    def _():
        drain_runs(b - 2, buf)

    tt = tt_ref[...]
    e_iota = lax.broadcasted_iota(jnp.int32, (n_experts, tb), 0)
    hit = [e_iota == tt[k:k + 1, :] for k in range(TOP_K)]
    member = jnp.where(hit[0] | hit[1] | hit[2] | hit[3], 1.0, 0.0)
    before = (lax.broadcasted_iota(jnp.int32, (tb, tb), 0)
              < lax.broadcasted_iota(jnp.int32, (tb, tb), 1))
    rank = _dot(member.astype(BF16), jnp.where(before, 1.0, 0.0).astype(BF16))
    loc = [jnp.sum(jnp.where(hit[k], lcol_ref[0] + rank, 0.0), axis=0, keepdims=True) for k in range(TOP_K)]
    for k in range(TOP_K):
        slot_ref[k:k + 1, :] = jnp.sum(jnp.where(hit[k], dcol_ref[0] + rank, 0.0),
                                       axis=0, keepdims=True).astype(jnp.int32)

    rows = stage.shape[1]
    r_iota = lax.broadcasted_iota(jnp.int32, (rows, tb), 0).astype(F32)
    place = jnp.zeros((rows, tb), F32)
    for k in range(TOP_K):
        place = jnp.where(r_iota == loc[k], 1.0, place)
    stage[buf] = _dot(place.astype(BF16), h_ref[...]).astype(BF16)
    run_copies(b, buf, lambda cp: cp.start())

    @pl.when(b == nb - 1)
    def _():
        drain_runs(b, buf)

        @pl.when(b >= 1)
        def _():
            drain_runs(b - 1, 1 - buf)
        tail_copies(lambda cp: cp.wait())


def _dispatch(h2, top_t, runs, cols, n_experts):
    n, d = h2.shape
    tb = DISPATCH_BLOCK
    stage_rows = tb * TOP_K + n_experts * RUN_ALIGN
    col = pl.BlockSpec((1, n_experts, 1), lambda b, *_: (b, 0, 0))
    grid_spec = pltpu.PrefetchScalarGridSpec(
        num_scalar_prefetch=6,
        grid=(n // tb,),
        in_specs=[pl.BlockSpec((tb, d), lambda b, *_: (b, 0)),
                  pl.BlockSpec((TOP_K, tb), lambda b, *_: (0, b)), col, col],
        out_specs=[pl.BlockSpec(memory_space=pl.ANY), pl.BlockSpec((TOP_K, tb), lambda b, *_: (0, b))],
        scratch_shapes=[pltpu.VMEM((2, stage_rows, d), BF16), pltpu.VMEM((MOE_TILE, d), BF16),
                        pltpu.SemaphoreType.DMA((2,)), pltpu.SemaphoreType.DMA(())])
    return pl.pallas_call(
        functools.partial(_dispatch_kernel, n_experts=n_experts),
        grid_spec=grid_spec,
        out_shape=[jax.ShapeDtypeStruct((_moe_rows(n, n_experts), d), BF16),
                   jax.ShapeDtypeStruct((TOP_K, n), jnp.int32)],
        compiler_params=_cparams(1),
        name="moe_dispatch",
    )(*runs, h2, top_t, *cols)


def _moe_sum(x_ref, mod_ref, yg_ref, p_ref):
    d = D_MODEL
    gate2 = mod_ref[0][:, 5 * d:6 * d]
    p = p_ref[...]
    moe = p[:, 0:1] * yg_ref[0].astype(F32)
    for kk in range(1, TOP_K):
        moe = moe + p[:, kk:kk + 1] * yg_ref[kk].astype(F32)
    return x_ref[...] + gate2 * moe


def _combine_final_kernel(x_ref, mod_ref, yg_ref, p_ref, gf_ref, yc_ref, yl_ref, *, nbc):
    i = pl.program_id(0)
    y = _rms(_moe_sum(x_ref, mod_ref, yg_ref, p_ref)) * gf_ref[...]

    @pl.when(i < nbc)
    def _():
        yc_ref[...] = y

    @pl.when(i >= nbc)
    def _():
        yl_ref[...] = y


def _combine(geo, layer, x, mods, yg, top_p, final_g):
    d = D_MODEL
    tb = TOKEN_BLOCK
    row = pl.BlockSpec((tb, d), lambda i: (i, 0))
    in_specs = [row, geo.mod_spec(layer, tb),
                pl.BlockSpec((TOP_K, tb, d), lambda i: (0, i, 0)),
                pl.BlockSpec((tb, TOP_K), lambda i: (i, 0))]
    ctx, lat = geo.split_specs(tb, d)
    return pl.pallas_call(
        functools.partial(_combine_final_kernel, nbc=geo.nc // tb),
        grid=(geo.n // tb,),
        in_specs=in_specs + [pl.BlockSpec(final_g.shape, lambda i: (0, 0))],
        out_specs=[ctx, lat],
        out_shape=[jax.ShapeDtypeStruct((geo.nc, d), F32), jax.ShapeDtypeStruct((geo.nl, d), F32)],
        compiler_params=_cparams(1), name="moe_combine_final",
    )(x, mods, yg, top_p, final_g)


@functools.lru_cache(maxsize=None)
def _rope_tables(length):
    rows = length // GRID_W
    row = np.repeat(np.arange(rows), GRID_W).astype(np.float32)
    col = np.tile(np.arange(GRID_W), rows).astype(np.float32)
    ax = DA_HEAD_DIM // 2
    inv = (ROPE_BASE ** (-(np.arange(ax // 2, dtype=np.float32) * 2.0 / ax))).astype(np.float32)
    ang_r = (row[:, None] * inv).astype(np.float32)
    ang_c = (col[:, None] * inv).astype(np.float32)
    ang = np.concatenate([ang_r, ang_r, ang_c, ang_c], axis=-1).astype(np.float64)
    sign = np.where((np.arange(DA_HEAD_DIM) % 32) < 16, -1.0, 1.0)
    reps = DA_WIDTH // DA_HEAD_DIM
    cos = np.tile(np.cos(ang), (1, reps)).astype(np.float32)
    sin_signed = np.tile(np.sin(ang) * sign[None, :], (1, reps)).astype(np.float32)
    return jnp.asarray(cos), jnp.asarray(sin_signed)


def kernel(x_prompt, x_sample, cache_k, cache_v, c, c_ctx, w_ada, b_ada, norm1, norm2, w_in, w_pool, pool_scale, da_lambda, da_subln, hy_conv_w, hy_conv_b, hy_f_w1, hy_f_b1, hy_f_w2, hy_f_b2, hy_f_w3, hy_sin_freq, hy_skip, w_branch, w_gate, b_gate, w_o, w_router, b_router, w_gu, b_gu, w_down, b_down, final_norm):
    bc, lc, d = x_prompt.shape
    bl, ll, _ = x_sample.shape
    depth = w_in.shape[0]
    n_experts = w_router.shape[-1]
    past = cache_k.shape[2]
    geo = _Geom(bc, lc, bl, ll)
    assert 1 + bl <= MOD_ROWS

    x = jnp.concatenate([x_prompt.reshape(bc * lc, d), x_sample.reshape(bl * ll, d)], axis=0)
    cond = jnp.concatenate([c_ctx[None], c, jnp.zeros((MOD_ROWS - 1 - bl, d), F32)], axis=0)
    mods = _ada(cond, w_ada, b_ada).reshape(depth * MOD_ROWS, 1, N_MOD * d)
    cos_t, sin_t = _rope_tables(ll)
    ck = cache_k.reshape(bl, depth, past, DA_WIDTH)
    cv = cache_v.reshape(bl, depth, past, DA_WIDTH)

    row3 = lambda a: a.reshape(depth, 1, a.shape[-1])
    g1, g2 = row3(norm1), row3(norm2)
    w_in_b = w_in.astype(BF16)
    w_pool_b = w_pool.astype(BF16)
    w_gate_b = w_gate.astype(BF16)
    w_branch_b = w_branch.astype(BF16)
    w_o_b = w_o.astype(BF16)
    w1p = jnp.pad(hy_f_w1, ((0, 0), (0, 64 - HY_EMB), (0, 0)))
    fargs = (w1p, row3(hy_f_b1), hy_f_w2, row3(hy_f_b2), hy_f_w3, hy_sin_freq)
    w_gu_s = w_gu.reshape(depth * n_experts, d, 2 * D_FF)
    b_gu_s = b_gu.reshape(depth * n_experts, 1, 2 * D_FF)
    w_down_s = w_down.reshape(depth * n_experts, D_FF, d)
    b_down_s = b_down.reshape(depth * n_experts, 1, d)

    new_k = jnp.zeros((bc, depth, lc * DA_HEADS, DA_VDIM), F32)
    new_v = jnp.zeros((bc, depth, lc * DA_HEADS, DA_VDIM), F32)
    moe = None
    for l in range(depth):
        x, u_pool, q, kb, vb, new_k, new_v, u_hy = _inproj(geo, l, x, mods, g1, w_in_b, cos_t, sin_t,
                                                           new_k, new_v, moe)
        a_out = _pool(geo, l, u_pool, w_pool_b, row3(pool_scale))

        lam_init = 0.8 - 0.6 * math.exp(-0.3 * l)
        subln = row3(da_subln)
        b_ctx = _attn_ctx(geo, l, q, kb, vb, da_lambda, subln, lam_init)
        b_lat = _attn_lat(geo, l, q, kb, vb, ck, cv, da_lambda, subln, lam_init)

        zb, x0 = _hyconv(geo, l, u_hy, hy_conv_w, row3(hy_conv_b))
        skip = row3(hy_skip)
        c_ctx_out = _hyena_long_conv(lc, l, bc, 0, zb, x0, _hyfilter(lc, l, *fargs), skip)
        c_lat_out = _hyena_long_conv(ll, l, bl, geo.nc // ll, zb, x0, _hyfilter(ll, l, *fargs), skip)

        x, h2, top_i, top_p = _merge(geo, l, x, mods, g1, a_out, b_ctx, b_lat, c_ctx_out, c_lat_out,
                                     w_gate_b, row3(b_gate), w_branch_b, w_o_b, g2, w_router, row3(b_router))

        runs, cols, (step_expert, step_tile, step_sub, n_valid) = _route(top_i, n_experts)
        xs, pair_slot = _dispatch(h2, top_i.T, runs, cols, n_experts)
        ys = _ffn(xs, step_expert + l * n_experts, step_tile, step_sub, n_valid,
                  w_gu_s, b_gu_s, w_down_s, b_down_s)
        yg = jnp.take(ys, pair_slot, axis=0, mode="clip")
        moe = (yg, top_p)

    y_ctx, y_lat = _combine(geo, depth - 1, x, mods, yg, top_p, final_norm[None])

    return (y_ctx.reshape(bc, lc, d), y_lat.reshape(bl, ll, d),
            new_k.reshape(bc, depth, lc, DA_HEADS, 2 * DA_HEAD_DIM),
            new_v.reshape(bc, depth, lc, DA_HEADS, DA_VDIM))
```

```python
import functools
import math

import numpy as np
import jax
import jax.numpy as jnp
from jax import lax
from jax.experimental import pallas as pl
from jax.experimental.pallas import tpu as pltpu

F32 = jnp.float32
BF16 = jnp.bfloat16

D_MODEL = 1024
GRID_W = 64
NORM_EPS = 1e-6
POOL_WIDTH = 512
POOL_WINDOWS = (2, 4, 8, 16)
POOL_GC = POOL_WIDTH // len(POOL_WINDOWS)
DA_HEADS = 4
DA_HEAD_DIM = 64
DA_VDIM = 2 * DA_HEAD_DIM
DA_WIDTH = DA_HEADS * DA_VDIM
ROPE_BASE = 10000.0
HY_WIDTH = 512
HY_EMB = 33
HY_BANDS = (HY_EMB - 1) // 2
HY_HIDDEN = 64
HY_FAST = 0.3
HY_SLOW = 1.5
HY_TARGET = 1e-2
N_BRANCH = 3
D_IN = POOL_WIDTH + 3 * DA_WIDTH + 3 * HY_WIDTH
TOP_K = 4
D_FF = 1024
SWIGLU_ALPHA = 1.702
SWIGLU_LIMIT = 7.0
N_MOD = 6
MOD_ROWS = 8

TOKEN_BLOCK = 256
WIDE_BLOCK = 512
DFT_ROWS = 1024
HALO = 16
MOE_TILE = 128
MOE_MAX_SUB = 8
DISPATCH_BLOCK = 512
RUN_ALIGN = 16
V7X_VMEM_LIMIT = 56 * 1024 * 1024


def _cparams(n_axes):
    return pltpu.CompilerParams(
        dimension_semantics=("arbitrary",) * n_axes,
        vmem_limit_bytes=V7X_VMEM_LIMIT)


def _dot(a, b):
    return jnp.dot(a, b, preferred_element_type=F32)


def _dot_nt(a, b):
    return lax.dot_general(a, b, (((1,), (1,)), ((), ())), preferred_element_type=F32)


def _split_bf16(a):
    hi = a.astype(BF16)
    lo = (a - hi.astype(F32)).astype(BF16)
    return hi, lo


def _dot3(a, b):
    ah, al = _split_bf16(a)
    bh, bl = _split_bf16(b)
    return _dot(ah, bh) + _dot(al, bh) + _dot(ah, bl)


def _sigmoid(x):
    return 0.5 * jnp.tanh(0.5 * x) + 0.5


def _rms(x):
    return x * lax.rsqrt(jnp.mean(x * x, axis=-1, keepdims=True) + NORM_EPS)


class _Geom:
    def __init__(self, n_ctx_seq, ctx_len, n_lat_seq, lat_len):
        assert ctx_len == TOKEN_BLOCK, "one context sequence per token block"
        assert lat_len % WIDE_BLOCK == 0 and (n_ctx_seq * ctx_len) % lat_len == 0
        self.bc, self.lc, self.bl, self.ll = n_ctx_seq, ctx_len, n_lat_seq, lat_len
        self.nc = n_ctx_seq * ctx_len
        self.nl = n_lat_seq * lat_len
        self.n = self.nc + self.nl

    def group(self, i, tb):
        nbc = self.nc // tb
        return jnp.where(i < nbc, 0, 1 + (i - nbc) // (self.ll // tb))

    def pos_block(self, i, tb):
        nbc = self.nc // tb
        return jnp.where(i < nbc, 0, (i - nbc) % (self.ll // tb))

    def is_start(self, i):
        nbc, bpl = self.nc // TOKEN_BLOCK, self.ll // TOKEN_BLOCK
        return jnp.logical_or(i < nbc, (i - nbc) % bpl == 0)

    def is_end(self, i):
        nbc, bpl = self.nc // TOKEN_BLOCK, self.ll // TOKEN_BLOCK
        return jnp.logical_or(i < nbc, (i - nbc) % bpl == bpl - 1)

    def mod_spec(self, layer, tb):
        return pl.BlockSpec((1, 1, N_MOD * D_MODEL),
                            lambda i: (layer * MOD_ROWS + self.group(i, tb), 0, 0))

    def halo_specs(self, width):
        per = TOKEN_BLOCK // HALO
        last = self.n // HALO - 1
        before = pl.BlockSpec((HALO, width), lambda i: (jnp.maximum(i * per - 1, 0), 0))
        after = pl.BlockSpec((HALO, width), lambda i: (jnp.minimum((i + 1) * per, last), 0))
        return before, after

    def split_specs(self, tb, width):
        nbc = self.nc // tb
        last_lat = self.nl // tb - 1
        ctx = pl.BlockSpec((tb, width), lambda i: (jnp.minimum(i, nbc - 1), 0))
        lat = pl.BlockSpec((tb, width), lambda i: (jnp.clip(i - nbc, 0, last_lat), 0))
        return ctx, lat


def _layer_spec(a, layer, single_buffer=False):
    kw = dict(pipeline_mode=pl.Buffered(1)) if single_buffer else {}
    return pl.BlockSpec((None,) + a.shape[1:], lambda *_: (layer,) + (0,) * (a.ndim - 1), **kw)


def _ada_kernel(c_ref, w_ref, b_ref, o_ref):
    c = c_ref[...]
    s = c * _sigmoid(c)
    o_ref[0] = _dot(s.astype(BF16), w_ref[0].astype(BF16)) + b_ref[0]


def _ada(cond, w_ada, b_ada):
    depth, d, n6 = w_ada.shape
    rows = cond.shape[0]
    tn = 1024
    return pl.pallas_call(
        _ada_kernel,
        grid=(depth, n6 // tn),
        in_specs=[pl.BlockSpec((rows, d), lambda l, j: (0, 0)),
                  pl.BlockSpec((1, d, tn), lambda l, j: (l, 0, j)),
                  pl.BlockSpec((1, 1, tn), lambda l, j: (l, 0, j))],
        out_specs=pl.BlockSpec((1, rows, tn), lambda l, j: (l, 0, j)),
        out_shape=jax.ShapeDtypeStruct((depth, rows, n6), F32),
        compiler_params=_cparams(2),
        name="ada_mod",
    )(cond, w_ada, b_ada.reshape(depth, 1, n6))


def _rope(x, cos, sin_signed, first_half):
    d = x.shape[-1]
    partner = jnp.where(first_half, pltpu.roll(x, d - 16, 1), pltpu.roll(x, 16, 1))
    return x * cos + partner * sin_signed


def _inproj_kernel(*refs, nbc, lc, add_moe):
    if add_moe:
        x_ref, pmod_ref, yg_ref, p_ref = refs[:4]
        refs = refs[4:]
        (mod_ref, g_ref, w_ref, cos_ref, sin_ref, _, _,
         xo_ref, up_ref, q_ref, kb_ref, vb_ref, kf_ref, vf_ref, uh_ref) = refs
        x = _moe_sum(x_ref, pmod_ref, yg_ref, p_ref)
        xo_ref[...] = x
    else:
        (x_ref, mod_ref, g_ref, w_ref, cos_ref, sin_ref, _, _,
         up_ref, q_ref, kb_ref, vb_ref, kf_ref, vf_ref, uh_ref) = refs
        x = x_ref[...]
    i = pl.program_id(0)
    d = D_MODEL
    mod = mod_ref[0]
    shift, scale = mod[:, 0:d], mod[:, d:2 * d]
    h = (_rms(x) * g_ref[...] * (1.0 + scale) + shift).astype(BF16)

    c1 = POOL_WIDTH
    c2 = c1 + DA_WIDTH
    c3 = c2 + DA_WIDTH
    c4 = c3 + DA_WIDTH
    up_ref[...] = _dot(h, w_ref[:, 0:c1])
    uh_ref[...] = _dot(h, w_ref[:, c4:D_IN]).astype(BF16)
    q = _dot(h, w_ref[:, c1:c2]) * (DA_HEAD_DIM ** -0.5 * math.log2(math.e))
    k = _dot(h, w_ref[:, c2:c3])
    v = _dot(h, w_ref[:, c3:c4])
    vb_ref[...] = v.astype(BF16)

    @pl.when(i < nbc)
    def _():
        q_ref[...] = q.astype(BF16)
        kb_ref[...] = k.astype(BF16)
        for s in range(k.shape[0] // lc):
            for hh in range(DA_HEADS):
                rows = pl.ds(hh, lc, stride=DA_HEADS)
                cols = slice(hh * DA_VDIM, (hh + 1) * DA_VDIM)
                kf_ref[s, 0, rows, :] = k[s * lc:(s + 1) * lc, cols]
                vf_ref[s, 0, rows, :] = v[s * lc:(s + 1) * lc, cols]

    @pl.when(i >= nbc)
    def _():
        cos, sin_signed = cos_ref[...], sin_ref[...]
        lane = lax.broadcasted_iota(jnp.int32, q.shape, 1)
        first_half = (lane % 32) < 16
        q_ref[...] = _rope(q, cos, sin_signed, first_half).astype(BF16)
        kb_ref[...] = _rope(k, cos, sin_signed, first_half).astype(BF16)


def _inproj(geo, layer, x, mods, g1, w_in_b, cos_t, sin_t, kacc, vacc, moe=None):
    d = D_MODEL
    tb = WIDE_BLOCK
    nbc = geo.nc // tb
    spb = tb // geo.lc
    row = lambda w: pl.BlockSpec((tb, w), lambda i: (i, 0))
    cache = pl.BlockSpec((spb, 1, geo.lc * DA_HEADS, DA_VDIM), lambda i: (jnp.minimum(i, nbc - 1), layer, 0, 0))
    tab = pl.BlockSpec((tb, DA_WIDTH), lambda i: (geo.pos_block(i, tb), 0))
    hbm = pl.BlockSpec(memory_space=pl.ANY)
    in_specs = [geo.mod_spec(layer, tb), _layer_spec(g1, layer),
                _layer_spec(w_in_b, layer, single_buffer=True), tab, tab, hbm, hbm]
    args = [mods, g1, w_in_b, cos_t, sin_t, kacc, vacc]
    out_specs = [row(POOL_WIDTH), row(DA_WIDTH), row(DA_WIDTH), row(DA_WIDTH), cache, cache, row(3 * HY_WIDTH)]
    out_shape = [jax.ShapeDtypeStruct((geo.n, POOL_WIDTH), F32),
                 jax.ShapeDtypeStruct((geo.n, DA_WIDTH), BF16),
                 jax.ShapeDtypeStruct((geo.n, DA_WIDTH), BF16),
                 jax.ShapeDtypeStruct((geo.n, DA_WIDTH), BF16),
                 jax.ShapeDtypeStruct(kacc.shape, F32),
                 jax.ShapeDtypeStruct(vacc.shape, F32),
                 jax.ShapeDtypeStruct((geo.n, 3 * HY_WIDTH), BF16)]
    if moe is None:
        in_specs = [row(d)] + in_specs
        args = [x] + args
    else:
        yg, top_p = moe
        in_specs = [row(d), geo.mod_spec(layer - 1, tb),
                    pl.BlockSpec((TOP_K, tb, d), lambda i: (0, i, 0)),
                    pl.BlockSpec((tb, TOP_K), lambda i: (i, 0))] + in_specs
        args = [x, mods, yg, top_p] + args
        out_specs = [row(d)] + out_specs
        out_shape = [jax.ShapeDtypeStruct((geo.n, d), F32)] + out_shape
    n_in, n_out = len(args), len(out_shape)
    outs = pl.pallas_call(
        functools.partial(_inproj_kernel, nbc=nbc, lc=geo.lc, add_moe=moe is not None),
        grid=(geo.n // tb,),
        in_specs=in_specs, out_specs=out_specs, out_shape=out_shape,
        input_output_aliases={n_in - 2: n_out - 3, n_in - 1: n_out - 2},
        compiler_params=_cparams(1),
        name="in_proj",
    )(*args)
    return outs if moe is not None else [x] + list(outs)


def _fill_padded(pad_ref, before_ref, main_ref, after_ref, start, end):
    tb = TOKEN_BLOCK
    zero = jnp.zeros(before_ref.shape, F32)
    pad_ref[0:HALO, :] = jnp.where(start, zero, before_ref[...].astype(F32))
    pad_ref[HALO:HALO + tb, :] = main_ref[...].astype(F32)
    pad_ref[HALO + tb:2 * HALO + tb, :] = jnp.where(end, zero, after_ref[...].astype(F32))


def _pool_kernel(main_ref, before_ref, after_ref, w_ref, s_ref, o_ref, pad_ref, *, geo):
    i = pl.program_id(0)
    tb = TOKEN_BLOCK
    start, end = geo.is_start(i), geo.is_end(i)
    _fill_padded(pad_ref, before_ref, main_ref, after_ref, start, end)
    r = lax.broadcasted_iota(jnp.int32, (tb, 1), 0)
    for g, w in enumerate(POOL_WINDOWS):
        cols = slice(g * POOL_GC, (g + 1) * POOL_GC)
        acc = pad_ref[HALO - w // 2:HALO - w // 2 + tb, cols]
        for j in range(-w // 2 + 1, w // 2):
            acc = acc + pad_ref[HALO + j:HALO + j + tb, cols]
        lo = jnp.where(start, jnp.maximum(r - w // 2, 0), r - w // 2)
        hi = jnp.where(end, jnp.minimum(r + w // 2, tb), r + w // 2)
        mean = acc / (hi - lo).astype(F32)
        dlt = mean - main_ref[:, cols]
        y = _dot(dlt.astype(BF16), w_ref[g])
        o_ref[:, cols] = (y * s_ref[:, cols]).astype(BF16)


def _attn_kernel(*refs, lam_init, has_ctx):
    if has_ctx:
        q_ref, k_ref, v_ref, kc_ref, vc_ref, lam_ref, g_ref, o_ref = refs
    else:
        q_ref, k_ref, v_ref, lam_ref, g_ref, o_ref = refs
    lp = lam_ref[...]
    lam = (jnp.exp(jnp.sum(lp[0:1] * lp[1:2], axis=-1, keepdims=True))
           - jnp.exp(jnp.sum(lp[2:3] * lp[3:4], axis=-1, keepdims=True)) + lam_init)
    for h in range(DA_HEADS):
        vcols = slice(h * DA_VDIM, (h + 1) * DA_VDIM)
        v = v_ref[0, :, vcols]
        if has_ctx:
            ones = lambda a: jnp.concatenate([a, jnp.ones_like(a)], axis=1)
            v = ones(v)
            vc = ones(vc_ref[0, 0, :, vcols].astype(BF16))
        outs = []
        for sub in range(2):
            c0 = h * DA_VDIM + sub * DA_HEAD_DIM
            cols = slice(c0, c0 + DA_HEAD_DIM)
            qs = q_ref[0, :, cols]
            s1 = _dot_nt(qs, k_ref[0, :, cols])
            m = jnp.max(s1, axis=-1, keepdims=True)
            if has_ctx:
                s2 = _dot_nt(qs, kc_ref[0, 0, :, cols].astype(BF16))
                m = jnp.maximum(m, jnp.max(s2, axis=-1, keepdims=True))
            e1 = jnp.exp2(s1 - m)
            pv = _dot(e1.astype(BF16), v)
            if has_ctx:
                pv = pv + _dot(jnp.exp2(s2 - m).astype(BF16), vc)
                outs.append(pv[:, 0:DA_VDIM] / pv[:, DA_VDIM:DA_VDIM + 1])
            else:
                outs.append(pv / jnp.sum(e1, axis=-1, keepdims=True))
        o = outs[0] - lam * outs[1]
        o = _rms(o) * g_ref[...] * (1.0 - lam_init)
        o_ref[0, :, vcols] = o.astype(BF16)


def _attn_ctx(geo, layer, q, kb, vb, lam_p, subln, lam_init):
    bc, lc = geo.bc, geo.lc
    seq = pl.BlockSpec((1, lc, DA_WIDTH), lambda b: (b, 0, 0))
    view = lambda a: a.reshape(geo.n // lc, lc, DA_WIDTH)
    out = pl.pallas_call(
        functools.partial(_attn_kernel, lam_init=lam_init, has_ctx=False),
        grid=(bc,),
        in_specs=[seq, seq, seq, _layer_spec(lam_p, layer), _layer_spec(subln, layer)],
        out_specs=seq,
        out_shape=jax.ShapeDtypeStruct((bc, lc, DA_WIDTH), BF16),
        compiler_params=_cparams(1),
        name="attn_ctx",
    )(view(q), view(kb), view(vb), lam_p, subln)
    return out.reshape(bc * lc, DA_WIDTH)


def _attn_lat(geo, layer, q, kb, vb, cache_k, cache_v, lam_p, subln, lam_init):
    bl, ll = geo.bl, geo.ll
    tq = TOKEN_BLOCK
    past = cache_k.shape[2]
    off = geo.nc // ll
    qblk = pl.BlockSpec((1, tq, DA_WIDTH), lambda b, j: (b + off, j, 0))
    seq = pl.BlockSpec((1, ll, DA_WIDTH), lambda b, j: (b + off, 0, 0))
    cache = pl.BlockSpec((1, 1, past, DA_WIDTH), lambda b, j: (b, layer, 0, 0))
    view = lambda a: a.reshape(geo.n // ll, ll, DA_WIDTH)
    out = pl.pallas_call(
        functools.partial(_attn_kernel, lam_init=lam_init, has_ctx=True),
        grid=(bl, ll // tq),
        in_specs=[qblk, seq, seq, cache, cache, _layer_spec(lam_p, layer), _layer_spec(subln, layer)],
        out_specs=pl.BlockSpec((1, tq, DA_WIDTH), lambda b, j: (b, j, 0)),
        out_shape=jax.ShapeDtypeStruct((bl, ll, DA_WIDTH), BF16),
        compiler_params=_cparams(2),
        name="attn_lat",
    )(view(q), view(kb), view(vb), cache_k, cache_v, lam_p, subln)
    return out.reshape(bl * ll, DA_WIDTH)


@functools.lru_cache(maxsize=None)
def _dft_tables(length):
    k = np.arange(length, dtype=np.int64)
    ks = (k[:, None] * k[None, :]) % (2 * length)
    ang = ks.astype(np.float64) * (np.pi / length)
    cmat = np.cos(ang)
    smat = -np.sin(ang)
    smat[0, :] = 1.0 - 2.0 * (k % 2)
    to_bf16 = lambda a: jnp.asarray(a.astype(np.float32)).astype(BF16)
    return to_bf16(cmat), to_bf16(smat), to_bf16(smat.T)


@functools.lru_cache(maxsize=None)
def _filter_features(length):
    t = np.linspace(0.0, 1.0, length, dtype=np.float32)
    w_ang = (2.0 * math.pi * np.arange(length, dtype=np.float32) / length).astype(np.float32)
    f = np.linspace(1e-4, HY_BANDS - 1, HY_BANDS, dtype=np.float32)
    arg = (w_ang[:, None] * f[None, :]).astype(np.float32).astype(np.float64)
    z = np.concatenate([t[:, None].astype(np.float64), np.cos(arg), -np.sin(arg)], axis=-1)
    z = np.pad(z, ((0, 0), (0, 64 - HY_EMB))).astype(np.float32)
    deltas = np.linspace(math.log(HY_TARGET) / HY_FAST, math.log(HY_TARGET) / HY_SLOW,
                         HY_WIDTH, dtype=np.float32)
    rates = np.abs(np.concatenate([deltas, deltas]))[None, :]
    return jnp.asarray(z), jnp.asarray(rates)


def _hyfilter_kernel(z_ref, dl_ref, w1_ref, b1_ref, w2_ref, b2_ref, w3_ref, fr_ref, o_ref):
    c = pl.program_id(0)
    fr = fr_ref[...]
    z = z_ref[...]
    hid = jnp.sin(fr[0:1] * (_dot3(z, w1_ref[...]) + b1_ref[...]))
    hid = jnp.sin(fr[1:2] * (_dot3(hid, w2_ref[...]) + b2_ref[...]))
    h = _dot3(hid, w3_ref[...]) * jnp.exp(-z[:, 0:1] * dl_ref[...])
    row = lax.broadcasted_iota(jnp.int32, h.shape, 0) + c * z.shape[0]
    col = lax.broadcasted_iota(jnp.int32, h.shape, 1)
    o_ref[0] = jnp.where(jnp.logical_and(row == 0, col >= HY_WIDTH), 0.0, h).astype(BF16)


def _hyfilter(length, layer, w1p, b1, w2, b2, w3, fr):
    z, dl = _filter_features(length)
    tb = TOKEN_BLOCK
    lay = lambda a: _layer_spec(a, layer)
    return pl.pallas_call(
        _hyfilter_kernel,
        grid=(length // tb,),
        in_specs=[pl.BlockSpec((tb, 64), lambda c: (c, 0)), pl.BlockSpec(dl.shape, lambda c: (0, 0)),
                  lay(w1p), lay(b1), lay(w2), lay(b2), lay(w3), lay(fr)],
        out_specs=pl.BlockSpec((1, tb, 2 * HY_WIDTH), lambda c: (0, c, 0)),
        out_shape=jax.ShapeDtypeStruct((1, length, 2 * HY_WIDTH), BF16),
        compiler_params=_cparams(1),
        name="hyena_filter",
    )(z, dl, w1p, b1, w2, b2, w3, fr)


def _hyconv_kernel(main_ref, before_ref, after_ref, cw_ref, cb_ref, zb_ref, x0_ref, pad_ref, *, geo):
    i = pl.program_id(0)
    tb = TOKEN_BLOCK
    _fill_padded(pad_ref, before_ref, main_ref, after_ref, geo.is_start(i), geo.is_end(i))
    w = HY_WIDTH
    parts = []
    for p in range(3):
        cols = slice(p * w, (p + 1) * w)
        uc = (pad_ref[HALO - 1:HALO - 1 + tb, cols] * cw_ref[0:1, cols]
              + pad_ref[HALO:HALO + tb, cols] * cw_ref[1:2, cols]
              + pad_ref[HALO + 1:HALO + 1 + tb, cols] * cw_ref[2:3, cols]
              + cb_ref[:, cols])
        parts.append(uc)
    x0, x1, v = parts
    zb_ref[...] = (v * x1).astype(BF16)
    x0_ref[...] = x0.astype(BF16)


def _local_mix_kernel(pm_ref, pb_ref, pa_ref, wp_ref, ps_ref, hm_ref, hb_ref, ha_ref, cw_ref, cb_ref,
                      a_ref, zb_ref, x0_ref, pad_p, pad_h, *, geo):
    _pool_kernel(pm_ref, pb_ref, pa_ref, wp_ref, ps_ref, a_ref, pad_p, geo=geo)
    _hyconv_kernel(hm_ref, hb_ref, ha_ref, cw_ref, cb_ref, zb_ref, x0_ref, pad_h, geo=geo)


def _local_mix(geo, layer, u_pool, w_pool_b, pool_scale, u_hy, conv_w, conv_b):
    tb = TOKEN_BLOCK
    w3 = 3 * HY_WIDTH
    p_before, p_after = geo.halo_specs(POOL_WIDTH)
    h_before, h_after = geo.halo_specs(w3)
    row = lambda w: pl.BlockSpec((tb, w), lambda i: (i, 0))
    return pl.pallas_call(
        functools.partial(_local_mix_kernel, geo=geo),
        grid=(geo.n // tb,),
        in_specs=[row(POOL_WIDTH), p_before, p_after, _layer_spec(w_pool_b, layer), _layer_spec(pool_scale, layer),
                  row(w3), h_before, h_after, _layer_spec(conv_w, layer), _layer_spec(conv_b, layer)],
        out_specs=[row(POOL_WIDTH), row(HY_WIDTH), row(HY_WIDTH)],
        out_shape=[jax.ShapeDtypeStruct((geo.n, POOL_WIDTH), BF16),
                   jax.ShapeDtypeStruct((geo.n, HY_WIDTH), BF16),
                   jax.ShapeDtypeStruct((geo.n, HY_WIDTH), BF16)],
        scratch_shapes=[pltpu.VMEM((tb + 2 * HALO, POOL_WIDTH), F32), pltpu.VMEM((tb + 2 * HALO, w3), F32)],
        compiler_params=_cparams(1),
        name="pool_and_conv_gate",
    )(u_pool, u_pool, u_pool, w_pool_b, pool_scale, u_hy, u_hy, u_hy, conv_w, conv_b)


def _seqs_per_step(length, nseq):
    g = max(1, min(nseq, 2048 // length))
    assert nseq % g == 0
    return g


def _dft_fwd_kernel(c_ref, s_ref, x_ref, re_ref, im_ref):
    for s in range(x_ref.shape[0]):
        x = x_ref[s]
        re_ref[s] = _dot(c_ref[...], x)
        im_ref[s] = _dot(s_ref[...], x)


def _dft_fwd(length, x, nseq, seq_off):
    cmat, smat, _ = _dft_tables(length)
    n = x.shape[-1]
    tm = min(length, DFT_ROWS)
    tn = 512
    g = _seqs_per_step(length, nseq)
    assert seq_off % g == 0
    a_spec = pl.BlockSpec((tm, length), lambda b, j, m: (m, 0))
    o_spec = pl.BlockSpec((g, tm, tn), lambda b, j, m: (b, m, j))
    return pl.pallas_call(
        _dft_fwd_kernel,
        grid=(nseq // g, n // tn, length // tm),
        in_specs=[a_spec, a_spec, pl.BlockSpec((g, length, tn), lambda b, j, m: (b + seq_off // g, 0, j))],
        out_specs=[o_spec, o_spec],
        out_shape=[jax.ShapeDtypeStruct((nseq, length, n), F32)] * 2,
        compiler_params=_cparams(3),
        name="hyena_dft",
    )(cmat, smat, x)


def _hyprod_kernel(zr_ref, zi_ref, hfr_ref, hgr_ref, hfi_ref, hgi_ref, yr_ref, yi_ref, *, length):
    c = pl.program_id(1)
    shape = zr_ref.shape[1:]
    k = lax.broadcasted_iota(jnp.int32, shape, 0) + c * shape[0]
    first = k == 0
    hr = hfr_ref[0] + hgr_ref[0]
    hi = jnp.where(first, hfi_ref[0] + hgi_ref[0], hfi_ref[0] - hgi_ref[0])
    inv = 1.0 / length
    for s in range(zr_ref.shape[0]):
        zr, zi = zr_ref[s], zi_ref[s]
        yr = (zr * hr - zi * hi) * inv
        yi = (zr * hi + zi * hr) * inv
        yr_ref[s] = jnp.where(first, zr * hr * (0.5 * inv), yr).astype(BF16)
        yi_ref[s] = jnp.where(first, zi * hi * (0.5 * inv), yi).astype(BF16)


def _hyprod(length, zre, zim, hre, him):
    nb = zre.shape[0]
    tb = TOKEN_BLOCK
    g = _seqs_per_step(length, nb)
    zs = pl.BlockSpec((g, tb, HY_WIDTH), lambda b, c: (b, c, 0))
    hf = pl.BlockSpec((1, tb, HY_WIDTH), lambda b, c: (0, c, 0))
    hg = pl.BlockSpec((1, tb, HY_WIDTH), lambda b, c: (0, c, 1))
    return pl.pallas_call(
        functools.partial(_hyprod_kernel, length=length),
        grid=(nb // g, length // tb),
        in_specs=[zs, zs, hf, hg, hf, hg],
        out_specs=[zs, zs],
        out_shape=[jax.ShapeDtypeStruct((nb, length, HY_WIDTH), BF16)] * 2,
        compiler_params=_cparams(2),
        name="hyena_spectral_product",
    )(zre, zim, hre, hre, him, him)


def _hyinv_kernel(c_ref, st_ref, yr_ref, yi_ref, z_ref, x0_ref, skip_ref, o_ref):
    for s in range(yr_ref.shape[0]):
        y = _dot(c_ref[...], yr_ref[s]) + _dot(st_ref[...], yi_ref[s])
        o_ref[s] = ((y + z_ref[s].astype(F32) * skip_ref[...]) * x0_ref[s].astype(F32)).astype(BF16)


def _hyinv(length, layer, yr, yi, z, x0, skip, seq_off):
    cmat, _, smat_t = _dft_tables(length)
    nseq = yr.shape[0]
    tm = min(length, DFT_ROWS)
    g = _seqs_per_step(length, nseq)
    assert seq_off % g == 0
    a_spec = pl.BlockSpec((tm, length), lambda b, m: (m, 0))
    y_spec = pl.BlockSpec((g, length, HY_WIDTH), lambda b, m: (b, 0, 0))
    t_spec = pl.BlockSpec((g, tm, HY_WIDTH), lambda b, m: (b + seq_off // g, m, 0))
    return pl.pallas_call(
        _hyinv_kernel,
        grid=(nseq // g, length // tm),
        in_specs=[a_spec, a_spec, y_spec, y_spec, t_spec, t_spec, _layer_spec(skip, layer)],
        out_specs=pl.BlockSpec((g, tm, HY_WIDTH), lambda b, m: (b, m, 0)),
        out_shape=jax.ShapeDtypeStruct((nseq, length, HY_WIDTH), BF16),
        compiler_params=_cparams(2),
        name="hyena_idft",
    )(cmat, smat_t, yr, yi, z, x0, skip)


def _hyena_long_conv(length, layer, nseq, seq_off, zb, x0, filt, skip):
    view = lambda a: a.reshape(a.shape[0] // length, length, HY_WIDTH)
    zre, zim = _dft_fwd(length, view(zb), nseq, seq_off)
    hre, him = _dft_fwd(length, filt, 1, 0)
    yr, yi = _hyprod(length, zre, zim, hre, him)
    out = _hyinv(length, layer, yr, yi, view(zb), view(x0), skip, seq_off)
    return out.reshape(nseq * length, HY_WIDTH)


def _merge_kernel(x_ref, mod_ref, g1_ref, a_ref, bc_ref, bl_ref, cc_ref, cl_ref, wg_ref, bg_ref, wbr_ref,
                  wo_ref, g2_ref, wr_ref, br_ref, xo_ref, h2_ref, ti_ref, tp_ref, *, nbc):
    d = D_MODEL
    is_ctx = pl.program_id(0) < nbc
    mod = mod_ref[0]
    shift1, scale1, gate1 = mod[:, 0:d], mod[:, d:2 * d], mod[:, 2 * d:3 * d]
    shift2, scale2 = mod[:, 3 * d:4 * d], mod[:, 4 * d:5 * d]
    x = x_ref[...]
    h = (_rms(x) * g1_ref[...] * (1.0 + scale1) + shift1).astype(BF16)
    branches = (a_ref[...],
                jnp.where(is_ctx, bc_ref[...], bl_ref[...]),
                jnp.where(is_ctx, cc_ref[...], cl_ref[...]))
    merged = None
    for n, br_n in enumerate(branches):
        cols = slice(n * d, (n + 1) * d)
        gate = _sigmoid(_dot(h, wg_ref[:, cols]) + bg_ref[:, cols])
        term = gate * _dot(br_n, wbr_ref[n])
        merged = term if merged is None else merged + term
    x = x + gate1 * _dot(merged.astype(BF16), wo_ref[...])
    xo_ref[...] = x
    h2 = _rms(x) * g2_ref[...] * (1.0 + scale2) + shift2
    h2_ref[...] = h2.astype(BF16)

    logits = _dot3(h2, wr_ref[...]) + br_ref[...]
    ne = logits.shape[-1]
    lane = lax.broadcasted_iota(jnp.int32, logits.shape, 1).astype(F32)
    vals = logits
    top_v, top_i = [], []
    for _ in range(TOP_K):
        m = jnp.max(vals, axis=-1, keepdims=True)
        idx = jnp.min(jnp.where(vals == m, lane, float(ne)), axis=-1, keepdims=True)
        top_v.append(m)
        top_i.append(idx)
        vals = jnp.where(lane == idx, -jnp.inf, vals)
    es = [jnp.exp(v - top_v[0]) for v in top_v]
    den = es[0] + es[1] + es[2] + es[3]
    for kk in range(TOP_K):
        ti_ref[:, kk:kk + 1] = top_i[kk].astype(jnp.int32)
        tp_ref[:, kk:kk + 1] = es[kk] / den


def _merge(geo, layer, x, mods, g1, a_out, b_ctx, b_lat, c_ctx, c_lat, wg_b, bg, wbr_b, wo_b, g2, wr, br):
    d = D_MODEL
    tb = WIDE_BLOCK
    row = lambda w: pl.BlockSpec((tb, w), lambda i: (i, 0))
    ctx, lat = geo.split_specs(tb, DA_WIDTH)
    lay = lambda a: _layer_spec(a, layer, single_buffer=True)
    return pl.pallas_call(
        functools.partial(_merge_kernel, nbc=geo.nc // tb),
        grid=(geo.n // tb,),
        in_specs=[row(d), geo.mod_spec(layer, tb), lay(g1), row(POOL_WIDTH), ctx, lat, ctx, lat,
                  lay(wg_b), lay(bg), lay(wbr_b), lay(wo_b), lay(g2), lay(wr), lay(br)],
        out_specs=[row(d), row(d), row(TOP_K), row(TOP_K)],
        out_shape=[jax.ShapeDtypeStruct((geo.n, d), F32),
                   jax.ShapeDtypeStruct((geo.n, d), BF16),
                   jax.ShapeDtypeStruct((geo.n, TOP_K), jnp.int32),
                   jax.ShapeDtypeStruct((geo.n, TOP_K), F32)],
        compiler_params=_cparams(1),
        name="merge_route",
    )(x, mods, g1, a_out, b_ctx, b_lat, c_ctx, c_lat, wg_b, bg, wbr_b, wo_b, g2, wr, br)


def _ffn_kernel(sw_ref, st_ref, sn_ref, nv_ref, xs_hbm, wgu_ref, bgu_ref, wd_ref, bd_ref, ys_hbm,
                xbuf, ybuf, semx, semy):
    i = pl.program_id(0)
    nv = nv_ref[0]
    slot = i % 2
    t = MOE_TILE
    half = D_FF // 2

    def rows_of(step, k):
        return pl.ds(pl.multiple_of(st_ref[step] * t, t), k * t)

    def x_copy(step, buf, k):
        return pltpu.make_async_copy(xs_hbm.at[rows_of(step, k)], xbuf.at[buf, pl.ds(0, k * t)], semx.at[buf])

    def y_copy(step, buf, k):
        return pltpu.make_async_copy(ybuf.at[buf, pl.ds(0, k * t)], ys_hbm.at[rows_of(step, k)], semy.at[buf])

    def for_size(step, fn):
        n = sn_ref[step]
        for k in range(1, MOE_MAX_SUB + 1):
            @pl.when(n == k)
            def _(k=k):
                fn(k)

    def compute(k):
        m = k * t
        x = xbuf[slot, 0:m, :]
        y = None
        for c in range(2):
            gcols = slice(c * half, (c + 1) * half)
            ucols = slice(D_FF + c * half, D_FF + (c + 1) * half)
            gate = _dot(x, wgu_ref[0, :, gcols].astype(BF16)) + bgu_ref[0, :, gcols]
            up = _dot(x, wgu_ref[0, :, ucols].astype(BF16)) + bgu_ref[0, :, ucols]
            gate = jnp.minimum(gate, SWIGLU_LIMIT)
            up = jnp.clip(up, -SWIGLU_LIMIT, SWIGLU_LIMIT)
            act = ((up + 1.0) * gate * _sigmoid(SWIGLU_ALPHA * gate)).astype(BF16)
            part = _dot(act, wd_ref[0, gcols, :].astype(BF16))
            y = part if y is None else y + part
        ybuf[slot, 0:m, :] = (y + bd_ref[0]).astype(BF16)
        y_copy(i, slot, k).start()

    @pl.when(i < nv)
    def _():
        @pl.when(i == 0)
        def _():
            for_size(0, lambda k: x_copy(0, 0, k).start())

        for_size(i, lambda k: x_copy(i, slot, k).wait())

        @pl.when(i + 1 < nv)
        def _():
            for_size(i + 1, lambda k: x_copy(i + 1, 1 - slot, k).start())

        @pl.when(i >= 2)
        def _():
            for_size(i - 2, lambda k: y_copy(i - 2, slot, k).wait())

        for_size(i, compute)

        @pl.when(i == nv - 1)
        def _():
            for_size(i, lambda k: y_copy(i, slot, k).wait())

            @pl.when(i >= 1)
            def _():
                for_size(i - 1, lambda k: y_copy(i - 1, 1 - slot, k).wait())


def _ffn(xs, step_weight, step_tile, step_sub, n_valid, w_gu, b_gu, w_down, b_down):
    p = xs.shape[0]
    d = D_MODEL
    t = MOE_TILE
    hbm = pl.BlockSpec(memory_space=pl.ANY)
    wmap = lambda i, sw, st, sn, nv: (sw[i], 0, 0)
    grid_spec = pltpu.PrefetchScalarGridSpec(
        num_scalar_prefetch=4,
        grid=(step_weight.shape[0],),
        in_specs=[hbm,
                  pl.BlockSpec((1, d, 2 * D_FF), wmap), pl.BlockSpec((1, 1, 2 * D_FF), wmap),
                  pl.BlockSpec((1, D_FF, d), wmap), pl.BlockSpec((1, 1, d), wmap)],
        out_specs=hbm,
        scratch_shapes=[pltpu.VMEM((2, MOE_MAX_SUB * t, d), BF16), pltpu.VMEM((2, MOE_MAX_SUB * t, d), BF16),
                        pltpu.SemaphoreType.DMA((2,)), pltpu.SemaphoreType.DMA((2,))])
    return pl.pallas_call(
        _ffn_kernel,
        grid_spec=grid_spec,
        out_shape=jax.ShapeDtypeStruct((p, d), BF16),
        compiler_params=_cparams(1),
        name="moe_experts",
    )(step_weight, step_tile, step_sub, n_valid, xs, w_gu, b_gu, w_down, b_down)


def _moe_rows(n, n_experts):
    worst = n * TOP_K + n_experts * ((n // DISPATCH_BLOCK) * (RUN_ALIGN - 1) + MOE_TILE - 1)
    return -(-worst // MOE_TILE) * MOE_TILE


def _route(top_i, n_experts):
    n = top_i.shape[0]
    t = MOE_TILE
    nb = n // DISPATCH_BLOCK
    experts = jnp.arange(n_experts, dtype=jnp.int32)
    onehot = top_i.reshape(nb, DISPATCH_BLOCK * TOP_K)[:, :, None] == experts[None, None, :]
    n_be = jnp.sum(onehot, axis=1, dtype=jnp.int32)
    cap = (n_be + RUN_ALIGN - 1) // RUN_ALIGN * RUN_ALIGN
    rows_e = jnp.sum(cap, axis=0)
    tiles_e = (rows_e + t - 1) // t
    tile_end = jnp.cumsum(tiles_e)
    tile_start = tile_end - tiles_e
    run_dst = tile_start[None, :] * t + jnp.cumsum(cap, axis=0) - cap
    run_loc = jnp.cumsum(cap, axis=1) - cap
    tail_dst = tile_start * t + rows_e
    tail = tiles_e * t - rows_e

    def owner(ends, ids):
        return jnp.sum(ends[None, :] <= ids[:, None], axis=1, dtype=jnp.int32)[:, None] == experts[None, :]

    pick = lambda oh, table: jnp.sum(jnp.where(oh, table[None, :], 0), axis=1)

    steps_e = (tiles_e + MOE_MAX_SUB - 1) // MOE_MAX_SUB
    step_end = jnp.cumsum(steps_e)
    n_steps = _moe_rows(n, n_experts) // (t * MOE_MAX_SUB) + n_experts
    n_valid = step_end[-1]
    step_ids = jnp.minimum(jnp.arange(n_steps, dtype=jnp.int32), n_valid - 1)
    step_oh = owner(step_end, step_ids)
    j = step_ids - pick(step_oh, step_end - steps_e)
    step_expert = jnp.argmax(step_oh, axis=1).astype(jnp.int32)
    n_st = jnp.maximum(pick(step_oh, steps_e), 1)
    n_ti = pick(step_oh, tiles_e)
    base, rem = n_ti // n_st, n_ti % n_st
    extra = jnp.maximum(j - (n_st - rem), 0)
    step_tile = pick(step_oh, tile_start) + j * base + extra
    step_sub = jnp.clip(base + (j >= n_st - rem).astype(jnp.int32), 1, MOE_MAX_SUB)
    units = lambda a: (a // RUN_ALIGN).reshape(-1).astype(jnp.int32)
    runs = (units(run_dst), units(run_loc), units(cap), units(jnp.sum(cap, axis=1)), units(tail_dst), units(tail))
    cols = (run_dst.reshape(nb, n_experts, 1).astype(F32), run_loc.reshape(nb, n_experts, 1).astype(F32))
    steps = (step_expert, step_tile.astype(jnp.int32), step_sub.astype(jnp.int32),
             n_valid.reshape(1).astype(jnp.int32))
    return runs, cols, steps


def _sized_copies(units, src_at, dst_at, sem, max_bits, fn):
    for bit in range(max_bits):
        size = RUN_ALIGN << bit

        @pl.when((units >> bit) & 1 == 1)
        def _(bit=bit, size=size):
            off = (units & ((1 << bit) - 1)) * RUN_ALIGN
            fn(pltpu.make_async_copy(src_at(off, size), dst_at(off, size), sem))


def _dispatch_kernel(rdst_ref, rloc_ref, rcap_ref, rtot_ref, tdst_ref, tlen_ref,
                     h_ref, tt_ref, dcol_ref, lcol_ref, xs_hbm, slot_ref, stage, zeros, sem, zsem,
                     *, n_experts):
    b = pl.program_id(0)
    nb = pl.num_programs(0)
    tb = DISPATCH_BLOCK
    buf = b % 2
    run_bits = (tb // RUN_ALIGN).bit_length()
    tail_bits = (MOE_TILE // RUN_ALIGN).bit_length()
    stage_bits = (stage.shape[1] // RUN_ALIGN).bit_length()

    def drain_runs(blk, slot):
        _sized_copies(rtot_ref[blk],
                      lambda off, size: stage.at[slot, pl.ds(0, size)],
                      lambda off, size: xs_hbm.at[pl.ds(0, size)],
                      sem.at[slot], stage_bits, lambda cp: cp.wait())

    def run_copies(blk, slot, fn):
        def body(e, carry):
            idx = blk * n_experts + e
            src0 = pl.multiple_of(rloc_ref[idx] * RUN_ALIGN, RUN_ALIGN)
            dst0 = pl.multiple_of(rdst_ref[idx] * RUN_ALIGN, RUN_ALIGN)
            _sized_copies(rcap_ref[idx],
                          lambda off, size: stage.at[slot, pl.ds(pl.multiple_of(src0 + off, RUN_ALIGN), size)],
                          lambda off, size: xs_hbm.at[pl.ds(pl.multiple_of(dst0 + off, RUN_ALIGN), size)],
                          sem.at[slot], run_bits, fn)
            return carry
        lax.fori_loop(0, n_experts, body, 0)

    def tail_copies(fn):
        def body(e, carry):
            dst0 = pl.multiple_of(tdst_ref[e] * RUN_ALIGN, RUN_ALIGN)
            _sized_copies(tlen_ref[e],
                          lambda off, size: zeros.at[pl.ds(0, size)],
                          lambda off, size: xs_hbm.at[pl.ds(pl.multiple_of(dst0 + off, RUN_ALIGN), size)],
                          zsem, tail_bits, fn)
            return carry
        lax.fori_loop(0, n_experts, body, 0)

    @pl.when(b == 0)
    def _():
        zeros[...] = jnp.zeros(zeros.shape, zeros.dtype)
        tail_copies(lambda cp: cp.start())

    @pl.when(b >= 2)
    def _():
        drain_runs(b - 2, buf)

    tt = tt_ref[...]
    e_iota = lax.broadcasted_iota(jnp.int32, (n_experts, tb), 0)
    hit = [e_iota == tt[k:k + 1, :] for k in range(TOP_K)]
    member = jnp.where(hit[0] | hit[1] | hit[2] | hit[3], 1.0, 0.0)
    before = (lax.broadcasted_iota(jnp.int32, (tb, tb), 0)
              < lax.broadcasted_iota(jnp.int32, (tb, tb), 1))
    rank = _dot(member.astype(BF16), jnp.where(before, 1.0, 0.0).astype(BF16))
    loc = [jnp.sum(jnp.where(hit[k], lcol_ref[0] + rank, 0.0), axis=0, keepdims=True) for k in range(TOP_K)]
    for k in range(TOP_K):
        slot_ref[k:k + 1, :] = jnp.sum(jnp.where(hit[k], dcol_ref[0] + rank, 0.0),
                                       axis=0, keepdims=True).astype(jnp.int32)

    rows = stage.shape[1]
    r_iota = lax.broadcasted_iota(jnp.int32, (rows, tb), 0).astype(F32)
    place = jnp.zeros((rows, tb), F32)
    for k in range(TOP_K):
        place = jnp.where(r_iota == loc[k], 1.0, place)
    stage[buf] = _dot(place.astype(BF16), h_ref[...]).astype(BF16)
    run_copies(b, buf, lambda cp: cp.start())

    @pl.when(b == nb - 1)
    def _():
        drain_runs(b, buf)

        @pl.when(b >= 1)
        def _():
            drain_runs(b - 1, 1 - buf)
        tail_copies(lambda cp: cp.wait())


def _dispatch(h2, top_t, runs, cols, n_experts):
    n, d = h2.shape
    tb = DISPATCH_BLOCK
    stage_rows = tb * TOP_K + n_experts * RUN_ALIGN
    col = pl.BlockSpec((1, n_experts, 1), lambda b, *_: (b, 0, 0))
    grid_spec = pltpu.PrefetchScalarGridSpec(
        num_scalar_prefetch=6,
        grid=(n // tb,),
        in_specs=[pl.BlockSpec((tb, d), lambda b, *_: (b, 0)),
                  pl.BlockSpec((TOP_K, tb), lambda b, *_: (0, b)), col, col],
        out_specs=[pl.BlockSpec(memory_space=pl.ANY), pl.BlockSpec((TOP_K, tb), lambda b, *_: (0, b))],
        scratch_shapes=[pltpu.VMEM((2, stage_rows, d), BF16), pltpu.VMEM((MOE_TILE, d), BF16),
                        pltpu.SemaphoreType.DMA((2,)), pltpu.SemaphoreType.DMA(())])
    return pl.pallas_call(
        functools.partial(_dispatch_kernel, n_experts=n_experts),
        grid_spec=grid_spec,
        out_shape=[jax.ShapeDtypeStruct((_moe_rows(n, n_experts), d), BF16),
                   jax.ShapeDtypeStruct((TOP_K, n), jnp.int32)],
        compiler_params=_cparams(1),
        name="moe_dispatch",
    )(*runs, h2, top_t, *cols)


def _moe_sum(x_ref, mod_ref, yg_ref, p_ref):
    d = D_MODEL
    gate2 = mod_ref[0][:, 5 * d:6 * d]
    p = p_ref[...]
    moe = p[:, 0:1] * yg_ref[0].astype(F32)
    for kk in range(1, TOP_K):
        moe = moe + p[:, kk:kk + 1] * yg_ref[kk].astype(F32)
    return x_ref[...] + gate2 * moe


def _combine_final_kernel(x_ref, mod_ref, yg_ref, p_ref, gf_ref, yc_ref, yl_ref, *, nbc):
    i = pl.program_id(0)
    y = _rms(_moe_sum(x_ref, mod_ref, yg_ref, p_ref)) * gf_ref[...]

    @pl.when(i < nbc)
    def _():
        yc_ref[...] = y

    @pl.when(i >= nbc)
    def _():
        yl_ref[...] = y


def _combine(geo, layer, x, mods, yg, top_p, final_g):
    d = D_MODEL
    tb = TOKEN_BLOCK
    row = pl.BlockSpec((tb, d), lambda i: (i, 0))
    in_specs = [row, geo.mod_spec(layer, tb),
                pl.BlockSpec((TOP_K, tb, d), lambda i: (0, i, 0)),
                pl.BlockSpec((tb, TOP_K), lambda i: (i, 0))]
    ctx, lat = geo.split_specs(tb, d)
    return pl.pallas_call(
        functools.partial(_combine_final_kernel, nbc=geo.nc // tb),
        grid=(geo.n // tb,),
        in_specs=in_specs + [pl.BlockSpec(final_g.shape, lambda i: (0, 0))],
        out_specs=[ctx, lat],
        out_shape=[jax.ShapeDtypeStruct((geo.nc, d), F32), jax.ShapeDtypeStruct((geo.nl, d), F32)],
        compiler_params=_cparams(1), name="moe_combine_final",
    )(x, mods, yg, top_p, final_g)


@functools.lru_cache(maxsize=None)
def _rope_tables(length):
    rows = length // GRID_W
    row = np.repeat(np.arange(rows), GRID_W).astype(np.float32)
    col = np.tile(np.arange(GRID_W), rows).astype(np.float32)
    ax = DA_HEAD_DIM // 2
    inv = (ROPE_BASE ** (-(np.arange(ax // 2, dtype=np.float32) * 2.0 / ax))).astype(np.float32)
    ang_r = (row[:, None] * inv).astype(np.float32)
    ang_c = (col[:, None] * inv).astype(np.float32)
    ang = np.concatenate([ang_r, ang_r, ang_c, ang_c], axis=-1).astype(np.float64)
    sign = np.where((np.arange(DA_HEAD_DIM) % 32) < 16, -1.0, 1.0)
    reps = DA_WIDTH // DA_HEAD_DIM
    cos = np.tile(np.cos(ang), (1, reps)).astype(np.float32)
    sin_signed = np.tile(np.sin(ang) * sign[None, :], (1, reps)).astype(np.float32)
    return jnp.asarray(cos), jnp.asarray(sin_signed)


def kernel(x_prompt, x_sample, cache_k, cache_v, c, c_ctx, w_ada, b_ada, norm1, norm2, w_in, w_pool, pool_scale, da_lambda, da_subln, hy_conv_w, hy_conv_b, hy_f_w1, hy_f_b1, hy_f_w2, hy_f_b2, hy_f_w3, hy_sin_freq, hy_skip, w_branch, w_gate, b_gate, w_o, w_router, b_router, w_gu, b_gu, w_down, b_down, final_norm):
    bc, lc, d = x_prompt.shape
    bl, ll, _ = x_sample.shape
    depth = w_in.shape[0]
    n_experts = w_router.shape[-1]
    past = cache_k.shape[2]
    geo = _Geom(bc, lc, bl, ll)
    assert 1 + bl <= MOD_ROWS

    x = jnp.concatenate([x_prompt.reshape(bc * lc, d), x_sample.reshape(bl * ll, d)], axis=0)
    cond = jnp.concatenate([c_ctx[None], c, jnp.zeros((MOD_ROWS - 1 - bl, d), F32)], axis=0)
    mods = _ada(cond, w_ada, b_ada).reshape(depth * MOD_ROWS, 1, N_MOD * d)
    cos_t, sin_t = _rope_tables(ll)
    ck = cache_k.reshape(bl, depth, past, DA_WIDTH)
    cv = cache_v.reshape(bl, depth, past, DA_WIDTH)

    row3 = lambda a: a.reshape(depth, 1, a.shape[-1])
    g1, g2 = row3(norm1), row3(norm2)
    w_in_b = w_in.astype(BF16)
    w_pool_b = w_pool.astype(BF16)
    w_gate_b = w_gate.astype(BF16)
    w_branch_b = w_branch.astype(BF16)
    w_o_b = w_o.astype(BF16)
    w1p = jnp.pad(hy_f_w1, ((0, 0), (0, 64 - HY_EMB), (0, 0)))
    fargs = (w1p, row3(hy_f_b1), hy_f_w2, row3(hy_f_b2), hy_f_w3, hy_sin_freq)
    w_gu_s = w_gu.reshape(depth * n_experts, d, 2 * D_FF)
    b_gu_s = b_gu.reshape(depth * n_experts, 1, 2 * D_FF)
    w_down_s = w_down.reshape(depth * n_experts, D_FF, d)
    b_down_s = b_down.reshape(depth * n_experts, 1, d)

    new_k = jnp.zeros((bc, depth, lc * DA_HEADS, DA_VDIM), F32)
    new_v = jnp.zeros((bc, depth, lc * DA_HEADS, DA_VDIM), F32)
    moe = None
    for l in range(depth):
        x, u_pool, q, kb, vb, new_k, new_v, u_hy = _inproj(geo, l, x, mods, g1, w_in_b, cos_t, sin_t,
                                                           new_k, new_v, moe)
        a_out, zb, x0 = _local_mix(geo, l, u_pool, w_pool_b, row3(pool_scale),
                                   u_hy, hy_conv_w, row3(hy_conv_b))

        lam_init = 0.8 - 0.6 * math.exp(-0.3 * l)
        subln = row3(da_subln)
        b_ctx = _attn_ctx(geo, l, q, kb, vb, da_lambda, subln, lam_init)
        b_lat = _attn_lat(geo, l, q, kb, vb, ck, cv, da_lambda, subln, lam_init)

        skip = row3(hy_skip)
        c_ctx_out = _hyena_long_conv(lc, l, bc, 0, zb, x0, _hyfilter(lc, l, *fargs), skip)
        c_lat_out = _hyena_long_conv(ll, l, bl, geo.nc // ll, zb, x0, _hyfilter(ll, l, *fargs), skip)

        x, h2, top_i, top_p = _merge(geo, l, x, mods, g1, a_out, b_ctx, b_lat, c_ctx_out, c_lat_out,
                                     w_gate_b, row3(b_gate), w_branch_b, w_o_b, g2, w_router, row3(b_router))

        runs, cols, (step_expert, step_tile, step_sub, n_valid) = _route(top_i, n_experts)
        xs, pair_slot = _dispatch(h2, top_i.T, runs, cols, n_experts)
        ys = _ffn(xs, step_expert + l * n_experts, step_tile, step_sub, n_valid,
                  w_gu_s, b_gu_s, w_down_s, b_down_s)
        yg = jnp.take(ys, pair_slot, axis=0, mode="clip")
        moe = (yg, top_p)

    y_ctx, y_lat = _combine(geo, depth - 1, x, mods, yg, top_p, final_norm[None])

    return (y_ctx.reshape(bc, lc, d), y_lat.reshape(bl, ll, d),
            new_k.reshape(bc, depth, lc, DA_HEADS, 2 * DA_HEAD_DIM),
            new_v.reshape(bc, depth, lc, DA_HEADS, DA_VDIM))
```

```python
import functools
import math

import numpy as np
import jax
import jax.numpy as jnp
from jax import lax
from jax.experimental import pallas as pl
from jax.experimental.pallas import tpu as pltpu

F32 = jnp.float32
BF16 = jnp.bfloat16

D_MODEL = 1024
GRID_W = 64
NORM_EPS = 1e-6
POOL_WIDTH = 512
POOL_WINDOWS = (2, 4, 8, 16)
POOL_GC = POOL_WIDTH // len(POOL_WINDOWS)
DA_HEADS = 4
DA_HEAD_DIM = 64
DA_VDIM = 2 * DA_HEAD_DIM
DA_WIDTH = DA_HEADS * DA_VDIM
ROPE_BASE = 10000.0
HY_WIDTH = 512
HY_EMB = 33
HY_BANDS = (HY_EMB - 1) // 2
HY_HIDDEN = 64
HY_FAST = 0.3
HY_SLOW = 1.5
HY_TARGET = 1e-2
N_BRANCH = 3
D_IN = POOL_WIDTH + 3 * DA_WIDTH + 3 * HY_WIDTH
TOP_K = 4
D_FF = 1024
SWIGLU_ALPHA = 1.702
SWIGLU_LIMIT = 7.0
N_MOD = 6
MOD_ROWS = 8

TOKEN_BLOCK = 256
WIDE_BLOCK = 512
MERGE_BLOCK = 1024
DFT_ROWS = 1024
HALO = 16
MOE_TILE = 128
MOE_MAX_SUB = 8
DISPATCH_BLOCK = 512
RUN_ALIGN = 16
V7X_VMEM_LIMIT = 56 * 1024 * 1024


def _cparams(n_axes):
    return pltpu.CompilerParams(
        dimension_semantics=("arbitrary",) * n_axes,
        vmem_limit_bytes=V7X_VMEM_LIMIT)


def _dot(a, b):
    return jnp.dot(a, b, preferred_element_type=F32)


def _dot_nt(a, b):
    return lax.dot_general(a, b, (((1,), (1,)), ((), ())), preferred_element_type=F32)


def _split_bf16(a):
    hi = a.astype(BF16)
    lo = (a - hi.astype(F32)).astype(BF16)
    return hi, lo


def _dot3(a, b):
    ah, al = _split_bf16(a)
    bh, bl = _split_bf16(b)
    return _dot(ah, bh) + _dot(al, bh) + _dot(ah, bl)


def _sigmoid(x):
    return 0.5 * jnp.tanh(0.5 * x) + 0.5


def _rms(x):
    return x * lax.rsqrt(jnp.mean(x * x, axis=-1, keepdims=True) + NORM_EPS)


class _Geom:
    def __init__(self, n_ctx_seq, ctx_len, n_lat_seq, lat_len):
        assert ctx_len == TOKEN_BLOCK, "one context sequence per token block"
        assert lat_len % WIDE_BLOCK == 0 and (n_ctx_seq * ctx_len) % lat_len == 0
        self.bc, self.lc, self.bl, self.ll = n_ctx_seq, ctx_len, n_lat_seq, lat_len
        self.nc = n_ctx_seq * ctx_len
        self.nl = n_lat_seq * lat_len
        self.n = self.nc + self.nl

    def group(self, i, tb):
        nbc = self.nc // tb
        return jnp.where(i < nbc, 0, 1 + (i - nbc) // (self.ll // tb))

    def pos_block(self, i, tb):
        nbc = self.nc // tb
        return jnp.where(i < nbc, 0, (i - nbc) % (self.ll // tb))

    def is_start(self, i):
        nbc, bpl = self.nc // TOKEN_BLOCK, self.ll // TOKEN_BLOCK
        return jnp.logical_or(i < nbc, (i - nbc) % bpl == 0)

    def is_end(self, i):
        nbc, bpl = self.nc // TOKEN_BLOCK, self.ll // TOKEN_BLOCK
        return jnp.logical_or(i < nbc, (i - nbc) % bpl == bpl - 1)

    def mod_spec(self, layer, tb):
        return pl.BlockSpec((1, 1, N_MOD * D_MODEL),
                            lambda i: (layer * MOD_ROWS + self.group(i, tb), 0, 0))

    def halo_specs(self, width):
        per = TOKEN_BLOCK // HALO
        last = self.n // HALO - 1
        before = pl.BlockSpec((HALO, width), lambda i: (jnp.maximum(i * per - 1, 0), 0))
        after = pl.BlockSpec((HALO, width), lambda i: (jnp.minimum((i + 1) * per, last), 0))
        return before, after

    def split_specs(self, tb, width):
        nbc = self.nc // tb
        last_lat = self.nl // tb - 1
        ctx = pl.BlockSpec((tb, width), lambda i: (jnp.minimum(i, nbc - 1), 0))
        lat = pl.BlockSpec((tb, width), lambda i: (jnp.clip(i - nbc, 0, last_lat), 0))
        return ctx, lat


def _layer_spec(a, layer, single_buffer=False):
    kw = dict(pipeline_mode=pl.Buffered(1)) if single_buffer else {}
    return pl.BlockSpec((None,) + a.shape[1:], lambda *_: (layer,) + (0,) * (a.ndim - 1), **kw)


def _ada_kernel(c_ref, w_ref, b_ref, o_ref):
    c = c_ref[...]
    s = c * _sigmoid(c)
    o_ref[0] = _dot(s.astype(BF16), w_ref[0].astype(BF16)) + b_ref[0]


def _ada(cond, w_ada, b_ada):
    depth, d, n6 = w_ada.shape
    rows = cond.shape[0]
    tn = 1024
    return pl.pallas_call(
        _ada_kernel,
        grid=(depth, n6 // tn),
        in_specs=[pl.BlockSpec((rows, d), lambda l, j: (0, 0)),
                  pl.BlockSpec((1, d, tn), lambda l, j: (l, 0, j)),
                  pl.BlockSpec((1, 1, tn), lambda l, j: (l, 0, j))],
        out_specs=pl.BlockSpec((1, rows, tn), lambda l, j: (l, 0, j)),
        out_shape=jax.ShapeDtypeStruct((depth, rows, n6), F32),
        compiler_params=_cparams(2),
        name="ada_mod",
    )(cond, w_ada, b_ada.reshape(depth, 1, n6))


def _rope(x, cos, sin_signed, first_half):
    d = x.shape[-1]
    partner = jnp.where(first_half, pltpu.roll(x, d - 16, 1), pltpu.roll(x, 16, 1))
    return x * cos + partner * sin_signed


def _inproj_kernel(*refs, nbc, lc, add_moe):
    if add_moe:
        x_ref, pmod_ref, yg_ref, p_ref = refs[:4]
        refs = refs[4:]
        (mod_ref, g_ref, w_ref, cos_ref, sin_ref, _, _,
         xo_ref, up_ref, q_ref, kb_ref, vb_ref, kf_ref, vf_ref, uh_ref) = refs
        x = _moe_sum(x_ref, pmod_ref, yg_ref, p_ref)
        xo_ref[...] = x
    else:
        (x_ref, mod_ref, g_ref, w_ref, cos_ref, sin_ref, _, _,
         up_ref, q_ref, kb_ref, vb_ref, kf_ref, vf_ref, uh_ref) = refs
        x = x_ref[...]
    i = pl.program_id(0)
    d = D_MODEL
    mod = mod_ref[0]
    shift, scale = mod[:, 0:d], mod[:, d:2 * d]
    h = (_rms(x) * g_ref[...] * (1.0 + scale) + shift).astype(BF16)

    c1 = POOL_WIDTH
    c2 = c1 + DA_WIDTH
    c3 = c2 + DA_WIDTH
    c4 = c3 + DA_WIDTH
    up_ref[...] = _dot(h, w_ref[:, 0:c1])
    uh_ref[...] = _dot(h, w_ref[:, c4:D_IN]).astype(BF16)
    q = _dot(h, w_ref[:, c1:c2]) * (DA_HEAD_DIM ** -0.5 * math.log2(math.e))
    k = _dot(h, w_ref[:, c2:c3])
    v = _dot(h, w_ref[:, c3:c4])
    vb_ref[...] = v.astype(BF16)

    @pl.when(i < nbc)
    def _():
        q_ref[...] = q.astype(BF16)
        kb_ref[...] = k.astype(BF16)
        for s in range(k.shape[0] // lc):
            for hh in range(DA_HEADS):
                rows = pl.ds(hh, lc, stride=DA_HEADS)
                cols = slice(hh * DA_VDIM, (hh + 1) * DA_VDIM)
                kf_ref[s, 0, rows, :] = k[s * lc:(s + 1) * lc, cols]
                vf_ref[s, 0, rows, :] = v[s * lc:(s + 1) * lc, cols]

    @pl.when(i >= nbc)
    def _():
        cos, sin_signed = cos_ref[...], sin_ref[...]
        lane = lax.broadcasted_iota(jnp.int32, q.shape, 1)
        first_half = (lane % 32) < 16
        q_ref[...] = _rope(q, cos, sin_signed, first_half).astype(BF16)
        kb_ref[...] = _rope(k, cos, sin_signed, first_half).astype(BF16)


def _inproj(geo, layer, x, mods, g1, w_in_b, cos_t, sin_t, kacc, vacc, moe=None):
    d = D_MODEL
    tb = WIDE_BLOCK
    nbc = geo.nc // tb
    spb = tb // geo.lc
    row = lambda w: pl.BlockSpec((tb, w), lambda i: (i, 0))
    cache = pl.BlockSpec((spb, 1, geo.lc * DA_HEADS, DA_VDIM), lambda i: (jnp.minimum(i, nbc - 1), layer, 0, 0))
    tab = pl.BlockSpec((tb, DA_WIDTH), lambda i: (geo.pos_block(i, tb), 0))
    hbm = pl.BlockSpec(memory_space=pl.ANY)
    in_specs = [geo.mod_spec(layer, tb), _layer_spec(g1, layer),
                _layer_spec(w_in_b, layer, single_buffer=True), tab, tab, hbm, hbm]
    args = [mods, g1, w_in_b, cos_t, sin_t, kacc, vacc]
    out_specs = [row(POOL_WIDTH), row(DA_WIDTH), row(DA_WIDTH), row(DA_WIDTH), cache, cache, row(3 * HY_WIDTH)]
    out_shape = [jax.ShapeDtypeStruct((geo.n, POOL_WIDTH), F32),
                 jax.ShapeDtypeStruct((geo.n, DA_WIDTH), BF16),
                 jax.ShapeDtypeStruct((geo.n, DA_WIDTH), BF16),
                 jax.ShapeDtypeStruct((geo.n, DA_WIDTH), BF16),
                 jax.ShapeDtypeStruct(kacc.shape, F32),
                 jax.ShapeDtypeStruct(vacc.shape, F32),
                 jax.ShapeDtypeStruct((geo.n, 3 * HY_WIDTH), BF16)]
    if moe is None:
        in_specs = [row(d)] + in_specs
        args = [x] + args
    else:
        yg, top_p = moe
        in_specs = [row(d), geo.mod_spec(layer - 1, tb),
                    pl.BlockSpec((TOP_K, tb, d), lambda i: (0, i, 0)),
                    pl.BlockSpec((tb, TOP_K), lambda i: (i, 0))] + in_specs
        args = [x, mods, yg, top_p] + args
        out_specs = [row(d)] + out_specs
        out_shape = [jax.ShapeDtypeStruct((geo.n, d), F32)] + out_shape
    n_in, n_out = len(args), len(out_shape)
    outs = pl.pallas_call(
        functools.partial(_inproj_kernel, nbc=nbc, lc=geo.lc, add_moe=moe is not None),
        grid=(geo.n // tb,),
        in_specs=in_specs, out_specs=out_specs, out_shape=out_shape,
        input_output_aliases={n_in - 2: n_out - 3, n_in - 1: n_out - 2},
        compiler_params=_cparams(1),
        name="in_proj",
    )(*args)
    return outs if moe is not None else [x] + list(outs)


def _fill_padded(pad_ref, before_ref, main_ref, after_ref, start, end):
    tb = TOKEN_BLOCK
    zero = jnp.zeros(before_ref.shape, F32)
    pad_ref[0:HALO, :] = jnp.where(start, zero, before_ref[...].astype(F32))
    pad_ref[HALO:HALO + tb, :] = main_ref[...].astype(F32)
    pad_ref[HALO + tb:2 * HALO + tb, :] = jnp.where(end, zero, after_ref[...].astype(F32))


def _pool_kernel(main_ref, before_ref, after_ref, w_ref, s_ref, o_ref, pad_ref, *, geo):
    i = pl.program_id(0)
    tb = TOKEN_BLOCK
    start, end = geo.is_start(i), geo.is_end(i)
    _fill_padded(pad_ref, before_ref, main_ref, after_ref, start, end)
    r = lax.broadcasted_iota(jnp.int32, (tb, 1), 0)
    for g, w in enumerate(POOL_WINDOWS):
        cols = slice(g * POOL_GC, (g + 1) * POOL_GC)
        acc = pad_ref[HALO - w // 2:HALO - w // 2 + tb, cols]
        for j in range(-w // 2 + 1, w // 2):
            acc = acc + pad_ref[HALO + j:HALO + j + tb, cols]
        lo = jnp.where(start, jnp.maximum(r - w // 2, 0), r - w // 2)
        hi = jnp.where(end, jnp.minimum(r + w // 2, tb), r + w // 2)
        mean = acc / (hi - lo).astype(F32)
        dlt = mean - main_ref[:, cols]
        y = _dot(dlt.astype(BF16), w_ref[g])
        o_ref[:, cols] = (y * s_ref[:, cols]).astype(BF16)


def _attn_kernel(*refs, lam_init, has_ctx):
    if has_ctx:
        q_ref, k_ref, v_ref, kc_ref, vc_ref, lam_ref, g_ref, o_ref = refs
    else:
        q_ref, k_ref, v_ref, lam_ref, g_ref, o_ref = refs
    lp = lam_ref[...]
    lam = (jnp.exp(jnp.sum(lp[0:1] * lp[1:2], axis=-1, keepdims=True))
           - jnp.exp(jnp.sum(lp[2:3] * lp[3:4], axis=-1, keepdims=True)) + lam_init)
    for h in range(DA_HEADS):
        vcols = slice(h * DA_VDIM, (h + 1) * DA_VDIM)
        v = v_ref[0, :, vcols]
        if has_ctx:
            ones = lambda a: jnp.concatenate([a, jnp.ones_like(a)], axis=1)
            v = ones(v)
            vc = ones(vc_ref[0, 0, :, vcols].astype(BF16))
        outs = []
        for sub in range(2):
            c0 = h * DA_VDIM + sub * DA_HEAD_DIM
            cols = slice(c0, c0 + DA_HEAD_DIM)
            qs = q_ref[0, :, cols]
            s1 = _dot_nt(qs, k_ref[0, :, cols])
            m = jnp.max(s1, axis=-1, keepdims=True)
            if has_ctx:
                s2 = _dot_nt(qs, kc_ref[0, 0, :, cols].astype(BF16))
                m = jnp.maximum(m, jnp.max(s2, axis=-1, keepdims=True))
            e1 = jnp.exp2(s1 - m)
            pv = _dot(e1.astype(BF16), v)
            if has_ctx:
                pv = pv + _dot(jnp.exp2(s2 - m).astype(BF16), vc)
                outs.append(pv[:, 0:DA_VDIM] / pv[:, DA_VDIM:DA_VDIM + 1])
            else:
                outs.append(pv / jnp.sum(e1, axis=-1, keepdims=True))
        o = outs[0] - lam * outs[1]
        o = _rms(o) * g_ref[...] * (1.0 - lam_init)
        o_ref[0, :, vcols] = o.astype(BF16)


def _attn_ctx(geo, layer, q, kb, vb, lam_p, subln, lam_init):
    bc, lc = geo.bc, geo.lc
    seq = pl.BlockSpec((1, lc, DA_WIDTH), lambda b: (b, 0, 0))
    view = lambda a: a.reshape(geo.n // lc, lc, DA_WIDTH)
    out = pl.pallas_call(
        functools.partial(_attn_kernel, lam_init=lam_init, has_ctx=False),
        grid=(bc,),
        in_specs=[seq, seq, seq, _layer_spec(lam_p, layer), _layer_spec(subln, layer)],
        out_specs=seq,
        out_shape=jax.ShapeDtypeStruct((bc, lc, DA_WIDTH), BF16),
        compiler_params=_cparams(1),
        name="attn_ctx",
    )(view(q), view(kb), view(vb), lam_p, subln)
    return out.reshape(bc * lc, DA_WIDTH)


def _attn_lat(geo, layer, q, kb, vb, cache_k, cache_v, lam_p, subln, lam_init):
    bl, ll = geo.bl, geo.ll
    tq = TOKEN_BLOCK
    past = cache_k.shape[2]
    off = geo.nc // ll
    qblk = pl.BlockSpec((1, tq, DA_WIDTH), lambda b, j: (b + off, j, 0))
    seq = pl.BlockSpec((1, ll, DA_WIDTH), lambda b, j: (b + off, 0, 0))
    cache = pl.BlockSpec((1, 1, past, DA_WIDTH), lambda b, j: (b, layer, 0, 0))
    view = lambda a: a.reshape(geo.n // ll, ll, DA_WIDTH)
    out = pl.pallas_call(
        functools.partial(_attn_kernel, lam_init=lam_init, has_ctx=True),
        grid=(bl, ll // tq),
        in_specs=[qblk, seq, seq, cache, cache, _layer_spec(lam_p, layer), _layer_spec(subln, layer)],
        out_specs=pl.BlockSpec((1, tq, DA_WIDTH), lambda b, j: (b, j, 0)),
        out_shape=jax.ShapeDtypeStruct((bl, ll, DA_WIDTH), BF16),
        compiler_params=_cparams(2),
        name="attn_lat",
    )(view(q), view(kb), view(vb), cache_k, cache_v, lam_p, subln)
    return out.reshape(bl * ll, DA_WIDTH)


@functools.lru_cache(maxsize=None)
def _dft_tables(length):
    k = np.arange(length, dtype=np.int64)
    ks = (k[:, None] * k[None, :]) % (2 * length)
    ang = ks.astype(np.float64) * (np.pi / length)
    cmat = np.cos(ang)
    smat = -np.sin(ang)
    smat[0, :] = 1.0 - 2.0 * (k % 2)
    to_bf16 = lambda a: jnp.asarray(a.astype(np.float32)).astype(BF16)
    return to_bf16(cmat), to_bf16(smat), to_bf16(smat.T)


@functools.lru_cache(maxsize=None)
def _filter_features(length):
    t = np.linspace(0.0, 1.0, length, dtype=np.float32)
    w_ang = (2.0 * math.pi * np.arange(length, dtype=np.float32) / length).astype(np.float32)
    f = np.linspace(1e-4, HY_BANDS - 1, HY_BANDS, dtype=np.float32)
    arg = (w_ang[:, None] * f[None, :]).astype(np.float32).astype(np.float64)
    z = np.concatenate([t[:, None].astype(np.float64), np.cos(arg), -np.sin(arg)], axis=-1)
    z = np.pad(z, ((0, 0), (0, 64 - HY_EMB))).astype(np.float32)
    deltas = np.linspace(math.log(HY_TARGET) / HY_FAST, math.log(HY_TARGET) / HY_SLOW,
                         HY_WIDTH, dtype=np.float32)
    rates = np.abs(np.concatenate([deltas, deltas]))[None, :]
    return jnp.asarray(z), jnp.asarray(rates)


def _hyfilter_kernel(z_ref, dl_ref, w1_ref, b1_ref, w2_ref, b2_ref, w3_ref, fr_ref, o_ref):
    c = pl.program_id(0)
    fr = fr_ref[...]
    z = z_ref[...]
    hid = jnp.sin(fr[0:1] * (_dot3(z, w1_ref[...]) + b1_ref[...]))
    hid = jnp.sin(fr[1:2] * (_dot3(hid, w2_ref[...]) + b2_ref[...]))
    h = _dot3(hid, w3_ref[...]) * jnp.exp(-z[:, 0:1] * dl_ref[...])
    row = lax.broadcasted_iota(jnp.int32, h.shape, 0) + c * z.shape[0]
    col = lax.broadcasted_iota(jnp.int32, h.shape, 1)
    o_ref[0] = jnp.where(jnp.logical_and(row == 0, col >= HY_WIDTH), 0.0, h).astype(BF16)


def _hyfilter(length, layer, w1p, b1, w2, b2, w3, fr):
    z, dl = _filter_features(length)
    tb = TOKEN_BLOCK
    lay = lambda a: _layer_spec(a, layer)
    return pl.pallas_call(
        _hyfilter_kernel,
        grid=(length // tb,),
        in_specs=[pl.BlockSpec((tb, 64), lambda c: (c, 0)), pl.BlockSpec(dl.shape, lambda c: (0, 0)),
                  lay(w1p), lay(b1), lay(w2), lay(b2), lay(w3), lay(fr)],
        out_specs=pl.BlockSpec((1, tb, 2 * HY_WIDTH), lambda c: (0, c, 0)),
        out_shape=jax.ShapeDtypeStruct((1, length, 2 * HY_WIDTH), BF16),
        compiler_params=_cparams(1),
        name="hyena_filter",
    )(z, dl, w1p, b1, w2, b2, w3, fr)


def _hyconv_kernel(main_ref, before_ref, after_ref, cw_ref, cb_ref, zb_ref, x0_ref, pad_ref, *, geo):
    i = pl.program_id(0)
    tb = TOKEN_BLOCK
    _fill_padded(pad_ref, before_ref, main_ref, after_ref, geo.is_start(i), geo.is_end(i))
    w = HY_WIDTH
    parts = []
    for p in range(3):
        cols = slice(p * w, (p + 1) * w)
        uc = (pad_ref[HALO - 1:HALO - 1 + tb, cols] * cw_ref[0:1, cols]
              + pad_ref[HALO:HALO + tb, cols] * cw_ref[1:2, cols]
              + pad_ref[HALO + 1:HALO + 1 + tb, cols] * cw_ref[2:3, cols]
              + cb_ref[:, cols])
        parts.append(uc)
    x0, x1, v = parts
    zb_ref[...] = (v * x1).astype(BF16)
    x0_ref[...] = x0.astype(BF16)


def _local_mix_kernel(pm_ref, pb_ref, pa_ref, wp_ref, ps_ref, hm_ref, hb_ref, ha_ref, cw_ref, cb_ref,
                      a_ref, zb_ref, x0_ref, pad_p, pad_h, *, geo):
    _pool_kernel(pm_ref, pb_ref, pa_ref, wp_ref, ps_ref, a_ref, pad_p, geo=geo)
    _hyconv_kernel(hm_ref, hb_ref, ha_ref, cw_ref, cb_ref, zb_ref, x0_ref, pad_h, geo=geo)


def _local_mix(geo, layer, u_pool, w_pool_b, pool_scale, u_hy, conv_w, conv_b):
    tb = TOKEN_BLOCK
    w3 = 3 * HY_WIDTH
    p_before, p_after = geo.halo_specs(POOL_WIDTH)
    h_before, h_after = geo.halo_specs(w3)
    row = lambda w: pl.BlockSpec((tb, w), lambda i: (i, 0))
    return pl.pallas_call(
        functools.partial(_local_mix_kernel, geo=geo),
        grid=(geo.n // tb,),
        in_specs=[row(POOL_WIDTH), p_before, p_after, _layer_spec(w_pool_b, layer), _layer_spec(pool_scale, layer),
                  row(w3), h_before, h_after, _layer_spec(conv_w, layer), _layer_spec(conv_b, layer)],
        out_specs=[row(POOL_WIDTH), row(HY_WIDTH), row(HY_WIDTH)],
        out_shape=[jax.ShapeDtypeStruct((geo.n, POOL_WIDTH), BF16),
                   jax.ShapeDtypeStruct((geo.n, HY_WIDTH), BF16),
                   jax.ShapeDtypeStruct((geo.n, HY_WIDTH), BF16)],
        scratch_shapes=[pltpu.VMEM((tb + 2 * HALO, POOL_WIDTH), F32), pltpu.VMEM((tb + 2 * HALO, w3), F32)],
        compiler_params=_cparams(1),
        name="pool_and_conv_gate",
    )(u_pool, u_pool, u_pool, w_pool_b, pool_scale, u_hy, u_hy, u_hy, conv_w, conv_b)


def _seqs_per_step(length, nseq):
    g = max(1, min(nseq, 2048 // length))
    assert nseq % g == 0
    return g


def _dft_fwd_kernel(c_ref, s_ref, x_ref, re_ref, im_ref):
    for s in range(x_ref.shape[0]):
        x = x_ref[s]
        re_ref[s] = _dot(c_ref[...], x)
        im_ref[s] = _dot(s_ref[...], x)


def _dft_fwd(length, x, nseq, seq_off):
    cmat, smat, _ = _dft_tables(length)
    n = x.shape[-1]
    tm = min(length, DFT_ROWS)
    tn = 512
    g = _seqs_per_step(length, nseq)
    assert seq_off % g == 0
    a_spec = pl.BlockSpec((tm, length), lambda b, j, m: (m, 0))
    o_spec = pl.BlockSpec((g, tm, tn), lambda b, j, m: (b, m, j))
    return pl.pallas_call(
        _dft_fwd_kernel,
        grid=(nseq // g, n // tn, length // tm),
        in_specs=[a_spec, a_spec, pl.BlockSpec((g, length, tn), lambda b, j, m: (b + seq_off // g, 0, j))],
        out_specs=[o_spec, o_spec],
        out_shape=[jax.ShapeDtypeStruct((nseq, length, n), F32)] * 2,
        compiler_params=_cparams(3),
        name="hyena_dft",
    )(cmat, smat, x)


def _hyprod_kernel(zr_ref, zi_ref, hfr_ref, hgr_ref, hfi_ref, hgi_ref, yr_ref, yi_ref, *, length):
    c = pl.program_id(1)
    shape = zr_ref.shape[1:]
    k = lax.broadcasted_iota(jnp.int32, shape, 0) + c * shape[0]
    first = k == 0
    hr = hfr_ref[0] + hgr_ref[0]
    hi = jnp.where(first, hfi_ref[0] + hgi_ref[0], hfi_ref[0] - hgi_ref[0])
    inv = 1.0 / length
    for s in range(zr_ref.shape[0]):
        zr, zi = zr_ref[s], zi_ref[s]
        yr = (zr * hr - zi * hi) * inv
        yi = (zr * hi + zi * hr) * inv
        yr_ref[s] = jnp.where(first, zr * hr * (0.5 * inv), yr).astype(BF16)
        yi_ref[s] = jnp.where(first, zi * hi * (0.5 * inv), yi).astype(BF16)


def _hyprod(length, zre, zim, hre, him):
    nb = zre.shape[0]
    tb = TOKEN_BLOCK
    g = _seqs_per_step(length, nb)
    zs = pl.BlockSpec((g, tb, HY_WIDTH), lambda b, c: (b, c, 0))
    hf = pl.BlockSpec((1, tb, HY_WIDTH), lambda b, c: (0, c, 0))
    hg = pl.BlockSpec((1, tb, HY_WIDTH), lambda b, c: (0, c, 1))
    return pl.pallas_call(
        functools.partial(_hyprod_kernel, length=length),
        grid=(nb // g, length // tb),
        in_specs=[zs, zs, hf, hg, hf, hg],
        out_specs=[zs, zs],
        out_shape=[jax.ShapeDtypeStruct((nb, length, HY_WIDTH), BF16)] * 2,
        compiler_params=_cparams(2),
        name="hyena_spectral_product",
    )(zre, zim, hre, hre, him, him)


def _hyinv_kernel(c_ref, st_ref, yr_ref, yi_ref, z_ref, x0_ref, skip_ref, o_ref):
    for s in range(yr_ref.shape[0]):
        y = _dot(c_ref[...], yr_ref[s]) + _dot(st_ref[...], yi_ref[s])
        o_ref[s] = ((y + z_ref[s].astype(F32) * skip_ref[...]) * x0_ref[s].astype(F32)).astype(BF16)


def _hyinv(length, layer, yr, yi, z, x0, skip, seq_off):
    cmat, _, smat_t = _dft_tables(length)
    nseq = yr.shape[0]
    tm = min(length, DFT_ROWS)
    g = _seqs_per_step(length, nseq)
    assert seq_off % g == 0
    a_spec = pl.BlockSpec((tm, length), lambda b, m: (m, 0))
    y_spec = pl.BlockSpec((g, length, HY_WIDTH), lambda b, m: (b, 0, 0))
    t_spec = pl.BlockSpec((g, tm, HY_WIDTH), lambda b, m: (b + seq_off // g, m, 0))
    return pl.pallas_call(
        _hyinv_kernel,
        grid=(nseq // g, length // tm),
        in_specs=[a_spec, a_spec, y_spec, y_spec, t_spec, t_spec, _layer_spec(skip, layer)],
        out_specs=pl.BlockSpec((g, tm, HY_WIDTH), lambda b, m: (b, m, 0)),
        out_shape=jax.ShapeDtypeStruct((nseq, length, HY_WIDTH), BF16),
        compiler_params=_cparams(2),
        name="hyena_idft",
    )(cmat, smat_t, yr, yi, z, x0, skip)


def _hyena_long_conv(length, layer, nseq, seq_off, zb, x0, filt, skip):
    view = lambda a: a.reshape(a.shape[0] // length, length, HY_WIDTH)
    zre, zim = _dft_fwd(length, view(zb), nseq, seq_off)
    hre, him = _dft_fwd(length, filt, 1, 0)
    yr, yi = _hyprod(length, zre, zim, hre, him)
    out = _hyinv(length, layer, yr, yi, view(zb), view(x0), skip, seq_off)
    return out.reshape(nseq * length, HY_WIDTH)


def _merge_kernel(x_ref, mod_ref, g1_ref, a_ref, bc_ref, bl_ref, cc_ref, cl_ref, wg_ref, bg_ref, wbr_ref,
                  wo_ref, g2_ref, wr_ref, br_ref, xo_ref, h2_ref, ti_ref, tp_ref, *, nbc):
    d = D_MODEL
    is_ctx = pl.program_id(0) < nbc
    mod = mod_ref[0]
    shift1, scale1, gate1 = mod[:, 0:d], mod[:, d:2 * d], mod[:, 2 * d:3 * d]
    shift2, scale2 = mod[:, 3 * d:4 * d], mod[:, 4 * d:5 * d]
    x = x_ref[...]
    h = (_rms(x) * g1_ref[...] * (1.0 + scale1) + shift1).astype(BF16)
    branches = (a_ref[...],
                jnp.where(is_ctx, bc_ref[...], bl_ref[...]),
                jnp.where(is_ctx, cc_ref[...], cl_ref[...]))
    merged = None
    for n, br_n in enumerate(branches):
        cols = slice(n * d, (n + 1) * d)
        gate = _sigmoid(_dot(h, wg_ref[:, cols]) + bg_ref[:, cols])
        term = gate * _dot(br_n, wbr_ref[n])
        merged = term if merged is None else merged + term
    x = x + gate1 * _dot(merged.astype(BF16), wo_ref[...])
    xo_ref[...] = x
    h2 = _rms(x) * g2_ref[...] * (1.0 + scale2) + shift2
    h2_ref[...] = h2.astype(BF16)

    logits = _dot3(h2, wr_ref[...]) + br_ref[...]
    ne = logits.shape[-1]
    lane = lax.broadcasted_iota(jnp.int32, logits.shape, 1).astype(F32)
    vals = logits
    top_v, top_i = [], []
    for _ in range(TOP_K):
        m = jnp.max(vals, axis=-1, keepdims=True)
        idx = jnp.min(jnp.where(vals == m, lane, float(ne)), axis=-1, keepdims=True)
        top_v.append(m)
        top_i.append(idx)
        vals = jnp.where(lane == idx, -jnp.inf, vals)
    es = [jnp.exp(v - top_v[0]) for v in top_v]
    den = es[0] + es[1] + es[2] + es[3]
    for kk in range(TOP_K):
        ti_ref[:, kk:kk + 1] = top_i[kk].astype(jnp.int32)
        tp_ref[:, kk:kk + 1] = es[kk] / den


def _merge(geo, layer, x, mods, g1, a_out, b_ctx, b_lat, c_ctx, c_lat, wg_b, bg, wbr_b, wo_b, g2, wr, br):
    d = D_MODEL
    tb = MERGE_BLOCK
    row = lambda w: pl.BlockSpec((tb, w), lambda i: (i, 0))
    ctx, lat = geo.split_specs(tb, DA_WIDTH)
    lay = lambda a: _layer_spec(a, layer, single_buffer=True)
    return pl.pallas_call(
        functools.partial(_merge_kernel, nbc=geo.nc // tb),
        grid=(geo.n // tb,),
        in_specs=[row(d), geo.mod_spec(layer, tb), lay(g1), row(POOL_WIDTH), ctx, lat, ctx, lat,
                  lay(wg_b), lay(bg), lay(wbr_b), lay(wo_b), lay(g2), lay(wr), lay(br)],
        out_specs=[row(d), row(d), row(TOP_K), row(TOP_K)],
        out_shape=[jax.ShapeDtypeStruct((geo.n, d), F32),
                   jax.ShapeDtypeStruct((geo.n, d), BF16),
                   jax.ShapeDtypeStruct((geo.n, TOP_K), jnp.int32),
                   jax.ShapeDtypeStruct((geo.n, TOP_K), F32)],
        compiler_params=_cparams(1),
        name="merge_route",
    )(x, mods, g1, a_out, b_ctx, b_lat, c_ctx, c_lat, wg_b, bg, wbr_b, wo_b, g2, wr, br)


def _ffn_kernel(sw_ref, st_ref, sn_ref, nv_ref, xs_hbm, wgu_ref, bgu_ref, wd_ref, bd_ref, ys_hbm,
                xbuf, ybuf, semx, semy):
    i = pl.program_id(0)
    nv = nv_ref[0]
    slot = i % 2
    t = MOE_TILE
    half = D_FF // 2

    def rows_of(step, k):
        return pl.ds(pl.multiple_of(st_ref[step] * t, t), k * t)

    def x_copy(step, buf, k):
        return pltpu.make_async_copy(xs_hbm.at[rows_of(step, k)], xbuf.at[buf, pl.ds(0, k * t)], semx.at[buf])

    def y_copy(step, buf, k):
        return pltpu.make_async_copy(ybuf.at[buf, pl.ds(0, k * t)], ys_hbm.at[rows_of(step, k)], semy.at[buf])

    def for_size(step, fn):
        n = sn_ref[step]
        for k in range(1, MOE_MAX_SUB + 1):
            @pl.when(n == k)
            def _(k=k):
                fn(k)

    def compute(k):
        m = k * t
        x = xbuf[slot, 0:m, :]
        y = None
        for c in range(2):
            gcols = slice(c * half, (c + 1) * half)
            ucols = slice(D_FF + c * half, D_FF + (c + 1) * half)
            gate = _dot(x, wgu_ref[0, :, gcols].astype(BF16)) + bgu_ref[0, :, gcols]
            up = _dot(x, wgu_ref[0, :, ucols].astype(BF16)) + bgu_ref[0, :, ucols]
            gate = jnp.minimum(gate, SWIGLU_LIMIT)
            up = jnp.clip(up, -SWIGLU_LIMIT, SWIGLU_LIMIT)
            act = ((up + 1.0) * gate * _sigmoid(SWIGLU_ALPHA * gate)).astype(BF16)
            part = _dot(act, wd_ref[0, gcols, :].astype(BF16))
            y = part if y is None else y + part
        ybuf[slot, 0:m, :] = (y + bd_ref[0]).astype(BF16)
        y_copy(i, slot, k).start()

    @pl.when(i < nv)
    def _():
        @pl.when(i == 0)
        def _():
            for_size(0, lambda k: x_copy(0, 0, k).start())

        for_size(i, lambda k: x_copy(i, slot, k).wait())

        @pl.when(i + 1 < nv)
        def _():
            for_size(i + 1, lambda k: x_copy(i + 1, 1 - slot, k).start())

        @pl.when(i >= 2)
        def _():
            for_size(i - 2, lambda k: y_copy(i - 2, slot, k).wait())

        for_size(i, compute)

        @pl.when(i == nv - 1)
        def _():
            for_size(i, lambda k: y_copy(i, slot, k).wait())

            @pl.when(i >= 1)
            def _():
                for_size(i - 1, lambda k: y_copy(i - 1, 1 - slot, k).wait())


def _ffn(xs, step_weight, step_tile, step_sub, n_valid, w_gu, b_gu, w_down, b_down):
    p = xs.shape[0]
    d = D_MODEL
    t = MOE_TILE
    hbm = pl.BlockSpec(memory_space=pl.ANY)
    wmap = lambda i, sw, st, sn, nv: (sw[i], 0, 0)
    grid_spec = pltpu.PrefetchScalarGridSpec(
        num_scalar_prefetch=4,
        grid=(step_weight.shape[0],),
        in_specs=[hbm,
                  pl.BlockSpec((1, d, 2 * D_FF), wmap), pl.BlockSpec((1, 1, 2 * D_FF), wmap),
                  pl.BlockSpec((1, D_FF, d), wmap), pl.BlockSpec((1, 1, d), wmap)],
        out_specs=hbm,
        scratch_shapes=[pltpu.VMEM((2, MOE_MAX_SUB * t, d), BF16), pltpu.VMEM((2, MOE_MAX_SUB * t, d), BF16),
                        pltpu.SemaphoreType.DMA((2,)), pltpu.SemaphoreType.DMA((2,))])
    return pl.pallas_call(
        _ffn_kernel,
        grid_spec=grid_spec,
        out_shape=jax.ShapeDtypeStruct((p, d), BF16),
        compiler_params=_cparams(1),
        name="moe_experts",
    )(step_weight, step_tile, step_sub, n_valid, xs, w_gu, b_gu, w_down, b_down)


def _moe_rows(n, n_experts):
    worst = n * TOP_K + n_experts * ((n // DISPATCH_BLOCK) * (RUN_ALIGN - 1) + MOE_TILE - 1)
    return -(-worst // MOE_TILE) * MOE_TILE


def _route(top_i, n_experts):
    n = top_i.shape[0]
    t = MOE_TILE
    nb = n // DISPATCH_BLOCK
    experts = jnp.arange(n_experts, dtype=jnp.int32)
    onehot = top_i.reshape(nb, DISPATCH_BLOCK * TOP_K)[:, :, None] == experts[None, None, :]
    n_be = jnp.sum(onehot, axis=1, dtype=jnp.int32)
    cap = (n_be + RUN_ALIGN - 1) // RUN_ALIGN * RUN_ALIGN
    rows_e = jnp.sum(cap, axis=0)
    tiles_e = (rows_e + t - 1) // t
    tile_end = jnp.cumsum(tiles_e)
    tile_start = tile_end - tiles_e
    run_dst = tile_start[None, :] * t + jnp.cumsum(cap, axis=0) - cap
    run_loc = jnp.cumsum(cap, axis=1) - cap
    tail_dst = tile_start * t + rows_e
    tail = tiles_e * t - rows_e

    def owner(ends, ids):
        return jnp.sum(ends[None, :] <= ids[:, None], axis=1, dtype=jnp.int32)[:, None] == experts[None, :]

    pick = lambda oh, table: jnp.sum(jnp.where(oh, table[None, :], 0), axis=1)

    steps_e = (tiles_e + MOE_MAX_SUB - 1) // MOE_MAX_SUB
    step_end = jnp.cumsum(steps_e)
    n_steps = _moe_rows(n, n_experts) // (t * MOE_MAX_SUB) + n_experts
    n_valid = step_end[-1]
    step_ids = jnp.minimum(jnp.arange(n_steps, dtype=jnp.int32), n_valid - 1)
    step_oh = owner(step_end, step_ids)
    j = step_ids - pick(step_oh, step_end - steps_e)
    step_expert = jnp.argmax(step_oh, axis=1).astype(jnp.int32)
    n_st = jnp.maximum(pick(step_oh, steps_e), 1)
    n_ti = pick(step_oh, tiles_e)
    base, rem = n_ti // n_st, n_ti % n_st
    extra = jnp.maximum(j - (n_st - rem), 0)
    step_tile = pick(step_oh, tile_start) + j * base + extra
    step_sub = jnp.clip(base + (j >= n_st - rem).astype(jnp.int32), 1, MOE_MAX_SUB)
    units = lambda a: (a // RUN_ALIGN).reshape(-1).astype(jnp.int32)
    runs = (units(run_dst), units(run_loc), units(cap), units(jnp.sum(cap, axis=1)), units(tail_dst), units(tail))
    cols = (run_dst.reshape(nb, n_experts, 1).astype(F32), run_loc.reshape(nb, n_experts, 1).astype(F32))
    steps = (step_expert, step_tile.astype(jnp.int32), step_sub.astype(jnp.int32),
             n_valid.reshape(1).astype(jnp.int32))
    return runs, cols, steps


def _sized_copies(units, src_at, dst_at, sem, max_bits, fn):
    for bit in range(max_bits):
        size = RUN_ALIGN << bit

        @pl.when((units >> bit) & 1 == 1)
        def _(bit=bit, size=size):
            off = (units & ((1 << bit) - 1)) * RUN_ALIGN
            fn(pltpu.make_async_copy(src_at(off, size), dst_at(off, size), sem))


def _dispatch_kernel(rdst_ref, rloc_ref, rcap_ref, rtot_ref, tdst_ref, tlen_ref,
                     h_ref, tt_ref, dcol_ref, lcol_ref, xs_hbm, slot_ref, stage, zeros, sem, zsem,
                     *, n_experts):
    b = pl.program_id(0)
    nb = pl.num_programs(0)
    tb = DISPATCH_BLOCK
    buf = b % 2
    run_bits = (tb // RUN_ALIGN).bit_length()
    tail_bits = (MOE_TILE // RUN_ALIGN).bit_length()
    stage_bits = (stage.shape[1] // RUN_ALIGN).bit_length()

    def drain_runs(blk, slot):
        _sized_copies(rtot_ref[blk],
                      lambda off, size: stage.at[slot, pl.ds(0, size)],
                      lambda off, size: xs_hbm.at[pl.ds(0, size)],
                      sem.at[slot], stage_bits, lambda cp: cp.wait())

    def run_copies(blk, slot, fn):
        def body(e, carry):
            idx = blk * n_experts + e
            src0 = pl.multiple_of(rloc_ref[idx] * RUN_ALIGN, RUN_ALIGN)
            dst0 = pl.multiple_of(rdst_ref[idx] * RUN_ALIGN, RUN_ALIGN)
            _sized_copies(rcap_ref[idx],
                          lambda off, size: stage.at[slot, pl.ds(pl.multiple_of(src0 + off, RUN_ALIGN), size)],
                          lambda off, size: xs_hbm.at[pl.ds(pl.multiple_of(dst0 + off, RUN_ALIGN), size)],
                          sem.at[slot], run_bits, fn)
            return carry
        lax.fori_loop(0, n_experts, body, 0)

    def tail_copies(fn):
        def body(e, carry):
            dst0 = pl.multiple_of(tdst_ref[e] * RUN_ALIGN, RUN_ALIGN)
            _sized_copies(tlen_ref[e],
                          lambda off, size: zeros.at[pl.ds(0, size)],
                          lambda off, size: xs_hbm.at[pl.ds(pl.multiple_of(dst0 + off, RUN_ALIGN), size)],
                          zsem, tail_bits, fn)
            return carry
        lax.fori_loop(0, n_experts, body, 0)

    @pl.when(b == 0)
    def _():
        zeros[...] = jnp.zeros(zeros.shape, zeros.dtype)
        tail_copies(lambda cp: cp.start())

    @pl.when(b >= 2)
    def _():
        drain_runs(b - 2, buf)

    tt = tt_ref[...]
    e_iota = lax.broadcasted_iota(jnp.int32, (n_experts, tb), 0)
    hit = [e_iota == tt[k:k + 1, :] for k in range(TOP_K)]
    member = jnp.where(hit[0] | hit[1] | hit[2] | hit[3], 1.0, 0.0)
    before = (lax.broadcasted_iota(jnp.int32, (tb, tb), 0)
              < lax.broadcasted_iota(jnp.int32, (tb, tb), 1))
    rank = _dot(member.astype(BF16), jnp.where(before, 1.0, 0.0).astype(BF16))
    loc = [jnp.sum(jnp.where(hit[k], lcol_ref[0] + rank, 0.0), axis=0, keepdims=True) for k in range(TOP_K)]
    for k in range(TOP_K):
        slot_ref[k:k + 1, :] = jnp.sum(jnp.where(hit[k], dcol_ref[0] + rank, 0.0),
                                       axis=0, keepdims=True).astype(jnp.int32)

    rows = stage.shape[1]
    r_iota = lax.broadcasted_iota(jnp.int32, (rows, tb), 0).astype(F32)
    place = jnp.zeros((rows, tb), F32)
    for k in range(TOP_K):
        place = jnp.where(r_iota == loc[k], 1.0, place)
    stage[buf] = _dot(place.astype(BF16), h_ref[...]).astype(BF16)
    run_copies(b, buf, lambda cp: cp.start())

    @pl.when(b == nb - 1)
    def _():
        drain_runs(b, buf)

        @pl.when(b >= 1)
        def _():
            drain_runs(b - 1, 1 - buf)
        tail_copies(lambda cp: cp.wait())


def _dispatch(h2, top_t, runs, cols, n_experts):
    n, d = h2.shape
    tb = DISPATCH_BLOCK
    stage_rows = tb * TOP_K + n_experts * RUN_ALIGN
    col = pl.BlockSpec((1, n_experts, 1), lambda b, *_: (b, 0, 0))
    grid_spec = pltpu.PrefetchScalarGridSpec(
        num_scalar_prefetch=6,
        grid=(n // tb,),
        in_specs=[pl.BlockSpec((tb, d), lambda b, *_: (b, 0)),
                  pl.BlockSpec((TOP_K, tb), lambda b, *_: (0, b)), col, col],
        out_specs=[pl.BlockSpec(memory_space=pl.ANY), pl.BlockSpec((TOP_K, tb), lambda b, *_: (0, b))],
        scratch_shapes=[pltpu.VMEM((2, stage_rows, d), BF16), pltpu.VMEM((MOE_TILE, d), BF16),
                        pltpu.SemaphoreType.DMA((2,)), pltpu.SemaphoreType.DMA(())])
    return pl.pallas_call(
        functools.partial(_dispatch_kernel, n_experts=n_experts),
        grid_spec=grid_spec,
        out_shape=[jax.ShapeDtypeStruct((_moe_rows(n, n_experts), d), BF16),
                   jax.ShapeDtypeStruct((TOP_K, n), jnp.int32)],
        compiler_params=_cparams(1),
        name="moe_dispatch",
    )(*runs, h2, top_t, *cols)


def _moe_sum(x_ref, mod_ref, yg_ref, p_ref):
    d = D_MODEL
    gate2 = mod_ref[0][:, 5 * d:6 * d]
    p = p_ref[...]
    moe = p[:, 0:1] * yg_ref[0].astype(F32)
    for kk in range(1, TOP_K):
        moe = moe + p[:, kk:kk + 1] * yg_ref[kk].astype(F32)
    return x_ref[...] + gate2 * moe


def _combine_final_kernel(x_ref, mod_ref, yg_ref, p_ref, gf_ref, yc_ref, yl_ref, *, nbc):
    i = pl.program_id(0)
    y = _rms(_moe_sum(x_ref, mod_ref, yg_ref, p_ref)) * gf_ref[...]

    @pl.when(i < nbc)
    def _():
        yc_ref[...] = y

    @pl.when(i >= nbc)
    def _():
        yl_ref[...] = y


def _combine(geo, layer, x, mods, yg, top_p, final_g):
    d = D_MODEL
    tb = TOKEN_BLOCK
    row = pl.BlockSpec((tb, d), lambda i: (i, 0))
    in_specs = [row, geo.mod_spec(layer, tb),
                pl.BlockSpec((TOP_K, tb, d), lambda i: (0, i, 0)),
                pl.BlockSpec((tb, TOP_K), lambda i: (i, 0))]
    ctx, lat = geo.split_specs(tb, d)
    return pl.pallas_call(
        functools.partial(_combine_final_kernel, nbc=geo.nc // tb),
        grid=(geo.n // tb,),
        in_specs=in_specs + [pl.BlockSpec(final_g.shape, lambda i: (0, 0))],
        out_specs=[ctx, lat],
        out_shape=[jax.ShapeDtypeStruct((geo.nc, d), F32), jax.ShapeDtypeStruct((geo.nl, d), F32)],
        compiler_params=_cparams(1), name="moe_combine_final",
    )(x, mods, yg, top_p, final_g)


@functools.lru_cache(maxsize=None)
def _rope_tables(length):
    rows = length // GRID_W
    row = np.repeat(np.arange(rows), GRID_W).astype(np.float32)
    col = np.tile(np.arange(GRID_W), rows).astype(np.float32)
    ax = DA_HEAD_DIM // 2
    inv = (ROPE_BASE ** (-(np.arange(ax // 2, dtype=np.float32) * 2.0 / ax))).astype(np.float32)
    ang_r = (row[:, None] * inv).astype(np.float32)
    ang_c = (col[:, None] * inv).astype(np.float32)
    ang = np.concatenate([ang_r, ang_r, ang_c, ang_c], axis=-1).astype(np.float64)
    sign = np.where((np.arange(DA_HEAD_DIM) % 32) < 16, -1.0, 1.0)
    reps = DA_WIDTH // DA_HEAD_DIM
    cos = np.tile(np.cos(ang), (1, reps)).astype(np.float32)
    sin_signed = np.tile(np.sin(ang) * sign[None, :], (1, reps)).astype(np.float32)
    return jnp.asarray(cos), jnp.asarray(sin_signed)


def kernel(x_prompt, x_sample, cache_k, cache_v, c, c_ctx, w_ada, b_ada, norm1, norm2, w_in, w_pool, pool_scale, da_lambda, da_subln, hy_conv_w, hy_conv_b, hy_f_w1, hy_f_b1, hy_f_w2, hy_f_b2, hy_f_w3, hy_sin_freq, hy_skip, w_branch, w_gate, b_gate, w_o, w_router, b_router, w_gu, b_gu, w_down, b_down, final_norm):
    bc, lc, d = x_prompt.shape
    bl, ll, _ = x_sample.shape
    depth = w_in.shape[0]
    n_experts = w_router.shape[-1]
    past = cache_k.shape[2]
    geo = _Geom(bc, lc, bl, ll)
    assert 1 + bl <= MOD_ROWS

    x = jnp.concatenate([x_prompt.reshape(bc * lc, d), x_sample.reshape(bl * ll, d)], axis=0)
    cond = jnp.concatenate([c_ctx[None], c, jnp.zeros((MOD_ROWS - 1 - bl, d), F32)], axis=0)
    mods = _ada(cond, w_ada, b_ada).reshape(depth * MOD_ROWS, 1, N_MOD * d)
    cos_t, sin_t = _rope_tables(ll)
    ck = cache_k.reshape(bl, depth, past, DA_WIDTH)
    cv = cache_v.reshape(bl, depth, past, DA_WIDTH)

    row3 = lambda a: a.reshape(depth, 1, a.shape[-1])
    g1, g2 = row3(norm1), row3(norm2)
    w_in_b = w_in.astype(BF16)
    w_pool_b = w_pool.astype(BF16)
    w_gate_b = w_gate.astype(BF16)
    w_branch_b = w_branch.astype(BF16)
    w_o_b = w_o.astype(BF16)
    w1p = jnp.pad(hy_f_w1, ((0, 0), (0, 64 - HY_EMB), (0, 0)))
    fargs = (w1p, row3(hy_f_b1), hy_f_w2, row3(hy_f_b2), hy_f_w3, hy_sin_freq)
    w_gu_s = w_gu.reshape(depth * n_experts, d, 2 * D_FF)
    b_gu_s = b_gu.reshape(depth * n_experts, 1, 2 * D_FF)
    w_down_s = w_down.reshape(depth * n_experts, D_FF, d)
    b_down_s = b_down.reshape(depth * n_experts, 1, d)

    new_k = jnp.zeros((bc, depth, lc * DA_HEADS, DA_VDIM), F32)
    new_v = jnp.zeros((bc, depth, lc * DA_HEADS, DA_VDIM), F32)
    moe = None
    for l in range(depth):
        x, u_pool, q, kb, vb, new_k, new_v, u_hy = _inproj(geo, l, x, mods, g1, w_in_b, cos_t, sin_t,
                                                           new_k, new_v, moe)
        a_out, zb, x0 = _local_mix(geo, l, u_pool, w_pool_b, row3(pool_scale),
                                   u_hy, hy_conv_w, row3(hy_conv_b))

        lam_init = 0.8 - 0.6 * math.exp(-0.3 * l)
        subln = row3(da_subln)
        b_ctx = _attn_ctx(geo, l, q, kb, vb, da_lambda, subln, lam_init)
        b_lat = _attn_lat(geo, l, q, kb, vb, ck, cv, da_lambda, subln, lam_init)

        skip = row3(hy_skip)
        c_ctx_out = _hyena_long_conv(lc, l, bc, 0, zb, x0, _hyfilter(lc, l, *fargs), skip)
        c_lat_out = _hyena_long_conv(ll, l, bl, geo.nc // ll, zb, x0, _hyfilter(ll, l, *fargs), skip)

        x, h2, top_i, top_p = _merge(geo, l, x, mods, g1, a_out, b_ctx, b_lat, c_ctx_out, c_lat_out,
                                     w_gate_b, row3(b_gate), w_branch_b, w_o_b, g2, w_router, row3(b_router))

        runs, cols, (step_expert, step_tile, step_sub, n_valid) = _route(top_i, n_experts)
        xs, pair_slot = _dispatch(h2, top_i.T, runs, cols, n_experts)
        ys = _ffn(xs, step_expert + l * n_experts, step_tile, step_sub, n_valid,
                  w_gu_s, b_gu_s, w_down_s, b_down_s)
        yg = jnp.take(ys, pair_slot, axis=0, mode="clip")
        moe = (yg, top_p)

    y_ctx, y_lat = _combine(geo, depth - 1, x, mods, yg, top_p, final_norm[None])

    return (y_ctx.reshape(bc, lc, d), y_lat.reshape(bl, ll, d),
            new_k.reshape(bc, depth, lc, DA_HEADS, 2 * DA_HEAD_DIM),
            new_v.reshape(bc, depth, lc, DA_HEADS, DA_VDIM))
```
